```python
import jax, jax.numpy as jnp
from jax import lax
import numpy as np


D_MODEL = 2048
BATCH = 1
SEQ = 8192
DEPTH = 2

HEAD_DIM = 128
SB_HEADS = 6
RET_HEADS = 4
DSA_HEADS = 6
D_SB = SB_HEADS * HEAD_DIM
D_RET = RET_HEADS * HEAD_DIM
D_DSA = DSA_HEADS * HEAD_DIM
D_MIX = D_SB + D_RET + D_DSA
KV_RANK = 256
IDX_HEADS = 8
IDX_DIM = 64
IDX_SCALE = IDX_DIM ** -0.5 * IDX_HEADS ** -0.5
TOPK_MAX = 256
Q_BLOCK = 128
RET_CHUNK = 128
N_EXPERTS = 16
N_GROUPS = 4
EXPERTS_PER_GROUP = N_EXPERTS // N_GROUPS
TOP_K_EXPERTS = 2
D_FF_EXPERT = 512
LN_EPS = 1e-5
RMS_EPS = 1e-6
GN_EPS = 1e-6
DEEPNORM_ALPHA = (2 * DEPTH) ** 0.25
DEEPNORM_BETA = (8 * DEPTH) ** -0.25
SPLITS = (D_SB, D_SB, D_SB,
          D_RET, D_RET, D_RET, D_RET,
          D_DSA, KV_RANK, IDX_HEADS * IDX_DIM, IDX_DIM, IDX_HEADS)
D_IN = sum(SPLITS)

kernel_name = 'hybrid_sb_ret_dsa_grouped_moe_deepnorm'


def layer_norm(x, g, b):
    xf = x.astype(jnp.float32)
    mu = jnp.mean(xf, axis=-1, keepdims=True)
    var = jnp.mean(jnp.square(xf - mu), axis=-1, keepdims=True)
    return ((xf - mu) * lax.rsqrt(var + LN_EPS) * g + b).astype(x.dtype)


def rms_norm(x, g):
    xf = x.astype(jnp.float32)
    y = xf * lax.rsqrt(jnp.mean(jnp.square(xf), axis=-1, keepdims=True) + RMS_EPS) * g
    return y.astype(x.dtype)


def split_columns(p):
    outs, o = [], 0
    for n in SPLITS:
        outs.append(p[..., o:o + n])
        o += n
    return outs


def to_blocks(x):
    b, s = x.shape[:2]
    return jnp.moveaxis(x.reshape((b, s // Q_BLOCK, Q_BLOCK) + x.shape[2:]), 1, 0)


def from_blocks(y):
    y = jnp.moveaxis(y, 0, 1)
    return y.reshape((y.shape[0], y.shape[1] * y.shape[2]) + y.shape[3:])


def stick_breaking_attention(q, k, v):
    b, s, h, d = q.shape
    scale = d ** -0.5
    kf = k.astype(jnp.float32)
    vf = v.astype(jnp.float32)
    key_pos = jnp.arange(s)

    def block(args):
        qblk, i = args
        t = i * Q_BLOCK + jnp.arange(Q_BLOCK)
        strict = key_pos[None, :] < t[:, None]
        z = jnp.einsum('bqhd,bkhd->bhqk', qblk.astype(jnp.float32), kf) * scale
        log_rem = jnp.where(strict, jax.nn.log_sigmoid(-z), 0.0)
        after = lax.cumsum(log_rem, axis=3, reverse=True) - log_rem
        a = jnp.where(strict, jnp.exp(jax.nn.log_sigmoid(z) + after), 0.0)
        return jnp.einsum('bhqk,bkhd->bqhd', a, vf)

    nb = s // Q_BLOCK
    out = lax.map(block, (to_blocks(q), jnp.arange(nb)))
    return from_blocks(out).reshape(b, s, h * d).astype(q.dtype)


def rotary(x, pos):
    half = x.shape[-1] // 2
    theta = 10000.0 ** (-jnp.linspace(0.0, 1.0, half, dtype=jnp.float32))
    ang = pos[:, None] * theta[None, :]
    cos = jnp.cos(ang)[None, :, None, :]
    sin = jnp.sin(ang)[None, :, None, :]
    x1, x2 = x[..., :half], x[..., half:]
    return jnp.concatenate([x1 * cos - x2 * sin, x1 * sin + x2 * cos], axis=-1)


def retention(q, k, v, g, gn_g):
    b, s, h, d = q.shape
    pos = jnp.arange(s, dtype=jnp.float32)
    qf = rotary(q.astype(jnp.float32), pos)
    kf = rotary(k.astype(jnp.float32), pos) * d ** -0.5
    vf = v.astype(jnp.float32)
    log_gamma = jnp.log1p(-(2.0 ** (-5.0 - jnp.arange(h, dtype=jnp.float32))))
    c = RET_CHUNK
    nc = s // c
    idx = jnp.arange(c, dtype=jnp.float32)
    diff = idx[:, None] - idx[None, :]
    intra = jnp.where(diff >= 0, jnp.exp(jnp.maximum(diff, 0.0)[None] * log_gamma[:, None, None]), 0.0)
    q_decay = jnp.exp((idx[None, :] + 1.0) * log_gamma[:, None])
    k_decay = jnp.exp((c - 1.0 - idx[None, :]) * log_gamma[:, None])
    chunk_decay = jnp.exp(c * log_gamma)

    def chunks(x):
        return x.reshape(b, nc, c, h, d).transpose(1, 0, 3, 2, 4)

    def step(state, inp):
        qc, kc, vc = inp
        scores = jnp.einsum('bhid,bhjd->bhij', qc, kc) * intra[None]
        o = (jnp.einsum('bhij,bhje->bhie', scores, vc)
             + jnp.einsum('bhid,bhde->bhie', qc * q_decay[None, :, :, None], state))
        state = (chunk_decay[None, :, None, None] * state
                 + jnp.einsum('bhjd,bhje->bhde', kc * k_decay[None, :, :, None], vc))
        return state, o

    state0 = jnp.zeros((b, h, d, d), jnp.float32)
    _, outs = lax.scan(step, state0, (chunks(qf), chunks(kf), chunks(vf)))
    o = outs.transpose(1, 0, 3, 2, 4).reshape(b, s, h, d)
    mu = jnp.mean(o, axis=-1, keepdims=True)
    var = jnp.mean(jnp.square(o - mu), axis=-1, keepdims=True)
    o = ((o - mu) * lax.rsqrt(var + GN_EPS)).reshape(b, s, h * d) * gn_g
    return (jax.nn.silu(g.astype(jnp.float32)) * o).astype(q.dtype)


def dsa_attention(q, c_kv, q_idx, k_idx, w_idx, kv_norm_g, w_kv_up):
    b, s, h, d = q.shape
    topk = min(TOPK_MAX, s // 4)
    kv = jnp.einsum('bsr,rn->bsn', rms_norm(c_kv, kv_norm_g), w_kv_up).reshape(b, s, h, 2, d)
    k = kv[..., 0, :]
    v = kv[..., 1, :]
    k_idx_f = k_idx.astype(jnp.float32)
    key_pos = jnp.arange(s)
    gather = jax.vmap(lambda xx, ii: xx[ii])

    def block(args):
        qb, qib, wb, i = args
        t = i * Q_BLOCK + jnp.arange(Q_BLOCK)
        causal = key_pos[None, :] <= t[:, None]
        rel = jax.nn.relu(jnp.einsum('bqhd,bkd->bqhk', qib.astype(jnp.float32), k_idx_f))
        score = jnp.einsum('bqhk,bqh->bqk', rel, wb.astype(jnp.float32)) * IDX_SCALE
        score = jnp.where(causal[None], score, -jnp.inf)
        _, sel = lax.top_k(score, topk)
        valid = sel <= t[None, :, None]
        k_sel = gather(k, sel).astype(jnp.float32)
        v_sel = gather(v, sel).astype(jnp.float32)
        logits = jnp.einsum('bqhd,bqkhd->bhqk', qb.astype(jnp.float32), k_sel) * d ** -0.5
        logits = jnp.where(valid[:, None], logits, -jnp.inf)
        p = jax.nn.softmax(logits, axis=-1)
        return jnp.einsum('bhqk,bqkhd->bqhd', p, v_sel)

    nb = s // Q_BLOCK
    out = lax.map(block, (to_blocks(q), to_blocks(q_idx), to_blocks(w_idx), jnp.arange(nb)))
    return from_blocks(out).reshape(b, s, h * d).astype(q.dtype)


def grouped_moe(x, w_router, w_gate, w_up, w_down):
    logits = jnp.einsum('bsd,de->bse', x.astype(jnp.float32), w_router.astype(jnp.float32))
    probs = jax.nn.softmax(logits, axis=-1)
    grouped = probs.reshape(probs.shape[:2] + (N_GROUPS, EXPERTS_PER_GROUP))
    group_score = jnp.sum(lax.top_k(grouped, TOP_K_EXPERTS)[0], axis=-1)
    g_sel = jnp.argmax(group_score, axis=-1)
    in_group = jnp.take_along_axis(grouped, g_sel[..., None, None], axis=2)[..., 0, :]
    top_w, top_i = lax.top_k(in_group, TOP_K_EXPERTS)
    top_w = top_w / jnp.sum(top_w, axis=-1, keepdims=True)
    expert_ids = g_sel[..., None] * EXPERTS_PER_GROUP + top_i
    gates = jnp.sum(jax.nn.one_hot(expert_ids, N_EXPERTS, dtype=jnp.float32) * top_w[..., None], axis=-2)
    hg = jnp.einsum('bsd,edf->bsef', x, w_gate)
    hu = jnp.einsum('bsd,edf->bsef', x, w_up)
    act = jax.nn.silu(hg) * hu * gates[..., None].astype(x.dtype)
    return jnp.einsum('bsef,efd->bsd', act, w_down)


def setup_inputs(seed: int = 0) -> dict:
    key = jax.random.key(seed)
    ks = jax.random.split(key, 16)
    f32 = jnp.float32
    nrm = lambda k, shp, sc: jax.random.normal(k, shp, f32) * sc
    return {
        'x': nrm(ks[0], (BATCH, SEQ, D_MODEL), 1.0),
        'w_in': nrm(ks[1], (DEPTH, D_MODEL, D_IN), D_MODEL ** -0.5),
        'w_kv_up': nrm(ks[2], (DEPTH, KV_RANK, DSA_HEADS * 2 * HEAD_DIM), KV_RANK ** -0.5),
        'kv_norm_g': 1.0 + nrm(ks[3], (DEPTH, KV_RANK), 0.02),
        'ret_gn_g': 1.0 + nrm(ks[4], (DEPTH, D_RET), 0.02),
        'w_o': nrm(ks[5], (DEPTH, D_MIX, D_MODEL), D_MIX ** -0.5 * DEEPNORM_BETA),
        'ln1_g': 1.0 + nrm(ks[6], (DEPTH, D_MODEL), 0.02),
        'ln1_b': nrm(ks[7], (DEPTH, D_MODEL), 0.02),
        'w_router': nrm(ks[8], (D_MODEL, N_EXPERTS), D_MODEL ** -0.5),
        'w_gate': nrm(ks[9], (DEPTH, N_EXPERTS, D_MODEL, D_FF_EXPERT), D_MODEL ** -0.5),
        'w_up': nrm(ks[10], (DEPTH, N_EXPERTS, D_MODEL, D_FF_EXPERT), D_MODEL ** -0.5),
        'w_down': nrm(ks[11], (DEPTH, N_EXPERTS, D_FF_EXPERT, D_MODEL), D_FF_EXPERT ** -0.5 * DEEPNORM_BETA),
        'ln2_g': 1.0 + nrm(ks[12], (DEPTH, D_MODEL), 0.02),
        'ln2_b': nrm(ks[13], (DEPTH, D_MODEL), 0.02),
    }


def reference(x, w_in, w_kv_up, kv_norm_g, ret_gn_g, w_o, ln1_g, ln1_b, w_router,
              w_gate, w_up, w_down, ln2_g, ln2_b):
    b, s, _ = x.shape
    for l in range(DEPTH):
        proj = jnp.einsum('bsd,dn->bsn', x, w_in[l])
        (sb_q, sb_k, sb_v, r_q, r_k, r_v, r_g,
         d_q, d_ckv, d_qi, d_ki, d_w) = split_columns(proj)
        heads = lambda t, n: t.reshape(b, s, n, HEAD_DIM)
        y_sb = stick_breaking_attention(heads(sb_q, SB_HEADS), heads(sb_k, SB_HEADS), heads(sb_v, SB_HEADS))
        y_ret = retention(heads(r_q, RET_HEADS), heads(r_k, RET_HEADS), heads(r_v, RET_HEADS), r_g, ret_gn_g[l])
        y_dsa = dsa_attention(heads(d_q, DSA_HEADS), d_ckv, d_qi.reshape(b, s, IDX_HEADS, IDX_DIM),
                              d_ki, d_w, kv_norm_g[l], w_kv_up[l])
        mix = jnp.einsum('bsm,md->bsd', jnp.concatenate([y_sb, y_ret, y_dsa], axis=-1), w_o[l])
        x = layer_norm(DEEPNORM_ALPHA * x + mix, ln1_g[l], ln1_b[l])
        ffn = grouped_moe(x, w_router, w_gate[l], w_up[l], w_down[l])
        x = layer_norm(DEEPNORM_ALPHA * x + ffn, ln2_g[l], ln2_b[l])
    return x
```

```python
import functools

import numpy as np
import jax
import jax.numpy as jnp
from jax import lax
from jax.experimental import pallas as pl
from jax.experimental.pallas import tpu as pltpu

F32 = jnp.float32
BF16 = jnp.bfloat16
I32 = jnp.int32

D_MODEL = 2048
HEAD_DIM = 128
SB_HEADS = 6
RET_HEADS = 4
DSA_HEADS = 6
D_SB = SB_HEADS * HEAD_DIM
D_RET = RET_HEADS * HEAD_DIM
D_DSA = DSA_HEADS * HEAD_DIM
KV_RANK = 256
IDX_HEADS = 8
IDX_DIM = 64
IDX_SCALE = IDX_DIM ** -0.5 * IDX_HEADS ** -0.5
TOPK_MAX = 256
BLK = 128
N_EXPERTS = 16
N_GROUPS = 4
EXPERTS_PER_GROUP = N_EXPERTS // N_GROUPS
D_FF_EXPERT = 512
LN_EPS = 1e-5
RMS_EPS = 1e-6
GN_EPS = 1e-6
DEPTH = 2
DEEPNORM_ALPHA = (2 * DEPTH) ** 0.25

OFF_SB_Q = 0
OFF_SB_K = OFF_SB_Q + D_SB
OFF_SB_V = OFF_SB_K + D_SB
OFF_D_Q = OFF_SB_V + D_SB
OFF_R_Q = OFF_D_Q + D_DSA
OFF_R_K = OFF_R_Q + D_RET
OFF_R_V = OFF_R_K + D_RET
OFF_R_G = OFF_R_V + D_RET
OFF_D_QI = OFF_R_G + D_RET
OFF_D_CKV = OFF_D_QI + IDX_HEADS * IDX_DIM
OFF_TAIL = OFF_D_CKV + KV_RANK
D_PROJ = 6144

KEY_CHUNK = 512
VMEM_LIMIT = 56 * 1024 * 1024
NEG_BIG = -1e30
EXP_UNDERFLOW = -104.0
KEY_OF_NEG_INF = -2139095041
INT_MIN = -2147483648


def _dot(a, b):
    return jnp.dot(a, b, preferred_element_type=F32)


def _dot_nt(a, b):
    return lax.dot_general(a, b, (((1,), (1,)), ((), ())), preferred_element_type=F32)


def _dot_tn(a, b):
    return lax.dot_general(a, b, (((0,), (0,)), ((), ())), preferred_element_type=F32)


def _params(*sem):
    return pltpu.CompilerParams(dimension_semantics=sem, vmem_limit_bytes=VMEM_LIMIT)


def _resident(shape, index_map):
    return pl.BlockSpec(shape, index_map, pipeline_mode=pl.Buffered(1))


def _proj_kernel(x_ref, w_ref, o_ref):
    o_ref[...] = _dot(x_ref[...].astype(BF16), w_ref[...]).astype(o_ref.dtype)


def _proj(x, w):
    s, d = x.shape
    n = w.shape[1]
    tm = min(1024, s)
    tn = 1536
    return pl.pallas_call(
        _proj_kernel,
        grid=(s // tm, n // tn),
        in_specs=[pl.BlockSpec((tm, d), lambda i, j: (i, 0)),
                  pl.BlockSpec((d, tn), lambda i, j: (0, j))],
        out_specs=pl.BlockSpec((tm, tn), lambda i, j: (i, j)),
        out_shape=jax.ShapeDtypeStruct((s, n), BF16),
        compiler_params=_params("parallel", "arbitrary"),
        name="proj",
    )(x, w)


def _sb_kernel(q_ref, k_ref, v_ref, o_ref):
    i = pl.program_id(1)
    q = q_ref[...]
    scale = HEAD_DIM ** -0.5
    row = lax.broadcasted_iota(I32, (BLK, BLK), 0)
    col = lax.broadcasted_iota(I32, (BLK, BLK), 1)
    later = (row > col).astype(BF16)

    def cond(carry):
        j, c, _ = carry
        return jnp.logical_and(j >= 0, jnp.max(c) > EXP_UNDERFLOW)

    def body(carry):
        j, c, acc = carry
        off = pl.multiple_of(j * BLK, BLK)
        kb = k_ref[pl.ds(off, BLK), :]
        vb = v_ref[pl.ds(off, BLK), :]
        z = _dot_nt(q, kb) * scale
        strict = (off + col) < (i * BLK + row)
        sp = jnp.maximum(z, 0.0) + jnp.log1p(jnp.exp(-jnp.abs(z)))
        log_rem = jnp.where(strict, -sp, 0.0)
        hi = log_rem.astype(BF16)
        lo = (log_rem - hi.astype(F32)).astype(BF16)
        after = _dot(hi, later) + _dot(lo, later)
        a = jnp.where(strict, jnp.exp(z - sp + after + c), 0.0)
        acc = acc + _dot(a.astype(BF16), vb)
        c = c + jnp.sum(log_rem, axis=1, keepdims=True)
        return j - 1, c, acc

    _, _, acc = lax.while_loop(
        cond, body, (i, jnp.zeros((BLK, 1), F32), jnp.zeros((BLK, HEAD_DIM), F32)))
    o_ref[...] = acc.astype(o_ref.dtype)


def _stick_breaking(p):
    s = p.shape[0]
    nb = s // BLK
    qb, kb, vb = OFF_SB_Q // HEAD_DIM, OFF_SB_K // HEAD_DIM, OFF_SB_V // HEAD_DIM
    return pl.pallas_call(
        _sb_kernel,
        grid=(SB_HEADS, nb),
        in_specs=[pl.BlockSpec((BLK, HEAD_DIM), lambda h, i: (i, qb + h)),
                  pl.BlockSpec((s, HEAD_DIM), lambda h, i: (0, kb + h)),
                  pl.BlockSpec((s, HEAD_DIM), lambda h, i: (0, vb + h))],
        out_specs=pl.BlockSpec((BLK, HEAD_DIM), lambda h, i: (i, h)),
        out_shape=jax.ShapeDtypeStruct((s, D_SB), BF16),
        compiler_params=_params("parallel", "arbitrary"),
        name="stick_breaking",
    )(p, p, p)


def _ret_kernel(q_ref, k_ref, v_ref, g_ref, cos_ref, sin_ref, intra_ref, qd_ref, kd_ref, cd_ref,
                gn_ref, o_ref, state_ref):
    n = pl.program_id(1)

    @pl.when(n == 0)
    def _():
        state_ref[...] = jnp.zeros_like(state_ref)

    cos = cos_ref[...]
    sin = sin_ref[...]
    q = q_ref[...].astype(F32)
    k = k_ref[...].astype(F32)
    v = v_ref[...]
    qr = q * cos + pltpu.roll(q, HEAD_DIM // 2, 1) * sin
    kr = (k * cos + pltpu.roll(k, HEAD_DIM // 2, 1) * sin) * (HEAD_DIM ** -0.5)
    scores = _dot_nt(qr.astype(BF16), kr.astype(BF16)) * intra_ref[0]
    state = state_ref[...]
    o = (_dot(scores.astype(BF16), v)
         + _dot((qr * qd_ref[0]).astype(BF16), state.astype(BF16)))
    state_ref[...] = cd_ref[0] * state + _dot_tn((kr * kd_ref[0]).astype(BF16), v)
    mu = jnp.mean(o, axis=1, keepdims=True)
    var = jnp.mean(jnp.square(o - mu), axis=1, keepdims=True)
    on = (o - mu) * lax.rsqrt(var + GN_EPS) * gn_ref[...]
    g = g_ref[...].astype(F32)
    o_ref[...] = (g * jax.nn.sigmoid(g) * on).astype(o_ref.dtype)


def _retention_tables(s):
    half = HEAD_DIM // 2
    pos = jnp.arange(s, dtype=F32)
    theta = 10000.0 ** (-jnp.linspace(0.0, 1.0, half, dtype=F32))
    ang = pos[:, None] * theta[None, :]
    cos, sin = jnp.cos(ang), jnp.sin(ang)
    cos2 = jnp.concatenate([cos, cos], axis=1)
    sin2 = jnp.concatenate([-sin, sin], axis=1)
    log_gamma = jnp.log1p(-(2.0 ** (-5.0 - jnp.arange(RET_HEADS, dtype=F32))))
    idx = jnp.arange(BLK, dtype=F32)
    diff = idx[:, None] - idx[None, :]
    intra = jnp.where(diff >= 0, jnp.exp(jnp.maximum(diff, 0.0)[None] * log_gamma[:, None, None]), 0.0)
    q_decay = jnp.exp((idx[None, :] + 1.0) * log_gamma[:, None])
    k_decay = jnp.exp((BLK - 1.0 - idx[None, :]) * log_gamma[:, None])
    chunk_decay = jnp.exp(BLK * log_gamma)
    full = (RET_HEADS, BLK, HEAD_DIM)
    return (cos2, sin2, intra,
            jnp.broadcast_to(q_decay[:, :, None], full),
            jnp.broadcast_to(k_decay[:, :, None], full),
            jnp.broadcast_to(chunk_decay[:, None, None], full))


def _retention(p, gn_g, tables):
    s = p.shape[0]
    nc = s // BLK
    cos2, sin2, intra, qd, kd, cd = tables
    col = lambda off: (lambda h, n: (n, off // HEAD_DIM + h))
    per_head = pl.BlockSpec((1, BLK, HEAD_DIM), lambda h, n: (h, 0, 0))
    pos_spec = pl.BlockSpec((BLK, HEAD_DIM), lambda h, n: (n, 0))
    return pl.pallas_call(
        _ret_kernel,
        grid=(RET_HEADS, nc),
        in_specs=[pl.BlockSpec((BLK, HEAD_DIM), col(OFF_R_Q)),
                  pl.BlockSpec((BLK, HEAD_DIM), col(OFF_R_K)),
                  pl.BlockSpec((BLK, HEAD_DIM), col(OFF_R_V)),
                  pl.BlockSpec((BLK, HEAD_DIM), col(OFF_R_G)),
                  pos_spec, pos_spec, per_head, per_head, per_head, per_head,
                  pl.BlockSpec((1, HEAD_DIM), lambda h, n: (0, h))],
        out_specs=pl.BlockSpec((BLK, HEAD_DIM), lambda h, n: (n, h)),
        out_shape=jax.ShapeDtypeStruct((s, D_RET), BF16),
        scratch_shapes=[pltpu.VMEM((HEAD_DIM, HEAD_DIM), F32)],
        compiler_params=_params("parallel", "arbitrary"),
        name="retention",
    )(p, p, p, p, cos2, sin2, intra, qd, kd, cd, gn_g)


def _kv_up_kernel(c_ref, g_ref, w_ref, o_ref):
    c = c_ref[...].astype(F32)
    y = c * lax.rsqrt(jnp.mean(jnp.square(c), axis=1, keepdims=True) + RMS_EPS) * g_ref[...]
    o_ref[...] = _dot(y.astype(BF16), w_ref[...]).astype(o_ref.dtype)


def _kv_up(p, g, w):
    s = p.shape[0]
    tm = min(1024, s)
    n = w.shape[1]
    return pl.pallas_call(
        _kv_up_kernel,
        grid=(s // tm,),
        in_specs=[pl.BlockSpec((tm, KV_RANK), lambda i: (i, OFF_D_CKV // KV_RANK)),
                  pl.BlockSpec((1, KV_RANK), lambda i: (0, 0)),
                  pl.BlockSpec((KV_RANK, n), lambda i: (0, 0))],
        out_specs=pl.BlockSpec((tm, n), lambda i: (i, 0)),
        out_shape=jax.ShapeDtypeStruct((s, n), BF16),
        compiler_params=_params("parallel"),
        name="kv_up",
    )(p, g, w)


def _dsa_kernel(q_ref, qi_ref, tq_ref, tail_ref, kv_ref, o_ref, key_ref, bias_ref, *, topk):
    i = pl.program_id(0)
    n_chunks = ((i + 1) * BLK + KEY_CHUNK - 1) // KEY_CHUNK
    t_row = i * BLK + lax.broadcasted_iota(I32, (BLK, KEY_CHUNK), 0)
    col = lax.broadcasted_iota(I32, (BLK, KEY_CHUNK), 1)

    w_idx = tq_ref[:, IDX_DIM:IDX_DIM + IDX_HEADS].astype(F32) * IDX_SCALE
    qi = qi_ref[...]

    def score_chunk(c, _):
        off = pl.multiple_of(c * KEY_CHUNK, KEY_CHUNK)
        k_idx = tail_ref[pl.ds(off, KEY_CHUNK), 0:IDX_DIM]
        score = jnp.zeros((BLK, KEY_CHUNK), F32)
        for h in range(IDX_HEADS):
            rel = jnp.maximum(_dot_nt(qi[:, h * IDX_DIM:(h + 1) * IDX_DIM], k_idx), 0.0)
            score = score + rel * w_idx[:, h:h + 1]
        score = jnp.where(score == 0.0, 0.0, score)
        score = jnp.where(off + col <= t_row, score, -jnp.inf)
        bits = pltpu.bitcast(score, I32)
        key_ref[c] = bits ^ ((bits >> 31) & 0x7FFFFFFF)
        return 0

    lax.fori_loop(0, n_chunks, score_chunk, 0)

    def count_ge(cand):
        cand_b = jnp.broadcast_to(cand, (BLK, BLK))

        def chunk(c, acc):
            keys = key_ref[c]
            for u in range(KEY_CHUNK // BLK):
                acc = acc + jnp.where(keys[:, u * BLK:(u + 1) * BLK] >= cand_b, 1, 0)
            return acc

        acc = lax.fori_loop(0, n_chunks, chunk, jnp.zeros((BLK, BLK), I32))
        return jnp.sum(acc, axis=1, keepdims=True)

    zero = jnp.zeros((BLK, 1), I32)
    thr = jnp.where(count_ge(zero) >= topk, zero, jnp.full((BLK, 1), INT_MIN, I32))

    def bit_step(b, thr):
        cand = thr | (jnp.int32(1) << (30 - b))
        return jnp.where(count_ge(cand) >= topk, cand, thr)

    thr = lax.fori_loop(0, 31, bit_step, thr)
    n_ge = count_ge(thr)
    real = thr > KEY_OF_NEG_INF
    thr_eff = jnp.maximum(thr, KEY_OF_NEG_INF + 1)

    def bias_chunk(c, _):
        bias_ref[c] = jnp.where(key_ref[c] >= thr_eff, 0.0, NEG_BIG)
        return 0

    lax.fori_loop(0, n_chunks, bias_chunk, 0)

    has_tie = jnp.max(jnp.where(jnp.logical_and(real, n_ge > topk), 1, 0)) > 0

    @pl.when(has_tie)
    def _():
        need = (topk - count_ge(thr + 1)).astype(F32)
        r = lax.broadcasted_iota(I32, (BLK, BLK), 0)
        cc = lax.broadcasted_iota(I32, (BLK, BLK), 1)
        upto = (r <= cc).astype(BF16)

        def tie_chunk(c, run):
            keys = key_ref[c]
            for u in range(KEY_CHUNK // BLK):
                kt = keys[:, u * BLK:(u + 1) * BLK]
                eq = kt == thr
                eqf = jnp.where(eq, 1.0, 0.0)
                rank = _dot(eqf.astype(BF16), upto) + run
                sel = jnp.logical_or(kt > thr, jnp.logical_and(eq, rank <= need))
                bias_ref[c, :, u * BLK:(u + 1) * BLK] = jnp.where(
                    real, jnp.where(sel, 0.0, NEG_BIG), jnp.where(kt >= thr_eff, 0.0, NEG_BIG))
                run = run + jnp.sum(eqf, axis=1, keepdims=True)
            return run

        lax.fori_loop(0, n_chunks, tie_chunk, jnp.zeros((BLK, 1), F32))

    scale = HEAD_DIM ** -0.5
    for h in range(DSA_HEADS):
        qh = q_ref[:, h * HEAD_DIM:(h + 1) * HEAD_DIM]

        def attn_chunk(c, carry, h=h, qh=qh):
            m, l, acc = carry
            off = pl.multiple_of(c * KEY_CHUNK, KEY_CHUNK)
            kh = kv_ref[pl.ds(off, KEY_CHUNK), (2 * h) * HEAD_DIM:(2 * h + 1) * HEAD_DIM]
            vh = kv_ref[pl.ds(off, KEY_CHUNK), (2 * h + 1) * HEAD_DIM:(2 * h + 2) * HEAD_DIM]
            logits = _dot_nt(qh, kh) * scale + bias_ref[c]
            m_new = jnp.maximum(m, jnp.max(logits, axis=1, keepdims=True))
            alpha = jnp.exp(m - m_new)
            pr = jnp.exp(logits - m_new)
            l = alpha * l + jnp.sum(pr, axis=1, keepdims=True)
            acc = alpha * acc + _dot(pr.astype(BF16), vh)
            return m_new, l, acc

        m0 = jnp.full((BLK, 1), NEG_BIG, F32)
        _, l, acc = lax.fori_loop(
            0, n_chunks, attn_chunk, (m0, jnp.zeros((BLK, 1), F32), jnp.zeros((BLK, HEAD_DIM), F32)))
        o_ref[:, h * HEAD_DIM:(h + 1) * HEAD_DIM] = (acc / l).astype(o_ref.dtype)


def _dsa(p, kv):
    s = p.shape[0]
    nb = s // BLK
    topk = min(TOPK_MAX, s // 4)
    n_kc = s // KEY_CHUNK
    return pl.pallas_call(
        functools.partial(_dsa_kernel, topk=topk),
        grid=(nb,),
        in_specs=[pl.BlockSpec((BLK, D_DSA), lambda i: (i, OFF_D_Q // D_DSA)),
                  pl.BlockSpec((BLK, IDX_HEADS * IDX_DIM), lambda i: (i, OFF_D_QI // (IDX_HEADS * IDX_DIM))),
                  pl.BlockSpec((BLK, BLK), lambda i: (i, OFF_TAIL // BLK)),
                  _resident((s, BLK), lambda i: (0, OFF_TAIL // BLK)),
                  _resident((s, 2 * D_DSA), lambda i: (0, 0))],
        out_specs=pl.BlockSpec((BLK, D_DSA), lambda i: (i, 0)),
        out_shape=jax.ShapeDtypeStruct((s, D_DSA), BF16),
        scratch_shapes=[pltpu.VMEM((n_kc, BLK, KEY_CHUNK), I32),
                        pltpu.VMEM((n_kc, BLK, KEY_CHUNK), F32)],
        compiler_params=_params("arbitrary"),
        name="dsa",
    )(p, p, p, p, kv)


def _layer_norm(r, g, b):
    mu = jnp.mean(r, axis=1, keepdims=True)
    var = jnp.mean(jnp.square(r - mu), axis=1, keepdims=True)
    return (r - mu) * lax.rsqrt(var + LN_EPS) * g + b


def _first_max_of4(vals):
    a, b, c, d = vals
    m = jnp.maximum(jnp.maximum(a, b), jnp.maximum(c, d))
    idx = jnp.where(a == m, 0, jnp.where(b == m, 1, jnp.where(c == m, 2, 3)))
    return m, idx


def _router_gates(logits_t):
    mx = jnp.max(logits_t, axis=0, keepdims=True)
    e = jnp.exp(logits_t - mx)
    probs = e / jnp.sum(e, axis=0, keepdims=True)
    rows = [probs[j:j + 1, :] for j in range(N_EXPERTS)]
    m1s, m2s, i1s, i2s, scores = [], [], [], [], []
    for g in range(N_GROUPS):
        vals = rows[g * EXPERTS_PER_GROUP:(g + 1) * EXPERTS_PER_GROUP]
        m1, i1 = _first_max_of4(vals)
        rest = [jnp.where(i1 == j, -1.0, vals[j]) for j in range(EXPERTS_PER_GROUP)]
        m2, i2 = _first_max_of4(rest)
        m1s.append(m1); m2s.append(m2); i1s.append(i1); i2s.append(i2); scores.append(m1 + m2)
    best, g_sel = _first_max_of4(scores)
    pick = lambda xs: jnp.where(g_sel == 0, xs[0], jnp.where(g_sel == 1, xs[1], jnp.where(g_sel == 2, xs[2], xs[3])))
    m1, m2, i1, i2 = pick(m1s), pick(m2s), pick(i1s), pick(i2s)
    den = m1 + m2
    w1, w2 = m1 / den, m2 / den
    e1 = g_sel * EXPERTS_PER_GROUP + i1
    e2 = g_sel * EXPERTS_PER_GROUP + i2
    gates = [jnp.where(e1 == j, w1, 0.0) + jnp.where(e2 == j, w2, 0.0) for j in range(N_EXPERTS)]
    return jnp.concatenate(gates, axis=0)


def _out_kernel(ysb_ref, yret_ref, ydsa_ref, wsb_ref, wret_ref, wdsa_ref, x_ref, g_ref, b_ref, wr_ref,
                x1_ref, gates_ref):
    mix = (_dot(ysb_ref[...], wsb_ref[...]) + _dot(yret_ref[...], wret_ref[...])
           + _dot(ydsa_ref[...], wdsa_ref[...]))
    x1 = _layer_norm(DEEPNORM_ALPHA * x_ref[...] + mix, g_ref[...], b_ref[...])
    x1_ref[...] = x1
    logits_t = lax.dot_general(wr_ref[...], x1, (((1,), (1,)), ((), ())),
                               preferred_element_type=F32, precision=lax.Precision.HIGHEST)
    gates_t = _router_gates(logits_t)
    tm = x1.shape[0]
    padded = jnp.concatenate([gates_t, jnp.zeros((BLK - N_EXPERTS, tm), F32)], axis=0)
    gates_ref[...] = padded.T


def _out_proj(ysb, yret, ydsa, wsb, wret, wdsa, x, g, b, wr_t):
    s = x.shape[0]
    tm = min(512, s)
    row = lambda n: pl.BlockSpec((tm, n), lambda i: (i, 0))
    whole = lambda a: _resident(a.shape, lambda i: (0, 0))
    return pl.pallas_call(
        _out_kernel,
        grid=(s // tm,),
        in_specs=[row(D_SB), row(D_RET), row(D_DSA), whole(wsb), whole(wret), whole(wdsa),
                  row(D_MODEL), whole(g), whole(b), whole(wr_t)],
        out_specs=[row(D_MODEL), row(BLK)],
        out_shape=[jax.ShapeDtypeStruct((s, D_MODEL), F32), jax.ShapeDtypeStruct((s, BLK), F32)],
        compiler_params=_params("parallel"),
        name="out_proj",
    )(ysb, yret, ydsa, wsb, wret, wdsa, x, g, b, wr_t)


FF_SPLIT = 2
FF_STEP = D_FF_EXPERT // FF_SPLIT


def _moe_kernel(x_ref, gates_ref, wg_ref, wu_ref, wd_ref, g_ref, b_ref, o_ref, xb_ref):
    step = pl.program_id(1)
    expert = step // FF_SPLIT

    @pl.when(step == 0)
    def _():
        xb_ref[...] = x_ref[...].astype(BF16)
        o_ref[...] = jnp.zeros_like(o_ref)

    xb = xb_ref[...]
    hg = _dot(xb, wg_ref[0, 0].astype(BF16))
    hu = _dot(xb, wu_ref[0, 0].astype(BF16))
    lane = lax.broadcasted_iota(I32, gates_ref.shape, 1)
    gate = jnp.sum(jnp.where(lane == expert, gates_ref[...], 0.0), axis=1, keepdims=True)
    act = hg * jax.nn.sigmoid(hg) * hu * gate
    o_ref[...] += _dot(act.astype(BF16), wd_ref[0, 0].astype(BF16))

    @pl.when(step == pl.num_programs(1) - 1)
    def _():
        o_ref[...] = _layer_norm(DEEPNORM_ALPHA * x_ref[...] + o_ref[...], g_ref[...], b_ref[...])


def _moe(x1, gates, w_gate, w_up, w_down, g, b, layer):
    s = x1.shape[0]
    tm = min(1024, s)
    row = lambda n: pl.BlockSpec((tm, n), lambda t, e: (t, 0))
    vec = pl.BlockSpec((1, D_MODEL), lambda t, e: (0, 0))
    return pl.pallas_call(
        _moe_kernel,
        grid=(s // tm, N_EXPERTS * FF_SPLIT),
        in_specs=[_resident((tm, D_MODEL), lambda t, e: (t, 0)), row(BLK),
                  pl.BlockSpec((1, 1, D_MODEL, FF_STEP), lambda t, e: (layer, e // FF_SPLIT, 0, e % FF_SPLIT)),
                  pl.BlockSpec((1, 1, D_MODEL, FF_STEP), lambda t, e: (layer, e // FF_SPLIT, 0, e % FF_SPLIT)),
                  pl.BlockSpec((1, 1, FF_STEP, D_MODEL), lambda t, e: (layer, e // FF_SPLIT, e % FF_SPLIT, 0)),
                  vec, vec],
        out_specs=row(D_MODEL),
        out_shape=jax.ShapeDtypeStruct((s, D_MODEL), F32),
        scratch_shapes=[pltpu.VMEM((tm, D_MODEL), BF16)],
        compiler_params=_params("parallel", "arbitrary"),
        name="moe",
    )(x1, gates, w_gate, w_up, w_down, g, b)


def _reorder_w_in(w):
    sb = w[:, 0:3 * D_SB]
    o = 3 * D_SB
    ret = w[:, o:o + 4 * D_RET]
    o += 4 * D_RET
    d_q = w[:, o:o + D_DSA]
    o += D_DSA
    d_ckv = w[:, o:o + KV_RANK]
    o += KV_RANK
    d_qi = w[:, o:o + IDX_HEADS * IDX_DIM]
    o += IDX_HEADS * IDX_DIM
    tail = w[:, o:]
    parts = [sb, d_q, ret, d_qi, d_ckv, tail]
    width = sum(a.shape[1] for a in parts)
    parts.append(jnp.zeros((w.shape[0], D_PROJ - width), w.dtype))
    return jnp.concatenate(parts, axis=1).astype(BF16)


def kernel(x, w_in, w_kv_up, kv_norm_g, ret_gn_g, w_o, ln1_g, ln1_b, w_router, w_gate, w_up, w_down,
           ln2_g, ln2_b):
    b, s, _ = x.shape
    assert b == 1 and s % KEY_CHUNK == 0
    h = x[0]
    tables = _retention_tables(s)
    wr_t = w_router.T
    for l in range(DEPTH):
        p = _proj(h, _reorder_w_in(w_in[l]))
        y_sb = _stick_breaking(p)
        y_ret = _retention(p, ret_gn_g[l][None, :], tables)
        kv = _kv_up(p, kv_norm_g[l][None, :], w_kv_up[l].astype(BF16))
        y_dsa = _dsa(p, kv)
        wo = w_o[l].astype(BF16)
        x1, gates = _out_proj(y_sb, y_ret, y_dsa, wo[0:D_SB], wo[D_SB:D_SB + D_RET], wo[D_SB + D_RET:],
                              h, ln1_g[l][None, :], ln1_b[l][None, :], wr_t)
        h = _moe(x1, gates, w_gate, w_up, w_down, ln2_g[l][None, :], ln2_b[l][None, :], l)
    return h[None]
```

```python
import functools

import numpy as np
import jax
import jax.numpy as jnp
from jax import lax
from jax.experimental import pallas as pl
from jax.experimental.pallas import tpu as pltpu

F32 = jnp.float32
BF16 = jnp.bfloat16
I32 = jnp.int32

D_MODEL = 2048
HEAD_DIM = 128
SB_HEADS = 6
RET_HEADS = 4
DSA_HEADS = 6
D_SB = SB_HEADS * HEAD_DIM
D_RET = RET_HEADS * HEAD_DIM
D_DSA = DSA_HEADS * HEAD_DIM
KV_RANK = 256
IDX_HEADS = 8
IDX_DIM = 64
IDX_SCALE = IDX_DIM ** -0.5 * IDX_HEADS ** -0.5
TOPK_MAX = 256
BLK = 128
N_EXPERTS = 16
N_GROUPS = 4
EXPERTS_PER_GROUP = N_EXPERTS // N_GROUPS
D_FF_EXPERT = 512
LN_EPS = 1e-5
RMS_EPS = 1e-6
GN_EPS = 1e-6
DEPTH = 2
DEEPNORM_ALPHA = (2 * DEPTH) ** 0.25

OFF_SB_Q = 0
OFF_SB_K = OFF_SB_Q + D_SB
OFF_SB_V = OFF_SB_K + D_SB
OFF_D_Q = OFF_SB_V + D_SB
OFF_R_Q = OFF_D_Q + D_DSA
OFF_R_K = OFF_R_Q + D_RET
OFF_R_V = OFF_R_K + D_RET
OFF_R_G = OFF_R_V + D_RET
OFF_D_QI = OFF_R_G + D_RET
OFF_D_CKV = OFF_D_QI + IDX_HEADS * IDX_DIM
OFF_TAIL = OFF_D_CKV + KV_RANK
D_PROJ = 6144

KEY_CHUNK = 512
VMEM_LIMIT = 56 * 1024 * 1024
NEG_BIG = -1e30
EXP_UNDERFLOW = -104.0
KEY_OF_NEG_INF = -2139095041
INT_MIN = -2147483648


def _dot(a, b):
    return jnp.dot(a, b, preferred_element_type=F32)


def _dot_nt(a, b):
    return lax.dot_general(a, b, (((1,), (1,)), ((), ())), preferred_element_type=F32)


def _dot_tn(a, b):
    return lax.dot_general(a, b, (((0,), (0,)), ((), ())), preferred_element_type=F32)


def _params(*sem):
    return pltpu.CompilerParams(dimension_semantics=sem, vmem_limit_bytes=VMEM_LIMIT)


def _resident(shape, index_map):
    return pl.BlockSpec(shape, index_map, pipeline_mode=pl.Buffered(1))


def _proj_kernel(x_ref, w_ref, o_ref):
    o_ref[...] = _dot(x_ref[...].astype(BF16), w_ref[...]).astype(o_ref.dtype)


def _proj(x, w):
    s, d = x.shape
    n = w.shape[1]
    tm = min(1024, s)
    tn = 1536
    return pl.pallas_call(
        _proj_kernel,
        grid=(s // tm, n // tn),
        in_specs=[pl.BlockSpec((tm, d), lambda i, j: (i, 0)),
                  pl.BlockSpec((d, tn), lambda i, j: (0, j))],
        out_specs=pl.BlockSpec((tm, tn), lambda i, j: (i, j)),
        out_shape=jax.ShapeDtypeStruct((s, n), BF16),
        compiler_params=_params("parallel", "arbitrary"),
        name="proj",
    )(x, w)


def _sb_kernel(q_ref, k_ref, v_ref, o_ref, acc_ref):
    i = pl.program_id(0)
    q = q_ref[...]
    scale = HEAD_DIM ** -0.5
    key_pos = lax.broadcasted_iota(I32, (BLK, BLK), 0)
    qry_pos = lax.broadcasted_iota(I32, (BLK, BLK), 1)
    later = (qry_pos > key_pos).astype(BF16)
    acc_ref[...] = jnp.zeros_like(acc_ref)

    def cond(carry):
        j, cs = carry
        c_max = functools.reduce(jnp.maximum, cs)
        return jnp.logical_and(j >= 0, jnp.max(c_max) > EXP_UNDERFLOW)

    def body(carry):
        j, cs = carry
        off = pl.multiple_of(j * BLK, BLK)
        strict = (off + key_pos) < (i * BLK + qry_pos)
        new_cs = []
        for h in range(SB_HEADS):
            hs = slice(h * HEAD_DIM, (h + 1) * HEAD_DIM)
            z = _dot_nt(k_ref[pl.ds(off, BLK), hs], q[:, hs]) * scale
            sp = jnp.maximum(z, 0.0) + jnp.log1p(jnp.exp(-jnp.abs(z)))
            log_rem = jnp.where(strict, -sp, 0.0)
            hi = log_rem.astype(BF16)
            lo = (log_rem - hi.astype(F32)).astype(BF16)
            after = _dot(later, hi) + _dot(later, lo)
            a = jnp.where(strict, jnp.exp(z - sp + after + cs[h]), 0.0)
            acc_ref[h] += _dot_tn(a.astype(BF16), v_ref[pl.ds(off, BLK), hs])
            new_cs.append(cs[h] + jnp.sum(log_rem, axis=0, keepdims=True))
        return j - 1, tuple(new_cs)

    lax.while_loop(cond, body, (i, tuple(jnp.zeros((1, BLK), F32) for _ in range(SB_HEADS))))
    for h in range(SB_HEADS):
        o_ref[:, h * HEAD_DIM:(h + 1) * HEAD_DIM] = acc_ref[h].astype(o_ref.dtype)


def _stick_breaking(p):
    s = p.shape[0]
    nb = s // BLK
    return pl.pallas_call(
        _sb_kernel,
        grid=(nb,),
        in_specs=[pl.BlockSpec((BLK, D_SB), lambda i: (i, OFF_SB_Q // D_SB)),
                  _resident((s, D_SB), lambda i: (0, OFF_SB_K // D_SB)),
                  _resident((s, D_SB), lambda i: (0, OFF_SB_V // D_SB))],
        out_specs=pl.BlockSpec((BLK, D_SB), lambda i: (i, 0)),
        out_shape=jax.ShapeDtypeStruct((s, D_SB), BF16),
        scratch_shapes=[pltpu.VMEM((SB_HEADS, BLK, HEAD_DIM), F32)],
        compiler_params=_params("arbitrary"),
        name="stick_breaking",
    )(p, p, p)


def _ret_kernel(q_ref, k_ref, v_ref, g_ref, cos_ref, sin_ref, intra_ref, qd_ref, kd_ref, cd_ref,
                gn_ref, o_ref, state_ref):
    n = pl.program_id(0)

    @pl.when(n == 0)
    def _():
        state_ref[...] = jnp.zeros_like(state_ref)

    cos = cos_ref[...]
    sin = sin_ref[...]
    for h in range(RET_HEADS):
        hs = slice(h * HEAD_DIM, (h + 1) * HEAD_DIM)
        q = q_ref[:, hs].astype(F32)
        k = k_ref[:, hs].astype(F32)
        v = v_ref[:, hs]
        qr = q * cos + pltpu.roll(q, HEAD_DIM // 2, 1) * sin
        kr = (k * cos + pltpu.roll(k, HEAD_DIM // 2, 1) * sin) * (HEAD_DIM ** -0.5)
        scores = _dot_nt(qr.astype(BF16), kr.astype(BF16)) * intra_ref[h]
        state = state_ref[h]
        o = (_dot(scores.astype(BF16), v)
             + _dot((qr * qd_ref[h]).astype(BF16), state.astype(BF16)))
        state_ref[h] = cd_ref[h] * state + _dot_tn((kr * kd_ref[h]).astype(BF16), v)
        mu = jnp.mean(o, axis=1, keepdims=True)
        var = jnp.mean(jnp.square(o - mu), axis=1, keepdims=True)
        on = (o - mu) * lax.rsqrt(var + GN_EPS) * gn_ref[:, hs]
        g = g_ref[:, hs].astype(F32)
        o_ref[:, hs] = (g * jax.nn.sigmoid(g) * on).astype(o_ref.dtype)


def _retention_tables(s):
    half = HEAD_DIM // 2
    pos = jnp.arange(s, dtype=F32)
    theta = 10000.0 ** (-jnp.linspace(0.0, 1.0, half, dtype=F32))
    ang = pos[:, None] * theta[None, :]
    cos, sin = jnp.cos(ang), jnp.sin(ang)
    cos2 = jnp.concatenate([cos, cos], axis=1)
    sin2 = jnp.concatenate([-sin, sin], axis=1)
    log_gamma = jnp.log1p(-(2.0 ** (-5.0 - jnp.arange(RET_HEADS, dtype=F32))))
    idx = jnp.arange(BLK, dtype=F32)
    diff = idx[:, None] - idx[None, :]
    intra = jnp.where(diff >= 0, jnp.exp(jnp.maximum(diff, 0.0)[None] * log_gamma[:, None, None]), 0.0)
    q_decay = jnp.exp((idx[None, :] + 1.0) * log_gamma[:, None])
    k_decay = jnp.exp((BLK - 1.0 - idx[None, :]) * log_gamma[:, None])
    chunk_decay = jnp.exp(BLK * log_gamma)
    full = (RET_HEADS, BLK, HEAD_DIM)
    return (cos2, sin2, intra,
            jnp.broadcast_to(q_decay[:, :, None], full),
            jnp.broadcast_to(k_decay[:, :, None], full),
            jnp.broadcast_to(chunk_decay[:, None, None], full))


def _retention(p, gn_g, tables):
    s = p.shape[0]
    nc = s // BLK
    cos2, sin2, intra, qd, kd, cd = tables
    col = lambda off: pl.BlockSpec((BLK, D_RET), lambda n: (n, off // D_RET))
    per_head = pl.BlockSpec((RET_HEADS, BLK, HEAD_DIM), lambda n: (0, 0, 0))
    pos_spec = pl.BlockSpec((BLK, HEAD_DIM), lambda n: (n, 0))
    return pl.pallas_call(
        _ret_kernel,
        grid=(nc,),
        in_specs=[col(OFF_R_Q), col(OFF_R_K), col(OFF_R_V), col(OFF_R_G),
                  pos_spec, pos_spec, per_head, per_head, per_head, per_head,
                  pl.BlockSpec((1, D_RET), lambda n: (0, 0))],
        out_specs=pl.BlockSpec((BLK, D_RET), lambda n: (n, 0)),
        out_shape=jax.ShapeDtypeStruct((s, D_RET), BF16),
        scratch_shapes=[pltpu.VMEM((RET_HEADS, HEAD_DIM, HEAD_DIM), F32)],
        compiler_params=_params("arbitrary"),
        name="retention",
    )(p, p, p, p, cos2, sin2, intra, qd, kd, cd, gn_g)


def _kv_up_kernel(c_ref, g_ref, w_ref, k_ref, vt_ref):
    c = c_ref[...].astype(F32)
    y = c * lax.rsqrt(jnp.mean(jnp.square(c), axis=1, keepdims=True) + RMS_EPS) * g_ref[...]
    kv = _dot(y.astype(BF16), w_ref[...])
    k_ref[...] = (kv[:, :D_DSA] * (HEAD_DIM ** -0.5)).astype(k_ref.dtype)
    vt_ref[0] = kv[:, D_DSA:].T.astype(vt_ref.dtype)


def _kv_up(p, g, w):
    s = p.shape[0]
    n = w.shape[1]
    return pl.pallas_call(
        _kv_up_kernel,
        grid=(s // KEY_CHUNK,),
        in_specs=[pl.BlockSpec((KEY_CHUNK, KV_RANK), lambda i: (i, OFF_D_CKV // KV_RANK)),
                  pl.BlockSpec((1, KV_RANK), lambda i: (0, 0)),
                  pl.BlockSpec((KV_RANK, n), lambda i: (0, 0))],
        out_specs=[pl.BlockSpec((KEY_CHUNK, D_DSA), lambda i: (i, 0)),
                   pl.BlockSpec((1, D_DSA, KEY_CHUNK), lambda i: (i, 0, 0))],
        out_shape=[jax.ShapeDtypeStruct((s, D_DSA), BF16),
                   jax.ShapeDtypeStruct((s // KEY_CHUNK, D_DSA, KEY_CHUNK), BF16)],
        compiler_params=_params("parallel"),
        name="kv_up",
    )(p, g, w)


def _dsa_kernel(q_ref, qi_ref, tq_ref, tail_ref, k_ref, vt_ref, o_ref, key_ref, bias_ref, acc_ref, *, topk):
    i = pl.program_id(0)
    n_chunks = ((i + 1) * BLK + KEY_CHUNK - 1) // KEY_CHUNK
    key_pos = lax.broadcasted_iota(I32, (KEY_CHUNK, BLK), 0)
    t_col = i * BLK + lax.broadcasted_iota(I32, (KEY_CHUNK, BLK), 1)

    w_t = tq_ref[...].astype(F32).T[IDX_DIM:IDX_DIM + IDX_HEADS, :] * IDX_SCALE
    qi = qi_ref[...]

    def score_chunk(c, _):
        off = pl.multiple_of(c * KEY_CHUNK, KEY_CHUNK)
        k_idx = tail_ref[pl.ds(off, KEY_CHUNK), 0:IDX_DIM]
        score = jnp.zeros((KEY_CHUNK, BLK), F32)
        for h in range(IDX_HEADS):
            rel = jnp.maximum(_dot_nt(k_idx, qi[:, h * IDX_DIM:(h + 1) * IDX_DIM]), 0.0)
            score = score + rel * w_t[h:h + 1, :]
        score = jnp.where(score == 0.0, 0.0, score)
        score = jnp.where(off + key_pos <= t_col, score, -jnp.inf)
        bits = pltpu.bitcast(score, I32)
        key_ref[c] = bits ^ ((bits >> 31) & 0x7FFFFFFF)
        return 0

    lax.fori_loop(0, n_chunks, score_chunk, 0)

    def count_ge(cand):
        def chunk(c, acc):
            hit = jnp.where(key_ref[c] >= cand, 1, 0)
            return acc + jnp.sum(hit.reshape(KEY_CHUNK // 8, 8, BLK), axis=0)

        acc = lax.fori_loop(0, n_chunks, chunk, jnp.zeros((8, BLK), I32))
        return jnp.sum(acc, axis=0, keepdims=True)

    zero = jnp.zeros((1, BLK), I32)
    thr = jnp.where(count_ge(zero) >= topk, zero, jnp.full((1, BLK), INT_MIN, I32))

    def bit_step(b, thr):
        cand = thr | (jnp.int32(1) << (30 - b))
        return jnp.where(count_ge(cand) >= topk, cand, thr)

    thr = lax.fori_loop(0, 31, bit_step, thr)
    n_ge = count_ge(thr)
    real = thr > KEY_OF_NEG_INF
    thr_eff = jnp.maximum(thr, KEY_OF_NEG_INF + 1)

    def bias_chunk(c, _):
        bias_ref[c] = jnp.where(key_ref[c] >= thr_eff, 0.0, NEG_BIG)
        return 0

    lax.fori_loop(0, n_chunks, bias_chunk, 0)

    has_tie = jnp.max(jnp.where(jnp.logical_and(real, n_ge > topk), 1, 0)) > 0

    @pl.when(has_tie)
    def _():
        need = (topk - count_ge(thr + 1)).astype(F32)
        r = lax.broadcasted_iota(I32, (BLK, BLK), 0)
        cc = lax.broadcasted_iota(I32, (BLK, BLK), 1)
        upto = (cc <= r).astype(BF16)

        def tie_chunk(c, run):
            keys = key_ref[c]
            for u in range(KEY_CHUNK // BLK):
                kt = keys[u * BLK:(u + 1) * BLK, :]
                eq = kt == thr
                eqf = jnp.where(eq, 1.0, 0.0)
                rank = _dot(upto, eqf.astype(BF16)) + run
                sel = jnp.logical_or(kt > thr, jnp.logical_and(eq, rank <= need))
                bias_ref[c, u * BLK:(u + 1) * BLK, :] = jnp.where(
                    real, jnp.where(sel, 0.0, NEG_BIG), jnp.where(kt >= thr_eff, 0.0, NEG_BIG))
                run = run + jnp.sum(eqf, axis=0, keepdims=True)
            return run

        lax.fori_loop(0, n_chunks, tie_chunk, jnp.zeros((1, BLK), F32))

    acc_ref[...] = jnp.zeros_like(acc_ref)
    q = q_ref[...]

    def attn_chunk(c, carry):
        ms, ls = carry
        off = pl.multiple_of(c * KEY_CHUNK, KEY_CHUNK)
        bias = bias_ref[c]
        new_m, new_l = [], []
        for h in range(DSA_HEADS):
            hs = slice(h * HEAD_DIM, (h + 1) * HEAD_DIM)
            logits = _dot_nt(k_ref[pl.ds(off, KEY_CHUNK), hs], q[:, hs]) + bias
            m_new = jnp.maximum(ms[h], jnp.max(logits, axis=0, keepdims=True))
            alpha = jnp.exp(ms[h] - m_new)
            pr = jnp.exp(logits - m_new)
            new_l.append(alpha * ls[h] + jnp.sum(pr, axis=0, keepdims=True))
            acc_ref[h] = alpha * acc_ref[h] + _dot(vt_ref[c, hs, :], pr.astype(BF16))
            new_m.append(m_new)
        return tuple(new_m), tuple(new_l)

    m0 = tuple(jnp.full((1, BLK), NEG_BIG, F32) for _ in range(DSA_HEADS))
    l0 = tuple(jnp.zeros((1, BLK), F32) for _ in range(DSA_HEADS))
    _, ls = lax.fori_loop(0, n_chunks, attn_chunk, (m0, l0))
    for h in range(DSA_HEADS):
        o_ref[:, h * HEAD_DIM:(h + 1) * HEAD_DIM] = (acc_ref[h] / ls[h]).T.astype(o_ref.dtype)


def _dsa(p, k, vt):
    s = p.shape[0]
    nb = s // BLK
    topk = min(TOPK_MAX, s // 4)
    n_kc = s // KEY_CHUNK
    return pl.pallas_call(
        functools.partial(_dsa_kernel, topk=topk),
        grid=(nb,),
        in_specs=[pl.BlockSpec((BLK, D_DSA), lambda i: (i, OFF_D_Q // D_DSA)),
                  pl.BlockSpec((BLK, IDX_HEADS * IDX_DIM), lambda i: (i, OFF_D_QI // (IDX_HEADS * IDX_DIM))),
                  pl.BlockSpec((BLK, BLK), lambda i: (i, OFF_TAIL // BLK)),
                  _resident((s, BLK), lambda i: (0, OFF_TAIL // BLK)),
                  _resident((s, D_DSA), lambda i: (0, 0)),
                  _resident((n_kc, D_DSA, KEY_CHUNK), lambda i: (0, 0, 0))],
        out_specs=pl.BlockSpec((BLK, D_DSA), lambda i: (i, 0)),
        out_shape=jax.ShapeDtypeStruct((s, D_DSA), BF16),
        scratch_shapes=[pltpu.VMEM((n_kc, KEY_CHUNK, BLK), I32),
                        pltpu.VMEM((n_kc, KEY_CHUNK, BLK), F32),
                        pltpu.VMEM((DSA_HEADS, HEAD_DIM, BLK), F32)],
        compiler_params=_params("arbitrary"),
        name="dsa",
    )(p, p, p, p, k, vt)


def _layer_norm(r, g, b):
    mu = jnp.mean(r, axis=1, keepdims=True)
    var = jnp.mean(jnp.square(r - mu), axis=1, keepdims=True)
    return (r - mu) * lax.rsqrt(var + LN_EPS) * g + b


def _first_max_of4(vals):
    a, b, c, d = vals
    m = jnp.maximum(jnp.maximum(a, b), jnp.maximum(c, d))
    idx = jnp.where(a == m, 0, jnp.where(b == m, 1, jnp.where(c == m, 2, 3)))
    return m, idx


def _router_gates(logits_t):
    mx = jnp.max(logits_t, axis=0, keepdims=True)
    e = jnp.exp(logits_t - mx)
    probs = e / jnp.sum(e, axis=0, keepdims=True)
    rows = [probs[j:j + 1, :] for j in range(N_EXPERTS)]
    m1s, m2s, i1s, i2s, scores = [], [], [], [], []
    for g in range(N_GROUPS):
        vals = rows[g * EXPERTS_PER_GROUP:(g + 1) * EXPERTS_PER_GROUP]
        m1, i1 = _first_max_of4(vals)
        rest = [jnp.where(i1 == j, -1.0, vals[j]) for j in range(EXPERTS_PER_GROUP)]
        m2, i2 = _first_max_of4(rest)
        m1s.append(m1); m2s.append(m2); i1s.append(i1); i2s.append(i2); scores.append(m1 + m2)
    best, g_sel = _first_max_of4(scores)
    pick = lambda xs: jnp.where(g_sel == 0, xs[0], jnp.where(g_sel == 1, xs[1], jnp.where(g_sel == 2, xs[2], xs[3])))
    m1, m2, i1, i2 = pick(m1s), pick(m2s), pick(i1s), pick(i2s)
    den = m1 + m2
    w1, w2 = m1 / den, m2 / den
    e1 = g_sel * EXPERTS_PER_GROUP + i1
    e2 = g_sel * EXPERTS_PER_GROUP + i2
    gates = [jnp.where(e1 == j, w1, 0.0) + jnp.where(e2 == j, w2, 0.0) for j in range(N_EXPERTS)]
    return jnp.concatenate(gates, axis=0)


def _out_kernel(ysb_ref, yret_ref, ydsa_ref, wsb_ref, wret_ref, wdsa_ref, x_ref, g_ref, b_ref, wr_ref,
                x1_ref, gates_ref):
    mix = (_dot(ysb_ref[...], wsb_ref[...]) + _dot(yret_ref[...], wret_ref[...])
           + _dot(ydsa_ref[...], wdsa_ref[...]))
    x1 = _layer_norm(DEEPNORM_ALPHA * x_ref[...] + mix, g_ref[...], b_ref[...])
    x1_ref[...] = x1
    logits_t = lax.dot_general(wr_ref[...], x1, (((1,), (1,)), ((), ())),
                               preferred_element_type=F32, precision=lax.Precision.HIGHEST)
    gates_t = _router_gates(logits_t)
    tm = x1.shape[0]
    padded = jnp.concatenate([gates_t, jnp.zeros((BLK - N_EXPERTS, tm), F32)], axis=0)
    gates_ref[...] = padded.T


def _out_proj(ysb, yret, ydsa, wsb, wret, wdsa, x, g, b, wr_t):
    s = x.shape[0]
    tm = min(512, s)
    row = lambda n: pl.BlockSpec((tm, n), lambda i: (i, 0))
    whole = lambda a: _resident(a.shape, lambda i: (0, 0))
    return pl.pallas_call(
        _out_kernel,
        grid=(s // tm,),
        in_specs=[row(D_SB), row(D_RET), row(D_DSA), whole(wsb), whole(wret), whole(wdsa),
                  row(D_MODEL), whole(g), whole(b), whole(wr_t)],
        out_specs=[row(D_MODEL), row(BLK)],
        out_shape=[jax.ShapeDtypeStruct((s, D_MODEL), F32), jax.ShapeDtypeStruct((s, BLK), F32)],
        compiler_params=_params("parallel"),
        name="out_proj",
    )(ysb, yret, ydsa, wsb, wret, wdsa, x, g, b, wr_t)


FF_SPLIT = 2
FF_STEP = D_FF_EXPERT // FF_SPLIT


def _moe_kernel(x_ref, gates_ref, wg_ref, wu_ref, wd_ref, g_ref, b_ref, o_ref, xb_ref):
    step = pl.program_id(1)
    expert = step // FF_SPLIT

    @pl.when(step == 0)
    def _():
        xb_ref[...] = x_ref[...].astype(BF16)
        o_ref[...] = jnp.zeros_like(o_ref)

    xb = xb_ref[...]
    hg = _dot(xb, wg_ref[0, 0].astype(BF16))
    hu = _dot(xb, wu_ref[0, 0].astype(BF16))
    lane = lax.broadcasted_iota(I32, gates_ref.shape, 1)
    gate = jnp.sum(jnp.where(lane == expert, gates_ref[...], 0.0), axis=1, keepdims=True)
    act = hg * jax.nn.sigmoid(hg) * hu * gate
    o_ref[...] += _dot(act.astype(BF16), wd_ref[0, 0].astype(BF16))

    @pl.when(step == pl.num_programs(1) - 1)
    def _():
        o_ref[...] = _layer_norm(DEEPNORM_ALPHA * x_ref[...] + o_ref[...], g_ref[...], b_ref[...])


def _moe(x1, gates, w_gate, w_up, w_down, g, b, layer):
    s = x1.shape[0]
    tm = min(1024, s)
    row = lambda n: pl.BlockSpec((tm, n), lambda t, e: (t, 0))
    vec = pl.BlockSpec((1, D_MODEL), lambda t, e: (0, 0))
    return pl.pallas_call(
        _moe_kernel,
        grid=(s // tm, N_EXPERTS * FF_SPLIT),
        in_specs=[_resident((tm, D_MODEL), lambda t, e: (t, 0)), row(BLK),
                  pl.BlockSpec((1, 1, D_MODEL, FF_STEP), lambda t, e: (layer, e // FF_SPLIT, 0, e % FF_SPLIT)),
                  pl.BlockSpec((1, 1, D_MODEL, FF_STEP), lambda t, e: (layer, e // FF_SPLIT, 0, e % FF_SPLIT)),
                  pl.BlockSpec((1, 1, FF_STEP, D_MODEL), lambda t, e: (layer, e // FF_SPLIT, e % FF_SPLIT, 0)),
                  vec, vec],
        out_specs=row(D_MODEL),
        out_shape=jax.ShapeDtypeStruct((s, D_MODEL), F32),
        scratch_shapes=[pltpu.VMEM((tm, D_MODEL), BF16)],
        compiler_params=_params("parallel", "arbitrary"),
        name="moe",
    )(x1, gates, w_gate, w_up, w_down, g, b)


def _reorder_w_in(w):
    sb = w[:, 0:3 * D_SB]
    o = 3 * D_SB
    ret = w[:, o:o + 4 * D_RET]
    o += 4 * D_RET
    d_q = w[:, o:o + D_DSA]
    o += D_DSA
    d_ckv = w[:, o:o + KV_RANK]
    o += KV_RANK
    d_qi = w[:, o:o + IDX_HEADS * IDX_DIM]
    o += IDX_HEADS * IDX_DIM
    tail = w[:, o:]
    parts = [sb, d_q, ret, d_qi, d_ckv, tail]
    width = sum(a.shape[1] for a in parts)
    parts.append(jnp.zeros((w.shape[0], D_PROJ - width), w.dtype))
    return jnp.concatenate(parts, axis=1).astype(BF16)


def kernel(x, w_in, w_kv_up, kv_norm_g, ret_gn_g, w_o, ln1_g, ln1_b, w_router, w_gate, w_up, w_down,
           ln2_g, ln2_b):
    b, s, _ = x.shape
    assert b == 1 and s % KEY_CHUNK == 0
    h = x[0]
    tables = _retention_tables(s)
    wr_t = w_router.T
    for l in range(DEPTH):
        p = _proj(h, _reorder_w_in(w_in[l]))
        y_sb = _stick_breaking(p)
        y_ret = _retention(p, ret_gn_g[l][None, :], tables)
        w_kv = w_kv_up[l].reshape(KV_RANK, DSA_HEADS, 2, HEAD_DIM)
        w_kv = jnp.concatenate([w_kv[:, :, 0, :].reshape(KV_RANK, D_DSA),
                                w_kv[:, :, 1, :].reshape(KV_RANK, D_DSA)], axis=1).astype(BF16)
        k_dsa, vt_dsa = _kv_up(p, kv_norm_g[l][None, :], w_kv)
        y_dsa = _dsa(p, k_dsa, vt_dsa)
        wo = w_o[l].astype(BF16)
        x1, gates = _out_proj(y_sb, y_ret, y_dsa, wo[0:D_SB], wo[D_SB:D_SB + D_RET], wo[D_SB + D_RET:],
                              h, ln1_g[l][None, :], ln1_b[l][None, :], wr_t)
        h = _moe(x1, gates, w_gate, w_up, w_down, ln2_g[l][None, :], ln2_b[l][None, :], l)
    return h[None]
```

```python
import functools

import numpy as np
import jax
import jax.numpy as jnp
from jax import lax
from jax.experimental import pallas as pl
from jax.experimental.pallas import tpu as pltpu

F32 = jnp.float32
BF16 = jnp.bfloat16
I32 = jnp.int32

D_MODEL = 2048
HEAD_DIM = 128
SB_HEADS = 6
RET_HEADS = 4
DSA_HEADS = 6
D_SB = SB_HEADS * HEAD_DIM
D_RET = RET_HEADS * HEAD_DIM
D_DSA = DSA_HEADS * HEAD_DIM
KV_RANK = 256
IDX_HEADS = 8
IDX_DIM = 64
IDX_SCALE = IDX_DIM ** -0.5 * IDX_HEADS ** -0.5
TOPK_MAX = 256
BLK = 128
N_EXPERTS = 16
N_GROUPS = 4
EXPERTS_PER_GROUP = N_EXPERTS // N_GROUPS
D_FF_EXPERT = 512
LN_EPS = 1e-5
RMS_EPS = 1e-6
GN_EPS = 1e-6
DEPTH = 2
DEEPNORM_ALPHA = (2 * DEPTH) ** 0.25

OFF_SB_Q = 0
OFF_SB_K = OFF_SB_Q + D_SB
OFF_SB_V = OFF_SB_K + D_SB
OFF_D_Q = OFF_SB_V + D_SB
OFF_R_Q = OFF_D_Q + D_DSA
OFF_R_K = OFF_R_Q + D_RET
OFF_R_V = OFF_R_K + D_RET
OFF_R_G = OFF_R_V + D_RET
OFF_D_QI = OFF_R_G + D_RET
OFF_D_CKV = OFF_D_QI + IDX_HEADS * IDX_DIM
OFF_TAIL = OFF_D_CKV + KV_RANK
D_PROJ = 6144

KEY_CHUNK = 512
VMEM_LIMIT = 56 * 1024 * 1024
NEG_BIG = -1e30
EXP_UNDERFLOW = -104.0
KEY_OF_NEG_INF = -2139095041
INT_MIN = -2147483648


def _dot(a, b):
    return jnp.dot(a, b, preferred_element_type=F32)


def _dot_nt(a, b):
    return lax.dot_general(a, b, (((1,), (1,)), ((), ())), preferred_element_type=F32)


def _dot_tn(a, b):
    return lax.dot_general(a, b, (((0,), (0,)), ((), ())), preferred_element_type=F32)


def _params(*sem):
    return pltpu.CompilerParams(dimension_semantics=sem, vmem_limit_bytes=VMEM_LIMIT)


def _resident(shape, index_map):
    return pl.BlockSpec(shape, index_map, pipeline_mode=pl.Buffered(1))


def _proj_kernel(x_ref, w_ref, o_ref):
    o_ref[...] = _dot(x_ref[...].astype(BF16), w_ref[...]).astype(o_ref.dtype)


def _proj(x, w):
    s, d = x.shape
    n = w.shape[1]
    tm = min(1024, s)
    tn = 1536
    return pl.pallas_call(
        _proj_kernel,
        grid=(s // tm, n // tn),
        in_specs=[pl.BlockSpec((tm, d), lambda i, j: (i, 0)),
                  pl.BlockSpec((d, tn), lambda i, j: (0, j))],
        out_specs=pl.BlockSpec((tm, tn), lambda i, j: (i, j)),
        out_shape=jax.ShapeDtypeStruct((s, n), BF16),
        compiler_params=_params("parallel", "arbitrary"),
        name="proj",
    )(x, w)


def _sb_kernel(q_ref, k_ref, v_ref, o_ref, acc_ref):
    i = pl.program_id(0)
    q = q_ref[...]
    scale = HEAD_DIM ** -0.5
    key_pos = lax.broadcasted_iota(I32, (BLK, BLK), 0)
    qry_pos = lax.broadcasted_iota(I32, (BLK, BLK), 1)
    later = (qry_pos > key_pos).astype(BF16)
    acc_ref[...] = jnp.zeros_like(acc_ref)

    def cond(carry):
        j, cs = carry
        c_max = functools.reduce(jnp.maximum, cs)
        return jnp.logical_and(j >= 0, jnp.max(c_max) > EXP_UNDERFLOW)

    def body(carry):
        j, cs = carry
        off = pl.multiple_of(j * BLK, BLK)
        strict = (off + key_pos) < (i * BLK + qry_pos)
        new_cs = []
        for h in range(SB_HEADS):
            hs = slice(h * HEAD_DIM, (h + 1) * HEAD_DIM)
            z = _dot_nt(k_ref[pl.ds(off, BLK), hs], q[:, hs]) * scale
            sp = jnp.maximum(z, 0.0) + jnp.log1p(jnp.exp(-jnp.abs(z)))
            log_rem = jnp.where(strict, -sp, 0.0)
            hi = log_rem.astype(BF16)
            lo = (log_rem - hi.astype(F32)).astype(BF16)
            after = _dot(later, hi) + _dot(later, lo)
            a = jnp.where(strict, jnp.exp(z - sp + after + cs[h]), 0.0)
            acc_ref[h] += _dot_tn(a.astype(BF16), v_ref[pl.ds(off, BLK), hs])
            new_cs.append(cs[h] + jnp.sum(log_rem, axis=0, keepdims=True))
        return j - 1, tuple(new_cs)

    lax.while_loop(cond, body, (i, tuple(jnp.zeros((1, BLK), F32) for _ in range(SB_HEADS))))
    for h in range(SB_HEADS):
        o_ref[:, h * HEAD_DIM:(h + 1) * HEAD_DIM] = acc_ref[h].astype(o_ref.dtype)


def _stick_breaking(p):
    s = p.shape[0]
    nb = s // BLK
    return pl.pallas_call(
        _sb_kernel,
        grid=(nb,),
        in_specs=[pl.BlockSpec((BLK, D_SB), lambda i: (i, OFF_SB_Q // D_SB)),
                  _resident((s, D_SB), lambda i: (0, OFF_SB_K // D_SB)),
                  _resident((s, D_SB), lambda i: (0, OFF_SB_V // D_SB))],
        out_specs=pl.BlockSpec((BLK, D_SB), lambda i: (i, 0)),
        out_shape=jax.ShapeDtypeStruct((s, D_SB), BF16),
        scratch_shapes=[pltpu.VMEM((SB_HEADS, BLK, HEAD_DIM), F32)],
        compiler_params=_params("arbitrary"),
        name="stick_breaking",
    )(p, p, p)


def _ret_kernel(q_ref, k_ref, v_ref, g_ref, cos_ref, sin_ref, intra_ref, qd_ref, kd_ref, cd_ref,
                gn_ref, o_ref, state_ref):
    n = pl.program_id(0)

    @pl.when(n == 0)
    def _():
        state_ref[...] = jnp.zeros_like(state_ref)

    cos = cos_ref[...]
    sin = sin_ref[...]
    for h in range(RET_HEADS):
        hs = slice(h * HEAD_DIM, (h + 1) * HEAD_DIM)
        q = q_ref[:, hs].astype(F32)
        k = k_ref[:, hs].astype(F32)
        v = v_ref[:, hs]
        qr = q * cos + pltpu.roll(q, HEAD_DIM // 2, 1) * sin
        kr = (k * cos + pltpu.roll(k, HEAD_DIM // 2, 1) * sin) * (HEAD_DIM ** -0.5)
        scores = _dot_nt(qr.astype(BF16), kr.astype(BF16)) * intra_ref[h]
        state = state_ref[h]
        o = (_dot(scores.astype(BF16), v)
             + _dot((qr * qd_ref[h]).astype(BF16), state.astype(BF16)))
        state_ref[h] = cd_ref[h] * state + _dot_tn((kr * kd_ref[h]).astype(BF16), v)
        mu = jnp.mean(o, axis=1, keepdims=True)
        var = jnp.mean(jnp.square(o - mu), axis=1, keepdims=True)
        on = (o - mu) * lax.rsqrt(var + GN_EPS) * gn_ref[:, hs]
        g = g_ref[:, hs].astype(F32)
        o_ref[:, hs] = (g * jax.nn.sigmoid(g) * on).astype(o_ref.dtype)


def _retention_tables(s):
    half = HEAD_DIM // 2
    pos = jnp.arange(s, dtype=F32)
    theta = 10000.0 ** (-jnp.linspace(0.0, 1.0, half, dtype=F32))
    ang = pos[:, None] * theta[None, :]
    cos, sin = jnp.cos(ang), jnp.sin(ang)
    cos2 = jnp.concatenate([cos, cos], axis=1)
    sin2 = jnp.concatenate([-sin, sin], axis=1)
    log_gamma = jnp.log1p(-(2.0 ** (-5.0 - jnp.arange(RET_HEADS, dtype=F32))))
    idx = jnp.arange(BLK, dtype=F32)
    diff = idx[:, None] - idx[None, :]
    intra = jnp.where(diff >= 0, jnp.exp(jnp.maximum(diff, 0.0)[None] * log_gamma[:, None, None]), 0.0)
    q_decay = jnp.exp((idx[None, :] + 1.0) * log_gamma[:, None])
    k_decay = jnp.exp((BLK - 1.0 - idx[None, :]) * log_gamma[:, None])
    chunk_decay = jnp.exp(BLK * log_gamma)
    full = (RET_HEADS, BLK, HEAD_DIM)
    return (cos2, sin2, intra,
            jnp.broadcast_to(q_decay[:, :, None], full),
            jnp.broadcast_to(k_decay[:, :, None], full),
            jnp.broadcast_to(chunk_decay[:, None, None], full))


def _retention(p, gn_g, tables):
    s = p.shape[0]
    nc = s // BLK
    cos2, sin2, intra, qd, kd, cd = tables
    col = lambda off: pl.BlockSpec((BLK, D_RET), lambda n: (n, off // D_RET))
    per_head = pl.BlockSpec((RET_HEADS, BLK, HEAD_DIM), lambda n: (0, 0, 0))
    pos_spec = pl.BlockSpec((BLK, HEAD_DIM), lambda n: (n, 0))
    return pl.pallas_call(
        _ret_kernel,
        grid=(nc,),
        in_specs=[col(OFF_R_Q), col(OFF_R_K), col(OFF_R_V), col(OFF_R_G),
                  pos_spec, pos_spec, per_head, per_head, per_head, per_head,
                  pl.BlockSpec((1, D_RET), lambda n: (0, 0))],
        out_specs=pl.BlockSpec((BLK, D_RET), lambda n: (n, 0)),
        out_shape=jax.ShapeDtypeStruct((s, D_RET), BF16),
        scratch_shapes=[pltpu.VMEM((RET_HEADS, HEAD_DIM, HEAD_DIM), F32)],
        compiler_params=_params("arbitrary"),
        name="retention",
    )(p, p, p, p, cos2, sin2, intra, qd, kd, cd, gn_g)


def _kv_up_kernel(c_ref, g_ref, w_ref, k_ref, vt_ref):
    c = c_ref[...].astype(F32)
    y = c * lax.rsqrt(jnp.mean(jnp.square(c), axis=1, keepdims=True) + RMS_EPS) * g_ref[...]
    kv = _dot(y.astype(BF16), w_ref[...])
    k_ref[...] = (kv[:, :D_DSA] * (HEAD_DIM ** -0.5)).astype(k_ref.dtype)
    vt_ref[0] = kv[:, D_DSA:].T.astype(vt_ref.dtype)


def _kv_up(p, g, w):
    s = p.shape[0]
    n = w.shape[1]
    return pl.pallas_call(
        _kv_up_kernel,
        grid=(s // KEY_CHUNK,),
        in_specs=[pl.BlockSpec((KEY_CHUNK, KV_RANK), lambda i: (i, OFF_D_CKV // KV_RANK)),
                  pl.BlockSpec((1, KV_RANK), lambda i: (0, 0)),
                  pl.BlockSpec((KV_RANK, n), lambda i: (0, 0))],
        out_specs=[pl.BlockSpec((KEY_CHUNK, D_DSA), lambda i: (i, 0)),
                   pl.BlockSpec((1, D_DSA, KEY_CHUNK), lambda i: (i, 0, 0))],
        out_shape=[jax.ShapeDtypeStruct((s, D_DSA), BF16),
                   jax.ShapeDtypeStruct((s // KEY_CHUNK, D_DSA, KEY_CHUNK), BF16)],
        compiler_params=_params("parallel"),
        name="kv_up",
    )(p, g, w)


def _ordered_bits_to_float(u):
    return pltpu.bitcast(u ^ ((u >> 31) & 0x7FFFFFFF), F32)


GROUPS_PER_CHUNK = KEY_CHUNK // (32 * 8)


def _bit_transpose32(words):
    a = list(words)
    j, mask = 16, 0x0000FFFF
    while j:
        k = 0
        while k < 32:
            t = (a[k] ^ (a[k + j] >> j)) & mask
            a[k] = a[k] ^ t
            a[k + j] = a[k + j] ^ (t << j)
            k = (k + j + 1) & ~j
        j >>= 1
        mask = (mask ^ (mask << j)) & 0xFFFFFFFF
    return a


def _dsa_kernel(q_ref, qi_ref, tq_ref, tail_ref, k_ref, vt_ref, o_ref,
                score_ref, planes_ref, alive_ref, bias_ref, acc_ref, lg_a, lg_b, *, topk):
    i = pl.program_id(0)
    n_chunks = ((i + 1) * BLK + KEY_CHUNK - 1) // KEY_CHUNK
    n_kc = score_ref.shape[0]

    w_t = tq_ref[...].astype(F32).T[IDX_DIM:IDX_DIM + IDX_HEADS, :] * IDX_SCALE
    qi = qi_ref[...]
    qi_rows = jnp.concatenate([qi[:, h * IDX_DIM:(h + 1) * IDX_DIM] for h in range(IDX_HEADS)], axis=0)

    def chunk_scores(c):
        off = pl.multiple_of(c * KEY_CHUNK, KEY_CHUNK)
        rel = jnp.maximum(_dot_nt(tail_ref[pl.ds(off, KEY_CHUNK), 0:IDX_DIM], qi_rows), 0.0)
        score = rel[:, 0:BLK] * w_t[0:1, :]
        for h in range(1, IDX_HEADS):
            score = score + rel[:, h * BLK:(h + 1) * BLK] * w_t[h:h + 1, :]
        return score

    def store_chunk(c, score):
        score = jnp.where(score == 0.0, 0.0, score)
        score_ref[c] = score
        bits = pltpu.bitcast(score, I32)
        u = bits ^ ((bits >> 31) | INT_MIN)
        for g in range(GROUPS_PER_CHUNK):
            words = [u[(g * 32 + j) * 8:(g * 32 + j + 1) * 8, :] for j in range(32)]
            planes = _bit_transpose32(words)
            for b in range(32):
                planes_ref[c * GROUPS_PER_CHUNK + g, b] = planes[b]

    def score_chunk(c, _):
        store_chunk(c, chunk_scores(c))
        return 0

    lax.fori_loop(0, n_chunks - 1, score_chunk, 0)
    last = n_chunks - 1
    key_pos = last * KEY_CHUNK + lax.broadcasted_iota(I32, (KEY_CHUNK, BLK), 0)
    t_col = i * BLK + lax.broadcasted_iota(I32, (KEY_CHUNK, BLK), 1)
    store_chunk(last, jnp.where(key_pos <= t_col, chunk_scores(last), -jnp.inf))

    def init_alive(c, _):
        for g in range(GROUPS_PER_CHUNK):
            alive_ref[c * GROUPS_PER_CHUNK + g] = jnp.full((8, BLK), -1, I32)
        return 0

    lax.fori_loop(0, n_chunks, init_alive, 0)

    def decide(above, t_bits, ones, bit):
        take = (above + ones) >= topk
        return (jnp.where(take, above, above + ones), jnp.where(take, t_bits | bit, t_bits),
                jnp.where(take, 0, -1))

    def sweep(plane_prev, plane, drop_prev):
        def chunk(c, acc):
            for g in range(GROUPS_PER_CHUNK):
                gi = c * GROUPS_PER_CHUNK + g
                alive = alive_ref[gi]
                if plane_prev is not None:
                    alive = alive & (planes_ref[gi, plane_prev] ^ drop_prev)
                    alive_ref[gi] = alive
                if plane is not None:
                    alive = alive & planes_ref[gi, plane]
                acc = acc + lax.population_count(alive)
            return acc

        acc = lax.fori_loop(0, n_chunks, chunk, jnp.zeros((8, BLK), I32))
        return jnp.sum(acc, axis=0, keepdims=True)

    zero = jnp.zeros((1, BLK), I32)
    state = decide(zero, zero, sweep(None, 0, None), INT_MIN)

    def bit_step(b, state):
        above, t_bits, drop = state
        return decide(above, t_bits, sweep(b - 1, b, drop), jnp.int32(1) << (31 - b))

    above, t_bits, drop = lax.fori_loop(1, 32, bit_step, state)
    n_eq = sweep(31, None, drop)
    need = topk - above
    thr = t_bits ^ INT_MIN
    real = thr > KEY_OF_NEG_INF
    thr_f = jnp.where(real, _ordered_bits_to_float(thr), jnp.finfo(F32).min)

    def bias_chunk(c, _):
        bias_ref[c] = jnp.where(score_ref[c] >= thr_f, 0.0, NEG_BIG)
        return 0

    lax.fori_loop(0, n_chunks, bias_chunk, 0)

    has_tie = jnp.max(jnp.where(jnp.logical_and(real, n_eq > need), 1, 0)) > 0

    @pl.when(has_tie)
    def _():
        need_f = need.astype(F32)
        r = lax.broadcasted_iota(I32, (BLK, BLK), 0)
        cc = lax.broadcasted_iota(I32, (BLK, BLK), 1)
        upto = (cc <= r).astype(BF16)

        def tie_chunk(c, run):
            sc = score_ref[c]
            for u in range(KEY_CHUNK // BLK):
                st = sc[u * BLK:(u + 1) * BLK, :]
                eq = jnp.logical_and(st == thr_f, real)
                eqf = jnp.where(eq, 1.0, 0.0)
                rank = _dot(upto, eqf.astype(BF16)) + run
                sel = jnp.logical_or(st > thr_f, jnp.logical_and(eq, rank <= need_f))
                bias_ref[c, u * BLK:(u + 1) * BLK, :] = jnp.where(
                    real, jnp.where(sel, 0.0, NEG_BIG), jnp.where(st >= thr_f, 0.0, NEG_BIG))
                run = run + jnp.sum(eqf, axis=0, keepdims=True)
            return run

        lax.fori_loop(0, n_chunks, tie_chunk, jnp.zeros((1, BLK), F32))

    acc_ref[...] = jnp.zeros_like(acc_ref)
    bias_ref[n_kc] = jnp.full((KEY_CHUNK, BLK), NEG_BIG, F32)
    q = q_ref[...]

    def logits_into(lg, c):
        kc = jnp.minimum(c, n_chunks - 1)
        off = pl.multiple_of(kc * KEY_CHUNK, KEY_CHUNK)
        bias = bias_ref[jnp.where(c < n_chunks, c, n_kc)]
        for h in range(DSA_HEADS):
            hs = slice(h * HEAD_DIM, (h + 1) * HEAD_DIM)
            lg[h] = _dot_nt(k_ref[pl.ds(off, KEY_CHUNK), hs], q[:, hs]) + bias

    def reduce_from(lg, c, ms, ls):
        vc = jnp.minimum(c, n_chunks - 1)
        new_m, new_l = [], []
        for h in range(DSA_HEADS):
            hs = slice(h * HEAD_DIM, (h + 1) * HEAD_DIM)
            logits = lg[h]
            m_new = jnp.maximum(ms[h], jnp.max(logits, axis=0, keepdims=True))
            alpha = jnp.exp(ms[h] - m_new)
            pr = jnp.exp(logits - m_new)
            new_l.append(alpha * ls[h] + jnp.sum(pr, axis=0, keepdims=True))
            acc_ref[h] = alpha * acc_ref[h] + _dot(vt_ref[vc, hs, :], pr.astype(BF16))
            new_m.append(m_new)
        return tuple(new_m), tuple(new_l)

    def attn_pair(pair, carry):
        ms, ls = carry
        c = 2 * pair
        logits_into(lg_b, c + 1)
        ms, ls = reduce_from(lg_a, c, ms, ls)
        logits_into(lg_a, c + 2)
        return reduce_from(lg_b, c + 1, ms, ls)

    logits_into(lg_a, 0)
    m0 = tuple(jnp.full((1, BLK), NEG_BIG, F32) for _ in range(DSA_HEADS))
    l0 = tuple(jnp.zeros((1, BLK), F32) for _ in range(DSA_HEADS))
    _, ls = lax.fori_loop(0, (n_chunks + 1) // 2, attn_pair, (m0, l0))
    for h in range(DSA_HEADS):
        o_ref[:, h * HEAD_DIM:(h + 1) * HEAD_DIM] = (acc_ref[h] / ls[h]).T.astype(o_ref.dtype)


def _dsa(p, k, vt):
    s = p.shape[0]
    nb = s // BLK
    topk = min(TOPK_MAX, s // 4)
    n_kc = s // KEY_CHUNK
    return pl.pallas_call(
        functools.partial(_dsa_kernel, topk=topk),
        grid=(nb,),
        in_specs=[pl.BlockSpec((BLK, D_DSA), lambda i: (i, OFF_D_Q // D_DSA)),
                  pl.BlockSpec((BLK, IDX_HEADS * IDX_DIM), lambda i: (i, OFF_D_QI // (IDX_HEADS * IDX_DIM))),
                  pl.BlockSpec((BLK, BLK), lambda i: (i, OFF_TAIL // BLK)),
                  _resident((s, BLK), lambda i: (0, OFF_TAIL // BLK)),
                  _resident((s, D_DSA), lambda i: (0, 0)),
                  _resident((n_kc, D_DSA, KEY_CHUNK), lambda i: (0, 0, 0))],
        out_specs=pl.BlockSpec((BLK, D_DSA), lambda i: (i, 0)),
        out_shape=jax.ShapeDtypeStruct((s, D_DSA), BF16),
        scratch_shapes=[pltpu.VMEM((n_kc, KEY_CHUNK, BLK), F32),
                        pltpu.VMEM((n_kc * GROUPS_PER_CHUNK, 32, 8, BLK), I32),
                        pltpu.VMEM((n_kc * GROUPS_PER_CHUNK, 8, BLK), I32),
                        pltpu.VMEM((n_kc + 1, KEY_CHUNK, BLK), F32),
                        pltpu.VMEM((DSA_HEADS, HEAD_DIM, BLK), F32),
                        pltpu.VMEM((DSA_HEADS, KEY_CHUNK, BLK), F32),
                        pltpu.VMEM((DSA_HEADS, KEY_CHUNK, BLK), F32)],
        compiler_params=_params("arbitrary"),
        name="dsa",
    )(p, p, p, p, k, vt)


def _layer_norm(r, g, b):
    mu = jnp.mean(r, axis=1, keepdims=True)
    var = jnp.mean(jnp.square(r - mu), axis=1, keepdims=True)
    return (r - mu) * lax.rsqrt(var + LN_EPS) * g + b


def _first_max_of4(vals):
    a, b, c, d = vals
    m = jnp.maximum(jnp.maximum(a, b), jnp.maximum(c, d))
    idx = jnp.where(a == m, 0, jnp.where(b == m, 1, jnp.where(c == m, 2, 3)))
    return m, idx


def _router_gates(logits_t):
    mx = jnp.max(logits_t, axis=0, keepdims=True)
    e = jnp.exp(logits_t - mx)
    probs = e / jnp.sum(e, axis=0, keepdims=True)
    rows = [probs[j:j + 1, :] for j in range(N_EXPERTS)]
    m1s, m2s, i1s, i2s, scores = [], [], [], [], []
    for g in range(N_GROUPS):
        vals = rows[g * EXPERTS_PER_GROUP:(g + 1) * EXPERTS_PER_GROUP]
        m1, i1 = _first_max_of4(vals)
        rest = [jnp.where(i1 == j, -1.0, vals[j]) for j in range(EXPERTS_PER_GROUP)]
        m2, i2 = _first_max_of4(rest)
        m1s.append(m1); m2s.append(m2); i1s.append(i1); i2s.append(i2); scores.append(m1 + m2)
    best, g_sel = _first_max_of4(scores)
    pick = lambda xs: jnp.where(g_sel == 0, xs[0], jnp.where(g_sel == 1, xs[1], jnp.where(g_sel == 2, xs[2], xs[3])))
    m1, m2, i1, i2 = pick(m1s), pick(m2s), pick(i1s), pick(i2s)
    den = m1 + m2
    w1, w2 = m1 / den, m2 / den
    e1 = g_sel * EXPERTS_PER_GROUP + i1
    e2 = g_sel * EXPERTS_PER_GROUP + i2
    gates = [jnp.where(e1 == j, w1, 0.0) + jnp.where(e2 == j, w2, 0.0) for j in range(N_EXPERTS)]
    return jnp.concatenate(gates, axis=0)


def _out_kernel(ysb_ref, yret_ref, ydsa_ref, wsb_ref, wret_ref, wdsa_ref, x_ref, g_ref, b_ref, wr_ref,
                x1_ref, gates_ref):
    mix = (_dot(ysb_ref[...], wsb_ref[...]) + _dot(yret_ref[...], wret_ref[...])
           + _dot(ydsa_ref[...], wdsa_ref[...]))
    x1 = _layer_norm(DEEPNORM_ALPHA * x_ref[...] + mix, g_ref[...], b_ref[...])
    x1_ref[...] = x1
    logits_t = lax.dot_general(wr_ref[...], x1, (((1,), (1,)), ((), ())),
                               preferred_element_type=F32, precision=lax.Precision.HIGHEST)
    gates_t = _router_gates(logits_t)
    tm = x1.shape[0]
    padded = jnp.concatenate([gates_t, jnp.zeros((BLK - N_EXPERTS, tm), F32)], axis=0)
    gates_ref[...] = padded.T


def _out_proj(ysb, yret, ydsa, wsb, wret, wdsa, x, g, b, wr_t):
    s = x.shape[0]
    tm = min(512, s)
    row = lambda n: pl.BlockSpec((tm, n), lambda i: (i, 0))
    whole = lambda a: _resident(a.shape, lambda i: (0, 0))
    return pl.pallas_call(
        _out_kernel,
        grid=(s // tm,),
        in_specs=[row(D_SB), row(D_RET), row(D_DSA), whole(wsb), whole(wret), whole(wdsa),
                  row(D_MODEL), whole(g), whole(b), whole(wr_t)],
        out_specs=[row(D_MODEL), row(BLK)],
        out_shape=[jax.ShapeDtypeStruct((s, D_MODEL), F32), jax.ShapeDtypeStruct((s, BLK), F32)],
        compiler_params=_params("parallel"),
        name="out_proj",
    )(ysb, yret, ydsa, wsb, wret, wdsa, x, g, b, wr_t)


FF_SPLIT = 2
FF_STEP = D_FF_EXPERT // FF_SPLIT


def _moe_kernel(x_ref, gates_ref, wg_ref, wu_ref, wd_ref, g_ref, b_ref, o_ref, xb_ref):
    step = pl.program_id(1)
    expert = step // FF_SPLIT

    @pl.when(step == 0)
    def _():
        xb_ref[...] = x_ref[...].astype(BF16)
        o_ref[...] = jnp.zeros_like(o_ref)

    xb = xb_ref[...]
    hg = _dot(xb, wg_ref[0, 0].astype(BF16))
    hu = _dot(xb, wu_ref[0, 0].astype(BF16))
    lane = lax.broadcasted_iota(I32, gates_ref.shape, 1)
    gate = jnp.sum(jnp.where(lane == expert, gates_ref[...], 0.0), axis=1, keepdims=True)
    act = hg * jax.nn.sigmoid(hg) * hu * gate
    o_ref[...] += _dot(act.astype(BF16), wd_ref[0, 0].astype(BF16))

    @pl.when(step == pl.num_programs(1) - 1)
    def _():
        o_ref[...] = _layer_norm(DEEPNORM_ALPHA * x_ref[...] + o_ref[...], g_ref[...], b_ref[...])


def _moe(x1, gates, w_gate, w_up, w_down, g, b, layer):
    s = x1.shape[0]
    tm = min(1024, s)
    row = lambda n: pl.BlockSpec((tm, n), lambda t, e: (t, 0))
    vec = pl.BlockSpec((1, D_MODEL), lambda t, e: (0, 0))
    return pl.pallas_call(
        _moe_kernel,
        grid=(s // tm, N_EXPERTS * FF_SPLIT),
        in_specs=[_resident((tm, D_MODEL), lambda t, e: (t, 0)), row(BLK),
                  pl.BlockSpec((1, 1, D_MODEL, FF_STEP), lambda t, e: (layer, e // FF_SPLIT, 0, e % FF_SPLIT)),
                  pl.BlockSpec((1, 1, D_MODEL, FF_STEP), lambda t, e: (layer, e // FF_SPLIT, 0, e % FF_SPLIT)),
                  pl.BlockSpec((1, 1, FF_STEP, D_MODEL), lambda t, e: (layer, e // FF_SPLIT, e % FF_SPLIT, 0)),
                  vec, vec],
        out_specs=row(D_MODEL),
        out_shape=jax.ShapeDtypeStruct((s, D_MODEL), F32),
        scratch_shapes=[pltpu.VMEM((tm, D_MODEL), BF16)],
        compiler_params=_params("parallel", "arbitrary"),
        name="moe",
    )(x1, gates, w_gate, w_up, w_down, g, b)


def _reorder_w_in(w):
    sb = w[:, 0:3 * D_SB]
    o = 3 * D_SB
    ret = w[:, o:o + 4 * D_RET]
    o += 4 * D_RET
    d_q = w[:, o:o + D_DSA]
    o += D_DSA
    d_ckv = w[:, o:o + KV_RANK]
    o += KV_RANK
    d_qi = w[:, o:o + IDX_HEADS * IDX_DIM]
    o += IDX_HEADS * IDX_DIM
    tail = w[:, o:]
    parts = [sb, d_q, ret, d_qi, d_ckv, tail]
    width = sum(a.shape[1] for a in parts)
    parts.append(jnp.zeros((w.shape[0], D_PROJ - width), w.dtype))
    return jnp.concatenate(parts, axis=1).astype(BF16)


def kernel(x, w_in, w_kv_up, kv_norm_g, ret_gn_g, w_o, ln1_g, ln1_b, w_router, w_gate, w_up, w_down,
           ln2_g, ln2_b):
    b, s, _ = x.shape
    assert b == 1 and s % KEY_CHUNK == 0
    h = x[0]
    tables = _retention_tables(s)
    wr_t = w_router.T
    for l in range(DEPTH):
        p = _proj(h, _reorder_w_in(w_in[l]))
        y_sb = _stick_breaking(p)
        y_ret = _retention(p, ret_gn_g[l][None, :], tables)
        w_kv = w_kv_up[l].reshape(KV_RANK, DSA_HEADS, 2, HEAD_DIM)
        w_kv = jnp.concatenate([w_kv[:, :, 0, :].reshape(KV_RANK, D_DSA),
                                w_kv[:, :, 1, :].reshape(KV_RANK, D_DSA)], axis=1).astype(BF16)
        k_dsa, vt_dsa = _kv_up(p, kv_norm_g[l][None, :], w_kv)
        y_dsa = _dsa(p, k_dsa, vt_dsa)
        wo = w_o[l].astype(BF16)
        x1, gates = _out_proj(y_sb, y_ret, y_dsa, wo[0:D_SB], wo[D_SB:D_SB + D_RET], wo[D_SB + D_RET:],
                              h, ln1_g[l][None, :], ln1_b[l][None, :], wr_t)
        h = _moe(x1, gates, w_gate, w_up, w_down, ln2_g[l][None, :], ln2_b[l][None, :], l)
    return h[None]
```

```python
import functools

import numpy as np
import jax
import jax.numpy as jnp
from jax import lax
from jax.experimental import pallas as pl
from jax.experimental.pallas import tpu as pltpu

F32 = jnp.float32
BF16 = jnp.bfloat16
I32 = jnp.int32

D_MODEL = 2048
HEAD_DIM = 128
SB_HEADS = 6
RET_HEADS = 4
DSA_HEADS = 6
D_SB = SB_HEADS * HEAD_DIM
D_RET = RET_HEADS * HEAD_DIM
D_DSA = DSA_HEADS * HEAD_DIM
KV_RANK = 256
IDX_HEADS = 8
IDX_DIM = 64
IDX_SCALE = IDX_DIM ** -0.5 * IDX_HEADS ** -0.5
TOPK_MAX = 256
BLK = 128
N_EXPERTS = 16
N_GROUPS = 4
EXPERTS_PER_GROUP = N_EXPERTS // N_GROUPS
D_FF_EXPERT = 512
LN_EPS = 1e-5
RMS_EPS = 1e-6
GN_EPS = 1e-6
DEPTH = 2
DEEPNORM_ALPHA = (2 * DEPTH) ** 0.25

OFF_SB_Q = 0
OFF_SB_K = OFF_SB_Q + D_SB
OFF_SB_V = OFF_SB_K + D_SB
OFF_D_Q = OFF_SB_V + D_SB
OFF_R_Q = OFF_D_Q + D_DSA
OFF_R_K = OFF_R_Q + D_RET
OFF_R_V = OFF_R_K + D_RET
OFF_R_G = OFF_R_V + D_RET
OFF_D_QI = OFF_R_G + D_RET
OFF_D_CKV = OFF_D_QI + IDX_HEADS * IDX_DIM
OFF_TAIL = OFF_D_CKV + KV_RANK
D_PROJ = 6144

KEY_CHUNK = 512
MOE_TILE = 512
WIN = 16
ROW_TILE = 256
VMEM_LIMIT = 56 * 1024 * 1024
NEG_BIG = -1e30
EXP_UNDERFLOW = -104.0
KEY_OF_NEG_INF = -2139095041
INT_MIN = -2147483648


def _dot(a, b):
    return jnp.dot(a, b, preferred_element_type=F32)


def _dot_nt(a, b):
    return lax.dot_general(a, b, (((1,), (1,)), ((), ())), preferred_element_type=F32)


def _dot_tn(a, b):
    return lax.dot_general(a, b, (((0,), (0,)), ((), ())), preferred_element_type=F32)


def _params(*sem):
    return pltpu.CompilerParams(dimension_semantics=sem, vmem_limit_bytes=VMEM_LIMIT)


def _resident(shape, index_map):
    return pl.BlockSpec(shape, index_map, pipeline_mode=pl.Buffered(1))


def _proj_kernel(x_ref, w_ref, o_ref):
    o_ref[...] = _dot(x_ref[...].astype(BF16), w_ref[...]).astype(o_ref.dtype)


def _proj(x, w):
    s, d = x.shape
    n = w.shape[1]
    tm = min(1024, s)
    tn = 1536
    return pl.pallas_call(
        _proj_kernel,
        grid=(s // tm, n // tn),
        in_specs=[pl.BlockSpec((tm, d), lambda i, j: (i, 0)),
                  pl.BlockSpec((d, tn), lambda i, j: (0, j))],
        out_specs=pl.BlockSpec((tm, tn), lambda i, j: (i, j)),
        out_shape=jax.ShapeDtypeStruct((s, n), BF16),
        compiler_params=_params("parallel", "arbitrary"),
        name="proj",
    )(x, w)


def _sb_kernel(q_ref, k_ref, v_ref, o_ref, acc_ref):
    i = pl.program_id(0)
    q = q_ref[...]
    scale = HEAD_DIM ** -0.5
    key_pos = lax.broadcasted_iota(I32, (BLK, BLK), 0)
    qry_pos = lax.broadcasted_iota(I32, (BLK, BLK), 1)
    later = (qry_pos > key_pos).astype(BF16)
    acc_ref[...] = jnp.zeros_like(acc_ref)

    def cond(carry):
        j, cs = carry
        c_max = functools.reduce(jnp.maximum, cs)
        return jnp.logical_and(j >= 0, jnp.max(c_max) > EXP_UNDERFLOW)

    def body(carry):
        j, cs = carry
        off = pl.multiple_of(j * BLK, BLK)
        strict = (off + key_pos) < (i * BLK + qry_pos)
        new_cs = []
        for h in range(SB_HEADS):
            hs = slice(h * HEAD_DIM, (h + 1) * HEAD_DIM)
            z = _dot_nt(k_ref[pl.ds(off, BLK), hs], q[:, hs]) * scale
            sp = jnp.maximum(z, 0.0) + jnp.log1p(jnp.exp(-jnp.abs(z)))
            log_rem = jnp.where(strict, -sp, 0.0)
            hi = log_rem.astype(BF16)
            lo = (log_rem - hi.astype(F32)).astype(BF16)
            after = _dot(later, hi) + _dot(later, lo)
            a = jnp.where(strict, jnp.exp(z - sp + after + cs[h]), 0.0)
            acc_ref[h] += _dot_tn(a.astype(BF16), v_ref[pl.ds(off, BLK), hs])
            new_cs.append(cs[h] + jnp.sum(log_rem, axis=0, keepdims=True))
        return j - 1, tuple(new_cs)

    lax.while_loop(cond, body, (i, tuple(jnp.zeros((1, BLK), F32) for _ in range(SB_HEADS))))
    for h in range(SB_HEADS):
        o_ref[:, h * HEAD_DIM:(h + 1) * HEAD_DIM] = acc_ref[h].astype(o_ref.dtype)


def _stick_breaking(p):
    s = p.shape[0]
    nb = s // BLK
    return pl.pallas_call(
        _sb_kernel,
        grid=(nb,),
        in_specs=[pl.BlockSpec((BLK, D_SB), lambda i: (i, OFF_SB_Q // D_SB)),
                  _resident((s, D_SB), lambda i: (0, OFF_SB_K // D_SB)),
                  _resident((s, D_SB), lambda i: (0, OFF_SB_V // D_SB))],
        out_specs=pl.BlockSpec((BLK, D_SB), lambda i: (i, 0)),
        out_shape=jax.ShapeDtypeStruct((s, D_SB), BF16),
        scratch_shapes=[pltpu.VMEM((SB_HEADS, BLK, HEAD_DIM), F32)],
        compiler_params=_params("arbitrary"),
        name="stick_breaking",
    )(p, p, p)


def _ret_kernel(q_ref, k_ref, v_ref, g_ref, cos_ref, sin_ref, intra_ref, qd_ref, kd_ref, cd_ref,
                gn_ref, o_ref, state_ref):
    n = pl.program_id(0)

    @pl.when(n == 0)
    def _():
        state_ref[...] = jnp.zeros_like(state_ref)

    cos = cos_ref[...]
    sin = sin_ref[...]
    for h in range(RET_HEADS):
        hs = slice(h * HEAD_DIM, (h + 1) * HEAD_DIM)
        q = q_ref[:, hs].astype(F32)
        k = k_ref[:, hs].astype(F32)
        v = v_ref[:, hs]
        qr = q * cos + pltpu.roll(q, HEAD_DIM // 2, 1) * sin
        kr = (k * cos + pltpu.roll(k, HEAD_DIM // 2, 1) * sin) * (HEAD_DIM ** -0.5)
        scores = _dot_nt(qr.astype(BF16), kr.astype(BF16)) * intra_ref[h]
        state = state_ref[h]
        o = (_dot(scores.astype(BF16), v)
             + _dot((qr * qd_ref[h]).astype(BF16), state.astype(BF16)))
        state_ref[h] = cd_ref[h] * state + _dot_tn((kr * kd_ref[h]).astype(BF16), v)
        mu = jnp.mean(o, axis=1, keepdims=True)
        var = jnp.mean(jnp.square(o - mu), axis=1, keepdims=True)
        on = (o - mu) * lax.rsqrt(var + GN_EPS) * gn_ref[:, hs]
        g = g_ref[:, hs].astype(F32)
        o_ref[:, hs] = (g * jax.nn.sigmoid(g) * on).astype(o_ref.dtype)


def _retention_tables(s):
    half = HEAD_DIM // 2
    pos = jnp.arange(s, dtype=F32)
    theta = 10000.0 ** (-jnp.linspace(0.0, 1.0, half, dtype=F32))
    ang = pos[:, None] * theta[None, :]
    cos, sin = jnp.cos(ang), jnp.sin(ang)
    cos2 = jnp.concatenate([cos, cos], axis=1)
    sin2 = jnp.concatenate([-sin, sin], axis=1)
    log_gamma = jnp.log1p(-(2.0 ** (-5.0 - jnp.arange(RET_HEADS, dtype=F32))))
    idx = jnp.arange(BLK, dtype=F32)
    diff = idx[:, None] - idx[None, :]
    intra = jnp.where(diff >= 0, jnp.exp(jnp.maximum(diff, 0.0)[None] * log_gamma[:, None, None]), 0.0)
    q_decay = jnp.exp((idx[None, :] + 1.0) * log_gamma[:, None])
    k_decay = jnp.exp((BLK - 1.0 - idx[None, :]) * log_gamma[:, None])
    chunk_decay = jnp.exp(BLK * log_gamma)
    full = (RET_HEADS, BLK, HEAD_DIM)
    return (cos2, sin2, intra,
            jnp.broadcast_to(q_decay[:, :, None], full),
            jnp.broadcast_to(k_decay[:, :, None], full),
            jnp.broadcast_to(chunk_decay[:, None, None], full))


def _retention(p, gn_g, tables):
    s = p.shape[0]
    nc = s // BLK
    cos2, sin2, intra, qd, kd, cd = tables
    col = lambda off: pl.BlockSpec((BLK, D_RET), lambda n: (n, off // D_RET))
    per_head = pl.BlockSpec((RET_HEADS, BLK, HEAD_DIM), lambda n: (0, 0, 0))
    pos_spec = pl.BlockSpec((BLK, HEAD_DIM), lambda n: (n, 0))
    return pl.pallas_call(
        _ret_kernel,
        grid=(nc,),
        in_specs=[col(OFF_R_Q), col(OFF_R_K), col(OFF_R_V), col(OFF_R_G),
                  pos_spec, pos_spec, per_head, per_head, per_head, per_head,
                  pl.BlockSpec((1, D_RET), lambda n: (0, 0))],
        out_specs=pl.BlockSpec((BLK, D_RET), lambda n: (n, 0)),
        out_shape=jax.ShapeDtypeStruct((s, D_RET), BF16),
        scratch_shapes=[pltpu.VMEM((RET_HEADS, HEAD_DIM, HEAD_DIM), F32)],
        compiler_params=_params("arbitrary"),
        name="retention",
    )(p, p, p, p, cos2, sin2, intra, qd, kd, cd, gn_g)


def _kv_up_kernel(c_ref, g_ref, w_ref, k_ref, vt_ref):
    c = c_ref[...].astype(F32)
    y = c * lax.rsqrt(jnp.mean(jnp.square(c), axis=1, keepdims=True) + RMS_EPS) * g_ref[...]
    kv = _dot(y.astype(BF16), w_ref[...])
    k_ref[...] = (kv[:, :D_DSA] * (HEAD_DIM ** -0.5)).astype(k_ref.dtype)
    vt_ref[0] = kv[:, D_DSA:].T.astype(vt_ref.dtype)


def _kv_up(p, g, w):
    s = p.shape[0]
    n = w.shape[1]
    return pl.pallas_call(
        _kv_up_kernel,
        grid=(s // KEY_CHUNK,),
        in_specs=[pl.BlockSpec((KEY_CHUNK, KV_RANK), lambda i: (i, OFF_D_CKV // KV_RANK)),
                  pl.BlockSpec((1, KV_RANK), lambda i: (0, 0)),
                  pl.BlockSpec((KV_RANK, n), lambda i: (0, 0))],
        out_specs=[pl.BlockSpec((KEY_CHUNK, D_DSA), lambda i: (i, 0)),
                   pl.BlockSpec((1, D_DSA, KEY_CHUNK), lambda i: (i, 0, 0))],
        out_shape=[jax.ShapeDtypeStruct((s, D_DSA), BF16),
                   jax.ShapeDtypeStruct((s // KEY_CHUNK, D_DSA, KEY_CHUNK), BF16)],
        compiler_params=_params("parallel"),
        name="kv_up",
    )(p, g, w)


def _ordered_bits_to_float(u):
    return pltpu.bitcast(u ^ ((u >> 31) & 0x7FFFFFFF), F32)


GROUPS_PER_CHUNK = KEY_CHUNK // (32 * 8)


def _bit_transpose32(words):
    a = list(words)
    j, mask = 16, 0x0000FFFF
    while j:
        k = 0
        while k < 32:
            t = (a[k] ^ (a[k + j] >> j)) & mask
            a[k] = a[k] ^ t
            a[k + j] = a[k + j] ^ (t << j)
            k = (k + j + 1) & ~j
        j >>= 1
        mask = (mask ^ (mask << j)) & 0xFFFFFFFF
    return a


def _dsa_kernel(q_ref, qi_ref, tq_ref, tail_ref, k_ref, vt_ref, o_ref,
                score_ref, planes_ref, alive_ref, bias_ref, acc_ref, lg_a, lg_b, *, topk):
    i = pl.program_id(0)
    n_chunks = ((i + 1) * BLK + KEY_CHUNK - 1) // KEY_CHUNK
    n_kc = score_ref.shape[0]

    w_t = tq_ref[...].astype(F32).T[IDX_DIM:IDX_DIM + IDX_HEADS, :] * IDX_SCALE
    qi = qi_ref[...]
    qi_rows = jnp.concatenate([qi[:, h * IDX_DIM:(h + 1) * IDX_DIM] for h in range(IDX_HEADS)], axis=0)

    def chunk_scores(c):
        off = pl.multiple_of(c * KEY_CHUNK, KEY_CHUNK)
        rel = jnp.maximum(_dot_nt(tail_ref[pl.ds(off, KEY_CHUNK), 0:IDX_DIM], qi_rows), 0.0)
        score = rel[:, 0:BLK] * w_t[0:1, :]
        for h in range(1, IDX_HEADS):
            score = score + rel[:, h * BLK:(h + 1) * BLK] * w_t[h:h + 1, :]
        return score

    def store_chunk(c, score):
        score = jnp.where(score == 0.0, 0.0, score)
        score_ref[c] = score
        bits = pltpu.bitcast(score, I32)
        u = bits ^ ((bits >> 31) | INT_MIN)
        for g in range(GROUPS_PER_CHUNK):
            words = [u[(g * 32 + j) * 8:(g * 32 + j + 1) * 8, :] for j in range(32)]
            planes = _bit_transpose32(words)
            for b in range(32):
                planes_ref[c * GROUPS_PER_CHUNK + g, b] = planes[b]

    def score_chunk(c, _):
        store_chunk(c, chunk_scores(c))
        return 0

    lax.fori_loop(0, n_chunks - 1, score_chunk, 0)
    last = n_chunks - 1
    key_pos = last * KEY_CHUNK + lax.broadcasted_iota(I32, (KEY_CHUNK, BLK), 0)
    t_col = i * BLK + lax.broadcasted_iota(I32, (KEY_CHUNK, BLK), 1)
    store_chunk(last, jnp.where(key_pos <= t_col, chunk_scores(last), -jnp.inf))

    def init_alive(c, _):
        for g in range(GROUPS_PER_CHUNK):
            alive_ref[c * GROUPS_PER_CHUNK + g] = jnp.full((8, BLK), -1, I32)
        return 0

    lax.fori_loop(0, n_chunks, init_alive, 0)

    def decide(above, t_bits, ones, bit):
        take = (above + ones) >= topk
        return (jnp.where(take, above, above + ones), jnp.where(take, t_bits | bit, t_bits),
                jnp.where(take, 0, -1))

    def sweep(plane_prev, plane, drop_prev):
        def chunk(c, acc):
            for g in range(GROUPS_PER_CHUNK):
                gi = c * GROUPS_PER_CHUNK + g
                alive = alive_ref[gi]
                if plane_prev is not None:
                    alive = alive & (planes_ref[gi, plane_prev] ^ drop_prev)
                    alive_ref[gi] = alive
                if plane is not None:
                    alive = alive & planes_ref[gi, plane]
                acc = acc + lax.population_count(alive)
            return acc

        acc = lax.fori_loop(0, n_chunks, chunk, jnp.zeros((8, BLK), I32))
        return jnp.sum(acc, axis=0, keepdims=True)

    zero = jnp.zeros((1, BLK), I32)
    state = decide(zero, zero, sweep(None, 0, None), INT_MIN)

    def bit_step(b, state):
        above, t_bits, drop = state
        return decide(above, t_bits, sweep(b - 1, b, drop), jnp.int32(1) << (31 - b))

    above, t_bits, drop = lax.fori_loop(1, 32, bit_step, state)
    n_eq = sweep(31, None, drop)
    need = topk - above
    thr = t_bits ^ INT_MIN
    real = thr > KEY_OF_NEG_INF
    thr_f = jnp.where(real, _ordered_bits_to_float(thr), jnp.finfo(F32).min)

    def bias_chunk(c, _):
        bias_ref[c] = jnp.where(score_ref[c] >= thr_f, 0.0, NEG_BIG)
        return 0

    lax.fori_loop(0, n_chunks, bias_chunk, 0)

    has_tie = jnp.max(jnp.where(jnp.logical_and(real, n_eq > need), 1, 0)) > 0

    @pl.when(has_tie)
    def _():
        need_f = need.astype(F32)
        r = lax.broadcasted_iota(I32, (BLK, BLK), 0)
        cc = lax.broadcasted_iota(I32, (BLK, BLK), 1)
        upto = (cc <= r).astype(BF16)

        def tie_chunk(c, run):
            sc = score_ref[c]
            for u in range(KEY_CHUNK // BLK):
                st = sc[u * BLK:(u + 1) * BLK, :]
                eq = jnp.logical_and(st == thr_f, real)
                eqf = jnp.where(eq, 1.0, 0.0)
                rank = _dot(upto, eqf.astype(BF16)) + run
                sel = jnp.logical_or(st > thr_f, jnp.logical_and(eq, rank <= need_f))
                bias_ref[c, u * BLK:(u + 1) * BLK, :] = jnp.where(
                    real, jnp.where(sel, 0.0, NEG_BIG), jnp.where(st >= thr_f, 0.0, NEG_BIG))
                run = run + jnp.sum(eqf, axis=0, keepdims=True)
            return run

        lax.fori_loop(0, n_chunks, tie_chunk, jnp.zeros((1, BLK), F32))

    acc_ref[...] = jnp.zeros_like(acc_ref)
    bias_ref[n_kc] = jnp.full((KEY_CHUNK, BLK), NEG_BIG, F32)
    q = q_ref[...]

    def logits_into(lg, c):
        kc = jnp.minimum(c, n_chunks - 1)
        off = pl.multiple_of(kc * KEY_CHUNK, KEY_CHUNK)
        bias = bias_ref[jnp.where(c < n_chunks, c, n_kc)]
        for h in range(DSA_HEADS):
            hs = slice(h * HEAD_DIM, (h + 1) * HEAD_DIM)
            lg[h] = _dot_nt(k_ref[pl.ds(off, KEY_CHUNK), hs], q[:, hs]) + bias

    def reduce_from(lg, c, ms, ls):
        vc = jnp.minimum(c, n_chunks - 1)
        new_m, new_l = [], []
        for h in range(DSA_HEADS):
            hs = slice(h * HEAD_DIM, (h + 1) * HEAD_DIM)
            logits = lg[h]
            m_new = jnp.maximum(ms[h], jnp.max(logits, axis=0, keepdims=True))
            alpha = jnp.exp(ms[h] - m_new)
            pr = jnp.exp(logits - m_new)
            new_l.append(alpha * ls[h] + jnp.sum(pr, axis=0, keepdims=True))
            acc_ref[h] = alpha * acc_ref[h] + _dot(vt_ref[vc, hs, :], pr.astype(BF16))
            new_m.append(m_new)
        return tuple(new_m), tuple(new_l)

    def attn_pair(pair, carry):
        ms, ls = carry
        c = 2 * pair
        logits_into(lg_b, c + 1)
        ms, ls = reduce_from(lg_a, c, ms, ls)
        logits_into(lg_a, c + 2)
        return reduce_from(lg_b, c + 1, ms, ls)

    logits_into(lg_a, 0)
    m0 = tuple(jnp.full((1, BLK), NEG_BIG, F32) for _ in range(DSA_HEADS))
    l0 = tuple(jnp.zeros((1, BLK), F32) for _ in range(DSA_HEADS))
    _, ls = lax.fori_loop(0, (n_chunks + 1) // 2, attn_pair, (m0, l0))
    for h in range(DSA_HEADS):
        o_ref[:, h * HEAD_DIM:(h + 1) * HEAD_DIM] = (acc_ref[h] / ls[h]).T.astype(o_ref.dtype)


def _dsa(p, k, vt):
    s = p.shape[0]
    nb = s // BLK
    topk = min(TOPK_MAX, s // 4)
    n_kc = s // KEY_CHUNK
    return pl.pallas_call(
        functools.partial(_dsa_kernel, topk=topk),
        grid=(nb,),
        in_specs=[pl.BlockSpec((BLK, D_DSA), lambda i: (i, OFF_D_Q // D_DSA)),
                  pl.BlockSpec((BLK, IDX_HEADS * IDX_DIM), lambda i: (i, OFF_D_QI // (IDX_HEADS * IDX_DIM))),
                  pl.BlockSpec((BLK, BLK), lambda i: (i, OFF_TAIL // BLK)),
                  _resident((s, BLK), lambda i: (0, OFF_TAIL // BLK)),
                  _resident((s, D_DSA), lambda i: (0, 0)),
                  _resident((n_kc, D_DSA, KEY_CHUNK), lambda i: (0, 0, 0))],
        out_specs=pl.BlockSpec((BLK, D_DSA), lambda i: (i, 0)),
        out_shape=jax.ShapeDtypeStruct((s, D_DSA), BF16),
        scratch_shapes=[pltpu.VMEM((n_kc, KEY_CHUNK, BLK), F32),
                        pltpu.VMEM((n_kc * GROUPS_PER_CHUNK, 32, 8, BLK), I32),
                        pltpu.VMEM((n_kc * GROUPS_PER_CHUNK, 8, BLK), I32),
                        pltpu.VMEM((n_kc + 1, KEY_CHUNK, BLK), F32),
                        pltpu.VMEM((DSA_HEADS, HEAD_DIM, BLK), F32),
                        pltpu.VMEM((DSA_HEADS, KEY_CHUNK, BLK), F32),
                        pltpu.VMEM((DSA_HEADS, KEY_CHUNK, BLK), F32)],
        compiler_params=_params("arbitrary"),
        name="dsa",
    )(p, p, p, p, k, vt)


def _layer_norm(r, g, b):
    mu = jnp.mean(r, axis=1, keepdims=True)
    var = jnp.mean(jnp.square(r - mu), axis=1, keepdims=True)
    return (r - mu) * lax.rsqrt(var + LN_EPS) * g + b


def _first_max_of4(vals):
    a, b, c, d = vals
    m = jnp.maximum(jnp.maximum(a, b), jnp.maximum(c, d))
    idx = jnp.where(a == m, 0, jnp.where(b == m, 1, jnp.where(c == m, 2, 3)))
    return m, idx


def _router_gates(logits_t):
    mx = jnp.max(logits_t, axis=0, keepdims=True)
    e = jnp.exp(logits_t - mx)
    probs = e / jnp.sum(e, axis=0, keepdims=True)
    rows = [probs[j:j + 1, :] for j in range(N_EXPERTS)]
    m1s, m2s, i1s, i2s, scores = [], [], [], [], []
    for g in range(N_GROUPS):
        vals = rows[g * EXPERTS_PER_GROUP:(g + 1) * EXPERTS_PER_GROUP]
        m1, i1 = _first_max_of4(vals)
        rest = [jnp.where(i1 == j, -1.0, vals[j]) for j in range(EXPERTS_PER_GROUP)]
        m2, i2 = _first_max_of4(rest)
        m1s.append(m1); m2s.append(m2); i1s.append(i1); i2s.append(i2); scores.append(m1 + m2)
    best, g_sel = _first_max_of4(scores)
    pick = lambda xs: jnp.where(g_sel == 0, xs[0], jnp.where(g_sel == 1, xs[1], jnp.where(g_sel == 2, xs[2], xs[3])))
    m1, m2, i1, i2 = pick(m1s), pick(m2s), pick(i1s), pick(i2s)
    den = m1 + m2
    w1, w2 = m1 / den, m2 / den
    e1 = g_sel * EXPERTS_PER_GROUP + i1
    e2 = g_sel * EXPERTS_PER_GROUP + i2
    gates = [jnp.where(e1 == j, w1, 0.0) + jnp.where(e2 == j, w2, 0.0) for j in range(N_EXPERTS)]
    return jnp.concatenate(gates, axis=0)


def _out_kernel(ysb_ref, yret_ref, ydsa_ref, wsb_ref, wret_ref, wdsa_ref, x_ref, g_ref, b_ref, wr_ref,
                x1_ref, gates_ref, gates_t_ref, cnt_ref):
    mix = (_dot(ysb_ref[...], wsb_ref[...]) + _dot(yret_ref[...], wret_ref[...])
           + _dot(ydsa_ref[...], wdsa_ref[...]))
    x1 = _layer_norm(DEEPNORM_ALPHA * x_ref[...] + mix, g_ref[...], b_ref[...])
    x1_ref[...] = x1
    logits_t = lax.dot_general(wr_ref[...], x1, (((1,), (1,)), ((), ())),
                               preferred_element_type=F32, precision=lax.Precision.HIGHEST)
    gates_t = _router_gates(logits_t)
    tm = x1.shape[0]
    gates_t_ref[...] = gates_t
    padded = jnp.concatenate([gates_t, jnp.zeros((BLK - N_EXPERTS, tm), F32)], axis=0)
    gates_ref[...] = padded.T
    chosen = jnp.sum(jnp.where(gates_t > 0.0, 1, 0), axis=1, keepdims=True)
    cnt_ref[0] = jnp.broadcast_to(chosen, (N_EXPERTS, BLK))


def _out_proj(ysb, yret, ydsa, wsb, wret, wdsa, x, g, b, wr_t):
    s = x.shape[0]
    tm = MOE_TILE
    row = lambda n: pl.BlockSpec((tm, n), lambda i: (i, 0))
    whole = lambda a: _resident(a.shape, lambda i: (0, 0))
    return pl.pallas_call(
        _out_kernel,
        grid=(s // tm,),
        in_specs=[row(D_SB), row(D_RET), row(D_DSA), whole(wsb), whole(wret), whole(wdsa),
                  row(D_MODEL), whole(g), whole(b), whole(wr_t)],
        out_specs=[row(D_MODEL), row(BLK), pl.BlockSpec((N_EXPERTS, tm), lambda i: (0, i)),
                   pl.BlockSpec((1, N_EXPERTS, BLK), lambda i: (i, 0, 0))],
        out_shape=[jax.ShapeDtypeStruct((s, D_MODEL), F32), jax.ShapeDtypeStruct((s, BLK), F32),
                   jax.ShapeDtypeStruct((N_EXPERTS, s), F32),
                   jax.ShapeDtypeStruct((s // tm, N_EXPERTS, BLK), I32)],
        compiler_params=_params("parallel"),
        name="out_proj",
    )(ysb, yret, ydsa, wsb, wret, wdsa, x, g, b, wr_t)


LOCAL_ROWS = 2 * MOE_TILE + N_EXPERTS * WIN


def _round_up(x, m):
    return (x + m - 1) // m * m


def _sorted_rows(s):
    n_tiles = s // MOE_TILE
    return _round_up(2 * s + n_tiles * N_EXPERTS * (WIN - 1) + N_EXPERTS * (ROW_TILE - 1), ROW_TILE)


def _moe_plan(cnt, s):
    n_row_tiles = _sorted_rows(s) // ROW_TILE
    seg = _round_up(cnt, WIN)
    rows_e = jnp.sum(seg, axis=0)
    region = _round_up(rows_e, ROW_TILE)
    region_off = jnp.cumsum(region) - region
    dest = region_off[None, :] + jnp.cumsum(seg, axis=0) - seg
    tiles_e = region // ROW_TILE
    tile_end = jnp.cumsum(tiles_e)
    k = jnp.arange(n_row_tiles, dtype=I32)
    tile_expert = jnp.minimum(jnp.searchsorted(tile_end, k, side="right"), N_EXPERTS - 1).astype(I32)
    first = (tile_end - tiles_e)[tile_expert]
    valid = jnp.clip(rows_e[tile_expert] - (k - first) * ROW_TILE, 0, ROW_TILE)
    tile_valid = jnp.where(k < tile_end[-1], valid, 0).astype(I32)
    fill = jnp.concatenate([region_off + rows_e, (region - rows_e) // WIN,
                            tile_end[-1:], n_row_tiles - tile_end[-1:]]).astype(I32)
    return cnt.reshape(-1).astype(I32), dest.reshape(-1).astype(I32), fill, tile_expert, tile_valid


def _slot_offsets(cnt_sm, tile):
    offs, o = [], 0
    for e in range(N_EXPERTS):
        offs.append(o)
        o = o + _round_up(cnt_sm[tile * N_EXPERTS + e], WIN)
    return offs


def _window_copies(cnt_sm, dest_sm, tile, offs, local_ref, sorted_hbm, sem, to_sorted):
    total = 0
    for e in range(N_EXPERTS):
        n_win = (cnt_sm[tile * N_EXPERTS + e] + WIN - 1) // WIN
        base_local, base_sorted = offs[e], dest_sm[tile * N_EXPERTS + e]

        def issue(j, _, base_local=base_local, base_sorted=base_sorted):
            loc = local_ref.at[pl.ds(pl.multiple_of(base_local + j * WIN, WIN), WIN)]
            srt = sorted_hbm.at[pl.ds(pl.multiple_of(base_sorted + j * WIN, WIN), WIN)]
            if to_sorted:
                pltpu.make_async_copy(loc, srt, sem).start()
            else:
                pltpu.make_async_copy(srt, loc, sem).start()
            return 0

        lax.fori_loop(0, n_win, issue, 0)
        total = total + n_win
    return total


def _wait_windows(total, local_ref, sorted_hbm, sem):
    def wait(j, _):
        pltpu.make_async_copy(local_ref.at[pl.ds(0, WIN)], sorted_hbm.at[pl.ds(0, WIN)], sem).wait()
        return 0

    lax.fori_loop(0, total, wait, 0)


def _zero_fill(fill_sm, xs_hbm, zero_ref, sem_win, sem_tile):
    zero_ref[...] = jnp.zeros_like(zero_ref)
    win_copy = lambda row: pltpu.make_async_copy(
        zero_ref.at[pl.ds(0, WIN)], xs_hbm.at[pl.ds(pl.multiple_of(row, WIN), WIN)], sem_win)
    tile_copy = lambda row: pltpu.make_async_copy(
        zero_ref, xs_hbm.at[pl.ds(pl.multiple_of(row, ROW_TILE), ROW_TILE)], sem_tile)
    n_pad = 0
    for e in range(N_EXPERTS):
        first, n_win = fill_sm[e], fill_sm[N_EXPERTS + e]

        def issue(j, _, first=first):
            win_copy(first + j * WIN).start()
            return 0

        lax.fori_loop(0, n_win, issue, 0)
        n_pad = n_pad + n_win
    first_tile, n_tail = fill_sm[2 * N_EXPERTS], fill_sm[2 * N_EXPERTS + 1]

    def issue_tile(j, _):
        tile_copy((first_tile + j) * ROW_TILE).start()
        return 0

    def wait_win(j, _):
        win_copy(0).wait()
        return 0

    def wait_tile(j, _):
        tile_copy(0).wait()
        return 0

    lax.fori_loop(0, n_tail, issue_tile, 0)
    lax.fori_loop(0, n_pad, wait_win, 0)
    lax.fori_loop(0, n_tail, wait_tile, 0)


def _dispatch_kernel(cnt_sm, dest_sm, fill_sm, x_ref, gt_ref, xs_hbm, local_ref, zero_ref, sem, sem_win, sem_tile):
    tile = pl.program_id(0)

    @pl.when(tile == 0)
    def _():
        _zero_fill(fill_sm, xs_hbm, zero_ref, sem_win, sem_tile)

    offs = _slot_offsets(cnt_sm, tile)
    chosen = gt_ref[...] > 0.0
    t_r = lax.broadcasted_iota(I32, (MOE_TILE, MOE_TILE), 0)
    t_c = lax.broadcasted_iota(I32, (MOE_TILE, MOE_TILE), 1)
    earlier = _dot(jnp.where(chosen, 1.0, 0.0).astype(BF16), (t_r < t_c).astype(BF16))
    e_id = lax.broadcasted_iota(I32, (N_EXPERTS, 1), 0)
    slot = jnp.zeros((N_EXPERTS, 1), I32)
    for e in range(N_EXPERTS):
        slot = jnp.where(e_id == e, offs[e], slot)
    pos = earlier + slot.astype(F32)
    p_lo = jnp.min(jnp.where(chosen, pos, float(LOCAL_ROWS)), axis=0, keepdims=True).astype(I32)
    p_hi = jnp.max(jnp.where(chosen, pos, -1.0), axis=0, keepdims=True).astype(I32)
    row = lax.broadcasted_iota(I32, (LOCAL_ROWS, MOE_TILE), 0)
    onehot = jnp.where(row == p_lo, 1.0, jnp.where(row == p_hi, 1.0, 0.0)).astype(BF16)
    local_ref[...] = _dot(onehot, x_ref[...].astype(BF16)).astype(BF16)
    total = _window_copies(cnt_sm, dest_sm, tile, offs, local_ref, xs_hbm, sem, True)
    _wait_windows(total, local_ref, xs_hbm, sem)


def _dispatch(x1, gates_t, cnt_flat, dest_flat, fill):
    s = x1.shape[0]
    return pl.pallas_call(
        _dispatch_kernel,
        grid_spec=pltpu.PrefetchScalarGridSpec(
            num_scalar_prefetch=3,
            grid=(s // MOE_TILE,),
            in_specs=[pl.BlockSpec((MOE_TILE, D_MODEL), lambda i, *_: (i, 0)),
                      pl.BlockSpec((N_EXPERTS, MOE_TILE), lambda i, *_: (0, i))],
            out_specs=pl.BlockSpec(memory_space=pl.ANY),
            scratch_shapes=[pltpu.VMEM((LOCAL_ROWS, D_MODEL), BF16), pltpu.VMEM((ROW_TILE, D_MODEL), BF16),
                            pltpu.SemaphoreType.DMA(()), pltpu.SemaphoreType.DMA(()),
                            pltpu.SemaphoreType.DMA(())]),
        out_shape=jax.ShapeDtypeStruct((_sorted_rows(s), D_MODEL), BF16),
        compiler_params=_params("arbitrary"),
        name="moe_dispatch",
    )(cnt_flat, dest_flat, fill, x1, gates_t)


def _expert_kernel(te_sm, tv_sm, xs_ref, wg_ref, wu_ref, wd_ref, y_ref, wg_b, wu_b, wd_b):
    k = pl.program_id(0)
    valid = tv_sm[k]
    new_expert = jnp.logical_or(k == 0, te_sm[k] != te_sm[jnp.maximum(k - 1, 0)])

    @pl.when(jnp.logical_and(valid > 0, new_expert))
    def _():
        wg_b[...] = wg_ref[0, 0].astype(BF16)
        wu_b[...] = wu_ref[0, 0].astype(BF16)
        wd_b[...] = wd_ref[0, 0].astype(BF16)

    @pl.when(valid > 0)
    def _():
        x = xs_ref[...]
        hg = _dot(x, wg_b[...])
        hu = _dot(x, wu_b[...])
        act = hg * jax.nn.sigmoid(hg) * hu
        y_ref[...] = _dot(act.astype(BF16), wd_b[...]).astype(y_ref.dtype)

    @pl.when(valid == 0)
    def _():
        y_ref[...] = jnp.zeros_like(y_ref)


def _experts(xs, tile_expert, tile_valid, w_gate, w_up, w_down, layer):
    rows = xs.shape[0]
    w_in_spec = pl.BlockSpec((1, 1, D_MODEL, D_FF_EXPERT), lambda k, te, tv: (layer, te[k], 0, 0))
    return pl.pallas_call(
        _expert_kernel,
        grid_spec=pltpu.PrefetchScalarGridSpec(
            num_scalar_prefetch=2,
            grid=(rows // ROW_TILE,),
            in_specs=[pl.BlockSpec((ROW_TILE, D_MODEL), lambda k, te, tv: (k, 0)),
                      w_in_spec, w_in_spec,
                      pl.BlockSpec((1, 1, D_FF_EXPERT, D_MODEL), lambda k, te, tv: (layer, te[k], 0, 0))],
            out_specs=pl.BlockSpec((ROW_TILE, D_MODEL), lambda k, te, tv: (k, 0)),
            scratch_shapes=[pltpu.VMEM((D_MODEL, D_FF_EXPERT), BF16), pltpu.VMEM((D_MODEL, D_FF_EXPERT), BF16),
                            pltpu.VMEM((D_FF_EXPERT, D_MODEL), BF16)]),
        out_shape=jax.ShapeDtypeStruct((rows, D_MODEL), BF16),
        compiler_params=_params("arbitrary"),
        name="moe_experts",
    )(tile_expert, tile_valid, xs, w_gate, w_up, w_down)


def _combine_kernel(cnt_sm, dest_sm, x_ref, gates_ref, y_hbm, g_ref, b_ref, o_ref, local_ref, sem):
    tile = pl.program_id(0)

    @pl.when(tile == 0)
    def _():
        local_ref[...] = jnp.zeros_like(local_ref)

    offs = _slot_offsets(cnt_sm, tile)
    total = _window_copies(cnt_sm, dest_sm, tile, offs, local_ref, y_hbm, sem, False)

    gates = gates_ref[...]
    chosen = gates > 0.0
    t_r = lax.broadcasted_iota(I32, (MOE_TILE, MOE_TILE), 0)
    t_c = lax.broadcasted_iota(I32, (MOE_TILE, MOE_TILE), 1)
    earlier = _dot((t_c < t_r).astype(BF16), jnp.where(chosen, 1.0, 0.0).astype(BF16))
    e_id = lax.broadcasted_iota(I32, (1, BLK), 1)
    slot = jnp.zeros((1, BLK), I32)
    for e in range(N_EXPERTS):
        slot = jnp.where(e_id == e, offs[e], slot)
    pos = jnp.where(chosen, earlier + slot.astype(F32), -1.0)
    p_lo = jnp.min(jnp.where(chosen, pos, float(LOCAL_ROWS)), axis=1, keepdims=True)
    p_hi = jnp.max(pos, axis=1, keepdims=True)
    w_lo = jnp.sum(jnp.where(pos == p_lo, gates, 0.0), axis=1, keepdims=True)
    w_hi = jnp.sum(jnp.where(pos == p_hi, gates, 0.0), axis=1, keepdims=True)
    col = lax.broadcasted_iota(I32, (MOE_TILE, LOCAL_ROWS), 1)
    weights = jnp.where(col == p_lo.astype(I32), w_lo, jnp.where(col == p_hi.astype(I32), w_hi, 0.0))

    _wait_windows(total, local_ref, y_hbm, sem)
    ffn = _dot(weights.astype(BF16), local_ref[...])
    o_ref[...] = _layer_norm(DEEPNORM_ALPHA * x_ref[...] + ffn, g_ref[...], b_ref[...])


def _combine(x1, gates, y, cnt_flat, dest_flat, g, b):
    s = x1.shape[0]
    row = lambda n: pl.BlockSpec((MOE_TILE, n), lambda i, *_: (i, 0))
    vec = pl.BlockSpec((1, D_MODEL), lambda i, *_: (0, 0))
    return pl.pallas_call(
        _combine_kernel,
        grid_spec=pltpu.PrefetchScalarGridSpec(
            num_scalar_prefetch=2,
            grid=(s // MOE_TILE,),
            in_specs=[row(D_MODEL), row(BLK), pl.BlockSpec(memory_space=pl.ANY), vec, vec],
            out_specs=row(D_MODEL),
            scratch_shapes=[pltpu.VMEM((LOCAL_ROWS, D_MODEL), BF16), pltpu.SemaphoreType.DMA(())]),
        out_shape=jax.ShapeDtypeStruct((s, D_MODEL), F32),
        compiler_params=_params("arbitrary"),
        name="moe_combine",
    )(cnt_flat, dest_flat, x1, gates, y, g, b)


def _moe(x1, gates, gates_t, cnt, w_gate, w_up, w_down, g, b, layer):
    s = x1.shape[0]
    cnt_flat, dest_flat, fill, tile_expert, tile_valid = _moe_plan(cnt[:, :, 0], s)
    xs = _dispatch(x1, gates_t, cnt_flat, dest_flat, fill)
    y = _experts(xs, tile_expert, tile_valid, w_gate, w_up, w_down, layer)
    return _combine(x1, gates, y, cnt_flat, dest_flat, g, b)


def _reorder_w_in(w):
    sb = w[:, 0:3 * D_SB]
    o = 3 * D_SB
    ret = w[:, o:o + 4 * D_RET]
    o += 4 * D_RET
    d_q = w[:, o:o + D_DSA]
    o += D_DSA
    d_ckv = w[:, o:o + KV_RANK]
    o += KV_RANK
    d_qi = w[:, o:o + IDX_HEADS * IDX_DIM]
    o += IDX_HEADS * IDX_DIM
    tail = w[:, o:]
    parts = [sb, d_q, ret, d_qi, d_ckv, tail]
    width = sum(a.shape[1] for a in parts)
    parts.append(jnp.zeros((w.shape[0], D_PROJ - width), w.dtype))
    return jnp.concatenate(parts, axis=1).astype(BF16)


def kernel(x, w_in, w_kv_up, kv_norm_g, ret_gn_g, w_o, ln1_g, ln1_b, w_router, w_gate, w_up, w_down,
           ln2_g, ln2_b):
    b, s, _ = x.shape
    assert b == 1 and s % KEY_CHUNK == 0
    h = x[0]
    tables = _retention_tables(s)
    wr_t = w_router.T
    for l in range(DEPTH):
        p = _proj(h, _reorder_w_in(w_in[l]))
        y_sb = _stick_breaking(p)
        y_ret = _retention(p, ret_gn_g[l][None, :], tables)
        w_kv = w_kv_up[l].reshape(KV_RANK, DSA_HEADS, 2, HEAD_DIM)
        w_kv = jnp.concatenate([w_kv[:, :, 0, :].reshape(KV_RANK, D_DSA),
                                w_kv[:, :, 1, :].reshape(KV_RANK, D_DSA)], axis=1).astype(BF16)
        k_dsa, vt_dsa = _kv_up(p, kv_norm_g[l][None, :], w_kv)
        y_dsa = _dsa(p, k_dsa, vt_dsa)
        wo = w_o[l].astype(BF16)
        x1, gates, gates_t, cnt = _out_proj(y_sb, y_ret, y_dsa, wo[0:D_SB], wo[D_SB:D_SB + D_RET],
                                            wo[D_SB + D_RET:], h, ln1_g[l][None, :], ln1_b[l][None, :], wr_t)
        h = _moe(x1, gates, gates_t, cnt, w_gate, w_up, w_down, ln2_g[l][None, :], ln2_b[l][None, :], l)
    return h[None]
```

```python
import functools

import numpy as np
import jax
import jax.numpy as jnp
from jax import lax
from jax.experimental import pallas as pl
from jax.experimental.pallas import tpu as pltpu

F32 = jnp.float32
BF16 = jnp.bfloat16
I32 = jnp.int32

D_MODEL = 2048
HEAD_DIM = 128
SB_HEADS = 6
RET_HEADS = 4
DSA_HEADS = 6
D_SB = SB_HEADS * HEAD_DIM
D_RET = RET_HEADS * HEAD_DIM
D_DSA = DSA_HEADS * HEAD_DIM
KV_RANK = 256
IDX_HEADS = 8
IDX_DIM = 64
IDX_SCALE = IDX_DIM ** -0.5 * IDX_HEADS ** -0.5
TOPK_MAX = 256
BLK = 128
N_EXPERTS = 16
N_GROUPS = 4
EXPERTS_PER_GROUP = N_EXPERTS // N_GROUPS
D_FF_EXPERT = 512
LN_EPS = 1e-5
RMS_EPS = 1e-6
GN_EPS = 1e-6
DEPTH = 2
DEEPNORM_ALPHA = (2 * DEPTH) ** 0.25

OFF_SB_Q = 0
OFF_SB_K = OFF_SB_Q + D_SB
OFF_SB_V = OFF_SB_K + D_SB
OFF_D_Q = OFF_SB_V + D_SB
OFF_R_Q = OFF_D_Q + D_DSA
OFF_R_K = OFF_R_Q + D_RET
OFF_R_V = OFF_R_K + D_RET
OFF_R_G = OFF_R_V + D_RET
OFF_D_QI = OFF_R_G + D_RET
OFF_D_CKV = OFF_D_QI + IDX_HEADS * IDX_DIM
OFF_TAIL = OFF_D_CKV + KV_RANK
D_PROJ = 6144

KEY_CHUNK = 512
MOE_TILE = 512
WIN = 16
ROW_TILE = 512
VMEM_LIMIT = 56 * 1024 * 1024
LOG2_E = 1.4426950408889634
NEG_BIG = -1e30
EXP_UNDERFLOW = -104.0
KEY_OF_NEG_INF = -2139095041
INT_MIN = -2147483648


def _dot(a, b):
    return jnp.dot(a, b, preferred_element_type=F32)


def _dot_nt(a, b):
    return lax.dot_general(a, b, (((1,), (1,)), ((), ())), preferred_element_type=F32)


def _dot_tn(a, b):
    return lax.dot_general(a, b, (((0,), (0,)), ((), ())), preferred_element_type=F32)


def _params(*sem):
    return pltpu.CompilerParams(dimension_semantics=sem, vmem_limit_bytes=VMEM_LIMIT)


def _resident(shape, index_map):
    return pl.BlockSpec(shape, index_map, pipeline_mode=pl.Buffered(1))


def _proj_kernel(x_ref, w_ref, o_ref):
    o_ref[...] = _dot(x_ref[...].astype(BF16), w_ref[...]).astype(o_ref.dtype)


def _proj(x, w):
    s, d = x.shape
    n = w.shape[1]
    tm = min(1024, s)
    tn = 1536
    return pl.pallas_call(
        _proj_kernel,
        grid=(s // tm, n // tn),
        in_specs=[pl.BlockSpec((tm, d), lambda i, j: (i, 0)),
                  pl.BlockSpec((d, tn), lambda i, j: (0, j))],
        out_specs=pl.BlockSpec((tm, tn), lambda i, j: (i, j)),
        out_shape=jax.ShapeDtypeStruct((s, n), BF16),
        compiler_params=_params("parallel", "arbitrary"),
        name="proj",
    )(x, w)


def _sb_kernel(q_ref, k_ref, v_ref, o_ref, acc_ref):
    i = pl.program_id(0)
    q = q_ref[...]
    scale = HEAD_DIM ** -0.5
    key_pos = lax.broadcasted_iota(I32, (BLK, BLK), 0)
    qry_pos = lax.broadcasted_iota(I32, (BLK, BLK), 1)
    later = (qry_pos > key_pos).astype(BF16)
    acc_ref[...] = jnp.zeros_like(acc_ref)

    def cond(carry):
        j, cs = carry
        c_max = functools.reduce(jnp.maximum, cs)
        return jnp.logical_and(j >= 0, jnp.max(c_max) > EXP_UNDERFLOW)

    def body(carry):
        j, cs = carry
        off = pl.multiple_of(j * BLK, BLK)
        strict = (off + key_pos) < (i * BLK + qry_pos)
        new_cs = []
        for h in range(SB_HEADS):
            hs = slice(h * HEAD_DIM, (h + 1) * HEAD_DIM)
            z = _dot_nt(k_ref[pl.ds(off, BLK), hs], q[:, hs]) * scale
            sp = jnp.maximum(z, 0.0) + jnp.log1p(jnp.exp(-jnp.abs(z)))
            log_rem = jnp.where(strict, -sp, 0.0)
            hi = log_rem.astype(BF16)
            lo = (log_rem - hi.astype(F32)).astype(BF16)
            after = _dot(later, hi) + _dot(later, lo)
            a = jnp.where(strict, jnp.exp(z - sp + after + cs[h]), 0.0)
            acc_ref[h] += _dot_tn(a.astype(BF16), v_ref[pl.ds(off, BLK), hs])
            new_cs.append(cs[h] + jnp.sum(log_rem, axis=0, keepdims=True))
        return j - 1, tuple(new_cs)

    lax.while_loop(cond, body, (i, tuple(jnp.zeros((1, BLK), F32) for _ in range(SB_HEADS))))
    for h in range(SB_HEADS):
        o_ref[:, h * HEAD_DIM:(h + 1) * HEAD_DIM] = acc_ref[h].astype(o_ref.dtype)


def _stick_breaking(p):
    s = p.shape[0]
    nb = s // BLK
    return pl.pallas_call(
        _sb_kernel,
        grid=(nb,),
        in_specs=[pl.BlockSpec((BLK, D_SB), lambda i: (i, OFF_SB_Q // D_SB)),
                  _resident((s, D_SB), lambda i: (0, OFF_SB_K // D_SB)),
                  _resident((s, D_SB), lambda i: (0, OFF_SB_V // D_SB))],
        out_specs=pl.BlockSpec((BLK, D_SB), lambda i: (i, 0)),
        out_shape=jax.ShapeDtypeStruct((s, D_SB), BF16),
        scratch_shapes=[pltpu.VMEM((SB_HEADS, BLK, HEAD_DIM), F32)],
        compiler_params=_params("arbitrary"),
        name="stick_breaking",
    )(p, p, p)


def _ret_kernel(q_ref, k_ref, v_ref, g_ref, cos_ref, sin_ref, intra_ref, qd_ref, kd_ref, cd_ref,
                gn_ref, o_ref, state_ref):
    n = pl.program_id(0)

    @pl.when(n == 0)
    def _():
        state_ref[...] = jnp.zeros_like(state_ref)

    cos = cos_ref[...]
    sin = sin_ref[...]
    for h in range(RET_HEADS):
        hs = slice(h * HEAD_DIM, (h + 1) * HEAD_DIM)
        q = q_ref[:, hs].astype(F32)
        k = k_ref[:, hs].astype(F32)
        v = v_ref[:, hs]
        qr = q * cos + pltpu.roll(q, HEAD_DIM // 2, 1) * sin
        kr = (k * cos + pltpu.roll(k, HEAD_DIM // 2, 1) * sin) * (HEAD_DIM ** -0.5)
        scores = _dot_nt(qr.astype(BF16), kr.astype(BF16)) * intra_ref[h]
        state = state_ref[h]
        o = (_dot(scores.astype(BF16), v)
             + _dot((qr * qd_ref[h]).astype(BF16), state.astype(BF16)))
        state_ref[h] = cd_ref[h] * state + _dot_tn((kr * kd_ref[h]).astype(BF16), v)
        mu = jnp.mean(o, axis=1, keepdims=True)
        var = jnp.mean(jnp.square(o - mu), axis=1, keepdims=True)
        on = (o - mu) * lax.rsqrt(var + GN_EPS) * gn_ref[:, hs]
        g = g_ref[:, hs].astype(F32)
        o_ref[:, hs] = (g * jax.nn.sigmoid(g) * on).astype(o_ref.dtype)


def _retention_tables(s):
    f32 = np.float32
    half = HEAD_DIM // 2
    pos = np.arange(s, dtype=f32)
    theta = (f32(10000.0) ** (-np.linspace(0.0, 1.0, half, dtype=f32))).astype(f32)
    ang = (pos[:, None] * theta[None, :]).astype(f32)
    cos, sin = np.cos(ang).astype(f32), np.sin(ang).astype(f32)
    cos2 = np.concatenate([cos, cos], axis=1)
    sin2 = np.concatenate([-sin, sin], axis=1)
    log_gamma = np.log1p(-(f32(2.0) ** (-5.0 - np.arange(RET_HEADS, dtype=f32)))).astype(f32)
    idx = np.arange(BLK, dtype=f32)
    diff = idx[:, None] - idx[None, :]
    intra = np.where(diff >= 0, np.exp(np.maximum(diff, 0.0)[None] * log_gamma[:, None, None]), 0.0).astype(f32)
    q_decay = np.exp((idx[None, :] + 1.0) * log_gamma[:, None]).astype(f32)
    k_decay = np.exp((BLK - 1.0 - idx[None, :]) * log_gamma[:, None]).astype(f32)
    chunk_decay = np.exp(BLK * log_gamma).astype(f32)
    full = (RET_HEADS, BLK, HEAD_DIM)
    return tuple(jnp.asarray(np.ascontiguousarray(t)) for t in (
        cos2, sin2, intra,
        np.broadcast_to(q_decay[:, :, None], full),
        np.broadcast_to(k_decay[:, :, None], full),
        np.broadcast_to(chunk_decay[:, None, None], full)))


def _retention(p, gn_g, tables):
    s = p.shape[0]
    nc = s // BLK
    cos2, sin2, intra, qd, kd, cd = tables
    col = lambda off: pl.BlockSpec((BLK, D_RET), lambda n: (n, off // D_RET))
    per_head = pl.BlockSpec((RET_HEADS, BLK, HEAD_DIM), lambda n: (0, 0, 0))
    pos_spec = pl.BlockSpec((BLK, HEAD_DIM), lambda n: (n, 0))
    return pl.pallas_call(
        _ret_kernel,
        grid=(nc,),
        in_specs=[col(OFF_R_Q), col(OFF_R_K), col(OFF_R_V), col(OFF_R_G),
                  pos_spec, pos_spec, per_head, per_head, per_head, per_head,
                  pl.BlockSpec((1, D_RET), lambda n: (0, 0))],
        out_specs=pl.BlockSpec((BLK, D_RET), lambda n: (n, 0)),
        out_shape=jax.ShapeDtypeStruct((s, D_RET), BF16),
        scratch_shapes=[pltpu.VMEM((RET_HEADS, HEAD_DIM, HEAD_DIM), F32)],
        compiler_params=_params("arbitrary"),
        name="retention",
    )(p, p, p, p, cos2, sin2, intra, qd, kd, cd, gn_g)


def _kv_up_kernel(c_ref, g_ref, w_ref, k_ref, vt_ref):
    c = c_ref[...].astype(F32)
    y = c * lax.rsqrt(jnp.mean(jnp.square(c), axis=1, keepdims=True) + RMS_EPS) * g_ref[...]
    kv = _dot(y.astype(BF16), w_ref[...])
    k_ref[...] = (kv[:, :D_DSA] * (HEAD_DIM ** -0.5 * LOG2_E)).astype(k_ref.dtype)
    vt_ref[0] = kv[:, D_DSA:].T.astype(vt_ref.dtype)


def _kv_up(p, g, w):
    s = p.shape[0]
    n = w.shape[1]
    return pl.pallas_call(
        _kv_up_kernel,
        grid=(s // KEY_CHUNK,),
        in_specs=[pl.BlockSpec((KEY_CHUNK, KV_RANK), lambda i: (i, OFF_D_CKV // KV_RANK)),
                  pl.BlockSpec((1, KV_RANK), lambda i: (0, 0)),
                  pl.BlockSpec((KV_RANK, n), lambda i: (0, 0))],
        out_specs=[pl.BlockSpec((KEY_CHUNK, D_DSA), lambda i: (i, 0)),
                   pl.BlockSpec((1, D_DSA, KEY_CHUNK), lambda i: (i, 0, 0))],
        out_shape=[jax.ShapeDtypeStruct((s, D_DSA), BF16),
                   jax.ShapeDtypeStruct((s // KEY_CHUNK, D_DSA, KEY_CHUNK), BF16)],
        compiler_params=_params("parallel"),
        name="kv_up",
    )(p, g, w)


def _ordered_bits_to_float(u):
    return pltpu.bitcast(u ^ ((u >> 31) & 0x7FFFFFFF), F32)


GROUPS_PER_CHUNK = KEY_CHUNK // (32 * 8)


def _bit_transpose32(words):
    a = list(words)
    j, mask = 16, 0x0000FFFF
    while j:
        k = 0
        while k < 32:
            t = (a[k] ^ (a[k + j] >> j)) & mask
            a[k] = a[k] ^ t
            a[k + j] = a[k + j] ^ (t << j)
            k = (k + j + 1) & ~j
        j >>= 1
        mask = (mask ^ (mask << j)) & 0xFFFFFFFF
    return a


def _dsa_kernel(q_ref, qi_ref, tq_ref, tail_ref, k_ref, vt_ref, o_ref,
                score_ref, planes_ref, alive_ref, bias_ref, acc_ref, lg_a, lg_b, *, topk):
    i = pl.program_id(0)
    n_chunks = ((i + 1) * BLK + KEY_CHUNK - 1) // KEY_CHUNK
    n_kc = score_ref.shape[0]

    w_t = tq_ref[...].astype(F32).T[IDX_DIM:IDX_DIM + IDX_HEADS, :] * IDX_SCALE
    qi = qi_ref[...]
    qi_rows = jnp.concatenate([qi[:, h * IDX_DIM:(h + 1) * IDX_DIM] for h in range(IDX_HEADS)], axis=0)

    def chunk_scores(c):
        off = pl.multiple_of(c * KEY_CHUNK, KEY_CHUNK)
        rel = jnp.maximum(_dot_nt(tail_ref[pl.ds(off, KEY_CHUNK), 0:IDX_DIM], qi_rows), 0.0)
        score = rel[:, 0:BLK] * w_t[0:1, :]
        for h in range(1, IDX_HEADS):
            score = score + rel[:, h * BLK:(h + 1) * BLK] * w_t[h:h + 1, :]
        return score

    def store_chunk(c, score):
        score = jnp.where(score == 0.0, 0.0, score)
        score_ref[c] = score
        bits = pltpu.bitcast(score, I32)
        u = bits ^ ((bits >> 31) | INT_MIN)
        for g in range(GROUPS_PER_CHUNK):
            words = [u[(g * 32 + j) * 8:(g * 32 + j + 1) * 8, :] for j in range(32)]
            planes = _bit_transpose32(words)
            for b in range(32):
                planes_ref[c * GROUPS_PER_CHUNK + g, b] = planes[b]

    def score_chunk(c, _):
        store_chunk(c, chunk_scores(c))
        return 0

    lax.fori_loop(0, n_chunks - 1, score_chunk, 0)
    last = n_chunks - 1
    key_pos = last * KEY_CHUNK + lax.broadcasted_iota(I32, (KEY_CHUNK, BLK), 0)
    t_col = i * BLK + lax.broadcasted_iota(I32, (KEY_CHUNK, BLK), 1)
    store_chunk(last, jnp.where(key_pos <= t_col, chunk_scores(last), -jnp.inf))

    def init_alive(c, _):
        for g in range(GROUPS_PER_CHUNK):
            alive_ref[c * GROUPS_PER_CHUNK + g] = jnp.full((8, BLK), -1, I32)
        return 0

    lax.fori_loop(0, n_chunks, init_alive, 0)

    @pl.when(n_chunks % 2 == 1)
    def _():
        for g in range(GROUPS_PER_CHUNK):
            alive_ref[n_chunks * GROUPS_PER_CHUNK + g] = jnp.zeros((8, BLK), I32)
            planes_ref[n_chunks * GROUPS_PER_CHUNK + g] = jnp.zeros((32, 8, BLK), I32)

    def decide(above, t_bits, ones, bit):
        take = (above + ones) >= topk
        return (jnp.where(take, above, above + ones), jnp.where(take, t_bits | bit, t_bits),
                jnp.where(take, 0, -1))

    groups_per_step = 2 * GROUPS_PER_CHUNK

    def sweep(plane_prev, plane, drop_prev):
        def step(p, acc):
            for g in range(groups_per_step):
                gi = p * groups_per_step + g
                alive = alive_ref[gi]
                if plane_prev is not None:
                    alive = alive & (planes_ref[gi, plane_prev] ^ drop_prev)
                    alive_ref[gi] = alive
                if plane is not None:
                    alive = alive & planes_ref[gi, plane]
                acc = acc + lax.population_count(alive)
            return acc

        acc = lax.fori_loop(0, (n_chunks + 1) // 2, step, jnp.zeros((8, BLK), I32))
        return jnp.sum(acc, axis=0, keepdims=True)

    zero = jnp.zeros((1, BLK), I32)
    state = decide(zero, zero, sweep(None, 0, None), INT_MIN)

    def bit_step(b, state):
        above, t_bits, drop = state
        return decide(above, t_bits, sweep(b - 1, b, drop), jnp.int32(1) << (31 - b))

    above, t_bits, drop = lax.fori_loop(1, 32, bit_step, state)
    n_eq = sweep(31, None, drop)
    need = topk - above
    thr = t_bits ^ INT_MIN
    real = thr > KEY_OF_NEG_INF
    thr_f = jnp.where(real, _ordered_bits_to_float(thr), jnp.finfo(F32).min)

    def bias_chunk(c, _):
        bias_ref[c] = jnp.where(score_ref[c] >= thr_f, 0.0, NEG_BIG)
        return 0

    lax.fori_loop(0, n_chunks, bias_chunk, 0)

    has_tie = jnp.max(jnp.where(jnp.logical_and(real, n_eq > need), 1, 0)) > 0

    @pl.when(has_tie)
    def _():
        need_f = need.astype(F32)
        r = lax.broadcasted_iota(I32, (BLK, BLK), 0)
        cc = lax.broadcasted_iota(I32, (BLK, BLK), 1)
        upto = (cc <= r).astype(BF16)

        def tie_chunk(c, run):
            sc = score_ref[c]
            for u in range(KEY_CHUNK // BLK):
                st = sc[u * BLK:(u + 1) * BLK, :]
                eq = jnp.logical_and(st == thr_f, real)
                eqf = jnp.where(eq, 1.0, 0.0)
                rank = _dot(upto, eqf.astype(BF16)) + run
                sel = jnp.logical_or(st > thr_f, jnp.logical_and(eq, rank <= need_f))
                bias_ref[c, u * BLK:(u + 1) * BLK, :] = jnp.where(
                    real, jnp.where(sel, 0.0, NEG_BIG), jnp.where(st >= thr_f, 0.0, NEG_BIG))
                run = run + jnp.sum(eqf, axis=0, keepdims=True)
            return run

        lax.fori_loop(0, n_chunks, tie_chunk, jnp.zeros((1, BLK), F32))

    acc_ref[...] = jnp.zeros_like(acc_ref)
    bias_ref[n_kc] = jnp.full((KEY_CHUNK, BLK), NEG_BIG, F32)
    q = q_ref[...]

    def logits_into(lg, c):
        kc = jnp.minimum(c, n_chunks - 1)
        off = pl.multiple_of(kc * KEY_CHUNK, KEY_CHUNK)
        bias = bias_ref[jnp.where(c < n_chunks, c, n_kc)]
        for h in range(DSA_HEADS):
            hs = slice(h * HEAD_DIM, (h + 1) * HEAD_DIM)
            lg[h] = _dot_nt(k_ref[pl.ds(off, KEY_CHUNK), hs], q[:, hs]) + bias

    def reduce_from(lg, c, ms, ls):
        vc = jnp.minimum(c, n_chunks - 1)
        new_m, new_l = [], []
        for h in range(DSA_HEADS):
            hs = slice(h * HEAD_DIM, (h + 1) * HEAD_DIM)
            logits = lg[h]
            m_new = jnp.maximum(ms[h], jnp.max(logits, axis=0, keepdims=True))
            alpha = jnp.exp2(ms[h] - m_new)
            pr = jnp.exp2(logits - m_new)
            new_l.append(alpha * ls[h] + jnp.sum(pr, axis=0, keepdims=True))
            acc_ref[h] = alpha * acc_ref[h] + _dot(vt_ref[vc, hs, :], pr.astype(BF16))
            new_m.append(m_new)
        return tuple(new_m), tuple(new_l)

    def attn_pair(pair, carry):
        ms, ls = carry
        c = 2 * pair
        logits_into(lg_b, c + 1)
        ms, ls = reduce_from(lg_a, c, ms, ls)
        logits_into(lg_a, c + 2)
        return reduce_from(lg_b, c + 1, ms, ls)

    logits_into(lg_a, 0)
    m0 = tuple(jnp.full((1, BLK), NEG_BIG, F32) for _ in range(DSA_HEADS))
    l0 = tuple(jnp.zeros((1, BLK), F32) for _ in range(DSA_HEADS))
    _, ls = lax.fori_loop(0, (n_chunks + 1) // 2, attn_pair, (m0, l0))
    for h in range(DSA_HEADS):
        o_ref[:, h * HEAD_DIM:(h + 1) * HEAD_DIM] = (acc_ref[h] / ls[h]).T.astype(o_ref.dtype)


def _dsa(p, k, vt):
    s = p.shape[0]
    nb = s // BLK
    topk = min(TOPK_MAX, s // 4)
    n_kc = s // KEY_CHUNK
    return pl.pallas_call(
        functools.partial(_dsa_kernel, topk=topk),
        grid=(nb,),
        in_specs=[pl.BlockSpec((BLK, D_DSA), lambda i: (i, OFF_D_Q // D_DSA)),
                  pl.BlockSpec((BLK, IDX_HEADS * IDX_DIM), lambda i: (i, OFF_D_QI // (IDX_HEADS * IDX_DIM))),
                  pl.BlockSpec((BLK, BLK), lambda i: (i, OFF_TAIL // BLK)),
                  _resident((s, BLK), lambda i: (0, OFF_TAIL // BLK)),
                  _resident((s, D_DSA), lambda i: (0, 0)),
                  _resident((n_kc, D_DSA, KEY_CHUNK), lambda i: (0, 0, 0))],
        out_specs=pl.BlockSpec((BLK, D_DSA), lambda i: (i, 0)),
        out_shape=jax.ShapeDtypeStruct((s, D_DSA), BF16),
        scratch_shapes=[pltpu.VMEM((n_kc, KEY_CHUNK, BLK), F32),
                        pltpu.VMEM((n_kc * GROUPS_PER_CHUNK, 32, 8, BLK), I32),
                        pltpu.VMEM((n_kc * GROUPS_PER_CHUNK, 8, BLK), I32),
                        pltpu.VMEM((n_kc + 1, KEY_CHUNK, BLK), F32),
                        pltpu.VMEM((DSA_HEADS, HEAD_DIM, BLK), F32),
                        pltpu.VMEM((DSA_HEADS, KEY_CHUNK, BLK), F32),
                        pltpu.VMEM((DSA_HEADS, KEY_CHUNK, BLK), F32)],
        compiler_params=_params("arbitrary"),
        name="dsa",
    )(p, p, p, p, k, vt)


def _layer_norm(r, g, b):
    mu = jnp.mean(r, axis=1, keepdims=True)
    var = jnp.mean(jnp.square(r - mu), axis=1, keepdims=True)
    return (r - mu) * lax.rsqrt(var + LN_EPS) * g + b


def _first_max_of4(vals):
    a, b, c, d = vals
    m = jnp.maximum(jnp.maximum(a, b), jnp.maximum(c, d))
    idx = jnp.where(a == m, 0, jnp.where(b == m, 1, jnp.where(c == m, 2, 3)))
    return m, idx


def _router_gates(logits_t):
    mx = jnp.max(logits_t, axis=0, keepdims=True)
    e = jnp.exp(logits_t - mx)
    probs = e / jnp.sum(e, axis=0, keepdims=True)
    rows = [probs[j:j + 1, :] for j in range(N_EXPERTS)]
    m1s, m2s, i1s, i2s, scores = [], [], [], [], []
    for g in range(N_GROUPS):
        vals = rows[g * EXPERTS_PER_GROUP:(g + 1) * EXPERTS_PER_GROUP]
        m1, i1 = _first_max_of4(vals)
        rest = [jnp.where(i1 == j, -1.0, vals[j]) for j in range(EXPERTS_PER_GROUP)]
        m2, i2 = _first_max_of4(rest)
        m1s.append(m1); m2s.append(m2); i1s.append(i1); i2s.append(i2); scores.append(m1 + m2)
    best, g_sel = _first_max_of4(scores)
    pick = lambda xs: jnp.where(g_sel == 0, xs[0], jnp.where(g_sel == 1, xs[1], jnp.where(g_sel == 2, xs[2], xs[3])))
    m1, m2, i1, i2 = pick(m1s), pick(m2s), pick(i1s), pick(i2s)
    den = m1 + m2
    w1, w2 = m1 / den, m2 / den
    e1 = g_sel * EXPERTS_PER_GROUP + i1
    e2 = g_sel * EXPERTS_PER_GROUP + i2
    gates = [jnp.where(e1 == j, w1, 0.0) + jnp.where(e2 == j, w2, 0.0) for j in range(N_EXPERTS)]
    return jnp.concatenate(gates, axis=0)


def _out_kernel(ysb_ref, yret_ref, ydsa_ref, wsb_ref, wret_ref, wdsa_ref, x_ref, g_ref, b_ref, wr_ref,
                x1_ref, gates_ref, gates_t_ref, cnt_ref):
    mix = (_dot(ysb_ref[...], wsb_ref[...]) + _dot(yret_ref[...], wret_ref[...])
           + _dot(ydsa_ref[...], wdsa_ref[...]))
    x1 = _layer_norm(DEEPNORM_ALPHA * x_ref[...] + mix, g_ref[...], b_ref[...])
    x1_ref[...] = x1
    logits_t = lax.dot_general(wr_ref[...], x1, (((1,), (1,)), ((), ())),
                               preferred_element_type=F32, precision=lax.Precision.HIGHEST)
    gates_t = _router_gates(logits_t)
    tm = x1.shape[0]
    gates_t_ref[...] = gates_t
    padded = jnp.concatenate([gates_t, jnp.zeros((BLK - N_EXPERTS, tm), F32)], axis=0)
    gates_ref[...] = padded.T
    chosen = jnp.sum(jnp.where(gates_t > 0.0, 1, 0), axis=1, keepdims=True)
    cnt_ref[0] = jnp.broadcast_to(chosen, (N_EXPERTS, BLK))


def _out_proj(ysb, yret, ydsa, wsb, wret, wdsa, x, g, b, wr_t):
    s = x.shape[0]
    tm = MOE_TILE
    row = lambda n: pl.BlockSpec((tm, n), lambda i: (i, 0))
    whole = lambda a: _resident(a.shape, lambda i: (0, 0))
    return pl.pallas_call(
        _out_kernel,
        grid=(s // tm,),
        in_specs=[row(D_SB), row(D_RET), row(D_DSA), whole(wsb), whole(wret), whole(wdsa),
                  row(D_MODEL), whole(g), whole(b), whole(wr_t)],
        out_specs=[row(D_MODEL), row(BLK), pl.BlockSpec((N_EXPERTS, tm), lambda i: (0, i)),
                   pl.BlockSpec((1, N_EXPERTS, BLK), lambda i: (i, 0, 0))],
        out_shape=[jax.ShapeDtypeStruct((s, D_MODEL), F32), jax.ShapeDtypeStruct((s, BLK), F32),
                   jax.ShapeDtypeStruct((N_EXPERTS, s), F32),
                   jax.ShapeDtypeStruct((s // tm, N_EXPERTS, BLK), I32)],
        compiler_params=_params("parallel"),
        name="out_proj",
    )(ysb, yret, ydsa, wsb, wret, wdsa, x, g, b, wr_t)


LOCAL_ROWS = 2 * MOE_TILE + N_EXPERTS * WIN


def _round_up(x, m):
    return (x + m - 1) // m * m


def _sorted_rows(s):
    n_tiles = s // MOE_TILE
    return _round_up(2 * s + n_tiles * N_EXPERTS * (WIN - 1) + N_EXPERTS * (ROW_TILE - 1), ROW_TILE)


def _moe_plan(cnt, s):
    n_row_tiles = _sorted_rows(s) // ROW_TILE
    seg = _round_up(cnt, WIN)
    rows_e = jnp.sum(seg, axis=0)
    region = _round_up(rows_e, ROW_TILE)
    region_off = jnp.cumsum(region) - region
    dest = region_off[None, :] + jnp.cumsum(seg, axis=0) - seg
    tiles_e = region // ROW_TILE
    tile_end = jnp.cumsum(tiles_e)
    k = jnp.arange(n_row_tiles, dtype=I32)
    tile_expert = jnp.minimum(jnp.sum((k[:, None] >= tile_end[None, :]).astype(I32), axis=1), N_EXPERTS - 1)
    first = (tile_end - tiles_e)[tile_expert]
    valid = jnp.clip(rows_e[tile_expert] - (k - first) * ROW_TILE, 0, ROW_TILE)
    tile_valid = jnp.where(k < tile_end[-1], valid, 0).astype(I32)
    fill = jnp.concatenate([region_off + rows_e, (region - rows_e) // WIN,
                            tile_end[-1:], n_row_tiles - tile_end[-1:]]).astype(I32)
    return cnt.reshape(-1).astype(I32), dest.reshape(-1).astype(I32), fill, tile_expert, tile_valid


def _slot_offsets(cnt_sm, tile):
    offs, o = [], 0
    for e in range(N_EXPERTS):
        offs.append(o)
        o = o + _round_up(cnt_sm[tile * N_EXPERTS + e], WIN)
    return offs


def _window_copies(cnt_sm, dest_sm, tile, offs, local_ref, sorted_hbm, sem, to_sorted):
    total = 0
    for e in range(N_EXPERTS):
        n_win = (cnt_sm[tile * N_EXPERTS + e] + WIN - 1) // WIN
        base_local, base_sorted = offs[e], dest_sm[tile * N_EXPERTS + e]

        def issue(j, _, base_local=base_local, base_sorted=base_sorted):
            loc = local_ref.at[pl.ds(pl.multiple_of(base_local + j * WIN, WIN), WIN)]
            srt = sorted_hbm.at[pl.ds(pl.multiple_of(base_sorted + j * WIN, WIN), WIN)]
            if to_sorted:
                pltpu.make_async_copy(loc, srt, sem).start()
            else:
                pltpu.make_async_copy(srt, loc, sem).start()
            return 0

        lax.fori_loop(0, n_win, issue, 0)
        total = total + n_win
    return total


def _wait_windows(total, local_ref, sorted_hbm, sem):
    def wait(j, _):
        pltpu.make_async_copy(local_ref.at[pl.ds(0, WIN)], sorted_hbm.at[pl.ds(0, WIN)], sem).wait()
        return 0

    lax.fori_loop(0, total, wait, 0)


def _zero_fill(fill_sm, xs_hbm, zero_ref, sem_win, sem_tile):
    zero_ref[...] = jnp.zeros_like(zero_ref)
    win_copy = lambda row: pltpu.make_async_copy(
        zero_ref.at[pl.ds(0, WIN)], xs_hbm.at[pl.ds(pl.multiple_of(row, WIN), WIN)], sem_win)
    tile_copy = lambda row: pltpu.make_async_copy(
        zero_ref, xs_hbm.at[pl.ds(pl.multiple_of(row, ROW_TILE), ROW_TILE)], sem_tile)
    n_pad = 0
    for e in range(N_EXPERTS):
        first, n_win = fill_sm[e], fill_sm[N_EXPERTS + e]

        def issue(j, _, first=first):
            win_copy(first + j * WIN).start()
            return 0

        lax.fori_loop(0, n_win, issue, 0)
        n_pad = n_pad + n_win
    first_tile, n_tail = fill_sm[2 * N_EXPERTS], fill_sm[2 * N_EXPERTS + 1]

    def issue_tile(j, _):
        tile_copy((first_tile + j) * ROW_TILE).start()
        return 0

    def wait_win(j, _):
        win_copy(0).wait()
        return 0

    def wait_tile(j, _):
        tile_copy(0).wait()
        return 0

    lax.fori_loop(0, n_tail, issue_tile, 0)
    lax.fori_loop(0, n_pad, wait_win, 0)
    lax.fori_loop(0, n_tail, wait_tile, 0)


def _dispatch_kernel(cnt_sm, dest_sm, fill_sm, x_ref, gt_ref, xs_hbm, local_ref, zero_ref, sem, sem_win, sem_tile):
    tile = pl.program_id(0)

    @pl.when(tile == 0)
    def _():
        _zero_fill(fill_sm, xs_hbm, zero_ref, sem_win, sem_tile)

    offs = _slot_offsets(cnt_sm, tile)
    chosen = gt_ref[...] > 0.0
    t_r = lax.broadcasted_iota(I32, (MOE_TILE, MOE_TILE), 0)
    t_c = lax.broadcasted_iota(I32, (MOE_TILE, MOE_TILE), 1)
    earlier = _dot(jnp.where(chosen, 1.0, 0.0).astype(BF16), (t_r < t_c).astype(BF16))
    e_id = lax.broadcasted_iota(I32, (N_EXPERTS, 1), 0)
    slot = jnp.zeros((N_EXPERTS, 1), I32)
    for e in range(N_EXPERTS):
        slot = jnp.where(e_id == e, offs[e], slot)
    pos = earlier + slot.astype(F32)
    p_lo = jnp.min(jnp.where(chosen, pos, float(LOCAL_ROWS)), axis=0, keepdims=True).astype(I32)
    p_hi = jnp.max(jnp.where(chosen, pos, -1.0), axis=0, keepdims=True).astype(I32)
    row = lax.broadcasted_iota(I32, (LOCAL_ROWS, MOE_TILE), 0)
    onehot = jnp.where(row == p_lo, 1.0, jnp.where(row == p_hi, 1.0, 0.0)).astype(BF16)
    local_ref[...] = _dot(onehot, x_ref[...].astype(BF16)).astype(BF16)
    total = _window_copies(cnt_sm, dest_sm, tile, offs, local_ref, xs_hbm, sem, True)
    _wait_windows(total, local_ref, xs_hbm, sem)


def _dispatch(x1, gates_t, cnt_flat, dest_flat, fill):
    s = x1.shape[0]
    return pl.pallas_call(
        _dispatch_kernel,
        grid_spec=pltpu.PrefetchScalarGridSpec(
            num_scalar_prefetch=3,
            grid=(s // MOE_TILE,),
            in_specs=[pl.BlockSpec((MOE_TILE, D_MODEL), lambda i, *_: (i, 0)),
                      pl.BlockSpec((N_EXPERTS, MOE_TILE), lambda i, *_: (0, i))],
            out_specs=pl.BlockSpec(memory_space=pl.ANY),
            scratch_shapes=[pltpu.VMEM((LOCAL_ROWS, D_MODEL), BF16), pltpu.VMEM((ROW_TILE, D_MODEL), BF16),
                            pltpu.SemaphoreType.DMA(()), pltpu.SemaphoreType.DMA(()),
                            pltpu.SemaphoreType.DMA(())]),
        out_shape=jax.ShapeDtypeStruct((_sorted_rows(s), D_MODEL), BF16),
        compiler_params=_params("arbitrary"),
        name="moe_dispatch",
    )(cnt_flat, dest_flat, fill, x1, gates_t)


def _expert_kernel(te_sm, tv_sm, xs_ref, wg_ref, wu_ref, wd_ref, y_ref, wg_b, wu_b, wd_b):
    k = pl.program_id(0)
    valid = tv_sm[k]
    new_expert = jnp.logical_or(k == 0, te_sm[k] != te_sm[jnp.maximum(k - 1, 0)])

    @pl.when(jnp.logical_and(valid > 0, new_expert))
    def _():
        wg_b[...] = wg_ref[0, 0].astype(BF16)
        wu_b[...] = wu_ref[0, 0].astype(BF16)
        wd_b[...] = wd_ref[0, 0].astype(BF16)

    @pl.when(valid > 0)
    def _():
        x = xs_ref[...]
        hg = _dot(x, wg_b[...])
        hu = _dot(x, wu_b[...])
        act = hg * jax.nn.sigmoid(hg) * hu
        y_ref[...] = _dot(act.astype(BF16), wd_b[...]).astype(y_ref.dtype)

    @pl.when(valid == 0)
    def _():
        y_ref[...] = jnp.zeros_like(y_ref)


def _experts(xs, tile_expert, tile_valid, w_gate, w_up, w_down, layer):
    rows = xs.shape[0]
    w_in_spec = pl.BlockSpec((1, 1, D_MODEL, D_FF_EXPERT), lambda k, te, tv: (layer, te[k], 0, 0))
    return pl.pallas_call(
        _expert_kernel,
        grid_spec=pltpu.PrefetchScalarGridSpec(
            num_scalar_prefetch=2,
            grid=(rows // ROW_TILE,),
            in_specs=[pl.BlockSpec((ROW_TILE, D_MODEL), lambda k, te, tv: (k, 0)),
                      w_in_spec, w_in_spec,
                      pl.BlockSpec((1, 1, D_FF_EXPERT, D_MODEL), lambda k, te, tv: (layer, te[k], 0, 0))],
            out_specs=pl.BlockSpec((ROW_TILE, D_MODEL), lambda k, te, tv: (k, 0)),
            scratch_shapes=[pltpu.VMEM((D_MODEL, D_FF_EXPERT), BF16), pltpu.VMEM((D_MODEL, D_FF_EXPERT), BF16),
                            pltpu.VMEM((D_FF_EXPERT, D_MODEL), BF16)]),
        out_shape=jax.ShapeDtypeStruct((rows, D_MODEL), BF16),
        compiler_params=_params("arbitrary"),
        name="moe_experts",
    )(tile_expert, tile_valid, xs, w_gate, w_up, w_down)


def _combine_kernel(cnt_sm, dest_sm, x_ref, gates_ref, y_hbm, g_ref, b_ref, o_ref, local_ref, sem):
    tile = pl.program_id(0)

    @pl.when(tile == 0)
    def _():
        local_ref[...] = jnp.zeros_like(local_ref)

    offs = _slot_offsets(cnt_sm, tile)
    total = _window_copies(cnt_sm, dest_sm, tile, offs, local_ref, y_hbm, sem, False)

    gates = gates_ref[...]
    chosen = gates > 0.0
    t_r = lax.broadcasted_iota(I32, (MOE_TILE, MOE_TILE), 0)
    t_c = lax.broadcasted_iota(I32, (MOE_TILE, MOE_TILE), 1)
    earlier = _dot((t_c < t_r).astype(BF16), jnp.where(chosen, 1.0, 0.0).astype(BF16))
    e_id = lax.broadcasted_iota(I32, (1, BLK), 1)
    slot = jnp.zeros((1, BLK), I32)
    for e in range(N_EXPERTS):
        slot = jnp.where(e_id == e, offs[e], slot)
    pos = jnp.where(chosen, earlier + slot.astype(F32), -1.0)
    p_lo = jnp.min(jnp.where(chosen, pos, float(LOCAL_ROWS)), axis=1, keepdims=True)
    p_hi = jnp.max(pos, axis=1, keepdims=True)
    w_lo = jnp.sum(jnp.where(pos == p_lo, gates, 0.0), axis=1, keepdims=True)
    w_hi = jnp.sum(jnp.where(pos == p_hi, gates, 0.0), axis=1, keepdims=True)
    col = lax.broadcasted_iota(I32, (MOE_TILE, LOCAL_ROWS), 1)
    weights = jnp.where(col == p_lo.astype(I32), w_lo, jnp.where(col == p_hi.astype(I32), w_hi, 0.0))

    _wait_windows(total, local_ref, y_hbm, sem)
    ffn = _dot(weights.astype(BF16), local_ref[...])
    o_ref[...] = _layer_norm(DEEPNORM_ALPHA * x_ref[...] + ffn, g_ref[...], b_ref[...])


def _combine(x1, gates, y, cnt_flat, dest_flat, g, b):
    s = x1.shape[0]
    row = lambda n: pl.BlockSpec((MOE_TILE, n), lambda i, *_: (i, 0))
    vec = pl.BlockSpec((1, D_MODEL), lambda i, *_: (0, 0))
    return pl.pallas_call(
        _combine_kernel,
        grid_spec=pltpu.PrefetchScalarGridSpec(
            num_scalar_prefetch=2,
            grid=(s // MOE_TILE,),
            in_specs=[row(D_MODEL), row(BLK), pl.BlockSpec(memory_space=pl.ANY), vec, vec],
            out_specs=row(D_MODEL),
            scratch_shapes=[pltpu.VMEM((LOCAL_ROWS, D_MODEL), BF16), pltpu.SemaphoreType.DMA(())]),
        out_shape=jax.ShapeDtypeStruct((s, D_MODEL), F32),
        compiler_params=_params("arbitrary"),
        name="moe_combine",
    )(cnt_flat, dest_flat, x1, gates, y, g, b)


def _moe(x1, gates, gates_t, cnt, w_gate, w_up, w_down, g, b, layer):
    s = x1.shape[0]
    cnt_flat, dest_flat, fill, tile_expert, tile_valid = _moe_plan(cnt[:, :, 0], s)
    xs = _dispatch(x1, gates_t, cnt_flat, dest_flat, fill)
    y = _experts(xs, tile_expert, tile_valid, w_gate, w_up, w_down, layer)
    return _combine(x1, gates, y, cnt_flat, dest_flat, g, b)


def _reorder_w_in(w):
    sb = w[:, 0:3 * D_SB]
    o = 3 * D_SB
    ret = w[:, o:o + 4 * D_RET]
    o += 4 * D_RET
    d_q = w[:, o:o + D_DSA]
    o += D_DSA
    d_ckv = w[:, o:o + KV_RANK]
    o += KV_RANK
    d_qi = w[:, o:o + IDX_HEADS * IDX_DIM]
    o += IDX_HEADS * IDX_DIM
    tail = w[:, o:]
    parts = [sb, d_q, ret, d_qi, d_ckv, tail]
    width = sum(a.shape[1] for a in parts)
    parts.append(jnp.zeros((w.shape[0], D_PROJ - width), w.dtype))
    return jnp.concatenate(parts, axis=1).astype(BF16)


def kernel(x, w_in, w_kv_up, kv_norm_g, ret_gn_g, w_o, ln1_g, ln1_b, w_router, w_gate, w_up, w_down,
           ln2_g, ln2_b):
    b, s, _ = x.shape
    assert b == 1 and s % KEY_CHUNK == 0
    h = x[0]
    tables = _retention_tables(s)
    wr_t = w_router.T
    for l in range(DEPTH):
        p = _proj(h, _reorder_w_in(w_in[l]))
        y_sb = _stick_breaking(p)
        y_ret = _retention(p, ret_gn_g[l][None, :], tables)
        w_kv = w_kv_up[l].reshape(KV_RANK, DSA_HEADS, 2, HEAD_DIM)
        w_kv = jnp.concatenate([w_kv[:, :, 0, :].reshape(KV_RANK, D_DSA),
                                w_kv[:, :, 1, :].reshape(KV_RANK, D_DSA)], axis=1).astype(BF16)
        k_dsa, vt_dsa = _kv_up(p, kv_norm_g[l][None, :], w_kv)
        y_dsa = _dsa(p, k_dsa, vt_dsa)
        wo = w_o[l].astype(BF16)
        x1, gates, gates_t, cnt = _out_proj(y_sb, y_ret, y_dsa, wo[0:D_SB], wo[D_SB:D_SB + D_RET],
                                            wo[D_SB + D_RET:], h, ln1_g[l][None, :], ln1_b[l][None, :], wr_t)
        h = _moe(x1, gates, gates_t, cnt, w_gate, w_up, w_down, ln2_g[l][None, :], ln2_b[l][None, :], l)
    return h[None]
```

```python
import functools

import numpy as np
import jax
import jax.numpy as jnp
from jax import lax
from jax.experimental import pallas as pl
from jax.experimental.pallas import tpu as pltpu

F32 = jnp.float32
BF16 = jnp.bfloat16
I32 = jnp.int32

D_MODEL = 2048
HEAD_DIM = 128
SB_HEADS = 6
RET_HEADS = 4
DSA_HEADS = 6
D_SB = SB_HEADS * HEAD_DIM
D_RET = RET_HEADS * HEAD_DIM
D_DSA = DSA_HEADS * HEAD_DIM
KV_RANK = 256
IDX_HEADS = 8
IDX_DIM = 64
IDX_SCALE = IDX_DIM ** -0.5 * IDX_HEADS ** -0.5
TOPK_MAX = 256
BLK = 128
N_EXPERTS = 16
N_GROUPS = 4
EXPERTS_PER_GROUP = N_EXPERTS // N_GROUPS
D_FF_EXPERT = 512
LN_EPS = 1e-5
RMS_EPS = 1e-6
GN_EPS = 1e-6
DEPTH = 2
DEEPNORM_ALPHA = (2 * DEPTH) ** 0.25

OFF_SB_Q = 0
OFF_SB_K = OFF_SB_Q + D_SB
OFF_SB_V = OFF_SB_K + D_SB
OFF_D_Q = OFF_SB_V + D_SB
OFF_R_Q = OFF_D_Q + D_DSA
OFF_R_K = OFF_R_Q + D_RET
OFF_R_V = OFF_R_K + D_RET
OFF_R_G = OFF_R_V + D_RET
OFF_D_QI = OFF_R_G + D_RET
OFF_D_CKV = OFF_D_QI + IDX_HEADS * IDX_DIM
OFF_TAIL = OFF_D_CKV + KV_RANK
D_PROJ = 6144

KEY_CHUNK = 512
MOE_TILE = 512
WIN = 16
ROW_TILE = 512
VMEM_LIMIT = 56 * 1024 * 1024
LOG2_E = 1.4426950408889634
NEG_BIG = -1e30
EXP_UNDERFLOW = -87.4
KEY_OF_NEG_INF = -2139095041
INT_MIN = -2147483648


def _dot(a, b):
    return jnp.dot(a, b, preferred_element_type=F32)


def _dot_nt(a, b):
    return lax.dot_general(a, b, (((1,), (1,)), ((), ())), preferred_element_type=F32)


def _dot_tn(a, b):
    return lax.dot_general(a, b, (((0,), (0,)), ((), ())), preferred_element_type=F32)


def _params(*sem):
    return pltpu.CompilerParams(dimension_semantics=sem, vmem_limit_bytes=VMEM_LIMIT)


def _resident(shape, index_map):
    return pl.BlockSpec(shape, index_map, pipeline_mode=pl.Buffered(1))


def _proj_kernel(x_ref, w_ref, o_ref):
    o_ref[...] = _dot(x_ref[...].astype(BF16), w_ref[...]).astype(o_ref.dtype)


def _proj(x, w):
    s, d = x.shape
    n = w.shape[1]
    tm = min(1024, s)
    tn = 1536
    return pl.pallas_call(
        _proj_kernel,
        grid=(s // tm, n // tn),
        in_specs=[pl.BlockSpec((tm, d), lambda i, j: (i, 0)),
                  pl.BlockSpec((d, tn), lambda i, j: (0, j))],
        out_specs=pl.BlockSpec((tm, tn), lambda i, j: (i, j)),
        out_shape=jax.ShapeDtypeStruct((s, n), BF16),
        compiler_params=_params("parallel", "arbitrary"),
        name="proj",
    )(x, w)


def _sb_kernel(q_ref, k_ref, v_ref, o_ref, acc_ref):
    i = pl.program_id(0)
    q = q_ref[...]
    scale = HEAD_DIM ** -0.5
    key_pos = lax.broadcasted_iota(I32, (BLK, BLK), 0)
    qry_pos = lax.broadcasted_iota(I32, (BLK, BLK), 1)
    later = (qry_pos > key_pos).astype(BF16)
    acc_ref[...] = jnp.zeros_like(acc_ref)

    def cond(carry):
        j, cs = carry
        c_max = functools.reduce(jnp.maximum, cs)
        return jnp.logical_and(j >= 0, jnp.max(c_max) > EXP_UNDERFLOW)

    def body(carry):
        j, cs = carry
        off = pl.multiple_of(j * BLK, BLK)
        strict = (off + key_pos) < (i * BLK + qry_pos)
        new_cs = []
        for h in range(SB_HEADS):
            hs = slice(h * HEAD_DIM, (h + 1) * HEAD_DIM)
            z = _dot_nt(k_ref[pl.ds(off, BLK), hs], q[:, hs]) * scale
            sp = jnp.maximum(z, 0.0) + jnp.log1p(jnp.exp(-jnp.abs(z)))
            log_rem = jnp.where(strict, -sp, 0.0)
            hi = log_rem.astype(BF16)
            lo = (log_rem - hi.astype(F32)).astype(BF16)
            after = _dot(later, hi) + _dot(later, lo)
            a = jnp.where(strict, jnp.exp(z - sp + after + cs[h]), 0.0)
            acc_ref[h] += _dot_tn(a.astype(BF16), v_ref[pl.ds(off, BLK), hs])
            new_cs.append(cs[h] + jnp.sum(log_rem, axis=0, keepdims=True))
        return j - 1, tuple(new_cs)

    lax.while_loop(cond, body, (i, tuple(jnp.zeros((1, BLK), F32) for _ in range(SB_HEADS))))
    for h in range(SB_HEADS):
        o_ref[:, h * HEAD_DIM:(h + 1) * HEAD_DIM] = acc_ref[h].astype(o_ref.dtype)


def _stick_breaking(p):
    s = p.shape[0]
    nb = s // BLK
    return pl.pallas_call(
        _sb_kernel,
        grid=(nb,),
        in_specs=[pl.BlockSpec((BLK, D_SB), lambda i: (i, OFF_SB_Q // D_SB)),
                  _resident((s, D_SB), lambda i: (0, OFF_SB_K // D_SB)),
                  _resident((s, D_SB), lambda i: (0, OFF_SB_V // D_SB))],
        out_specs=pl.BlockSpec((BLK, D_SB), lambda i: (i, 0)),
        out_shape=jax.ShapeDtypeStruct((s, D_SB), BF16),
        scratch_shapes=[pltpu.VMEM((SB_HEADS, BLK, HEAD_DIM), F32)],
        compiler_params=_params("arbitrary"),
        name="stick_breaking",
    )(p, p, p)


def _ret_kernel(q_ref, k_ref, v_ref, g_ref, cos_ref, sin_ref, intra_ref, qd_ref, kd_ref, cd_ref,
                gn_ref, o_ref, state_ref):
    n = pl.program_id(0)

    @pl.when(n == 0)
    def _():
        state_ref[...] = jnp.zeros_like(state_ref)

    cos = cos_ref[...]
    sin = sin_ref[...]
    for h in range(RET_HEADS):
        hs = slice(h * HEAD_DIM, (h + 1) * HEAD_DIM)
        q = q_ref[:, hs].astype(F32)
        k = k_ref[:, hs].astype(F32)
        v = v_ref[:, hs]
        qr = q * cos + pltpu.roll(q, HEAD_DIM // 2, 1) * sin
        kr = (k * cos + pltpu.roll(k, HEAD_DIM // 2, 1) * sin) * (HEAD_DIM ** -0.5)
        scores = _dot_nt(qr.astype(BF16), kr.astype(BF16)) * intra_ref[h]
        state = state_ref[h]
        o = (_dot(scores.astype(BF16), v)
             + _dot((qr * qd_ref[h]).astype(BF16), state.astype(BF16)))
        state_ref[h] = cd_ref[h] * state + _dot_tn((kr * kd_ref[h]).astype(BF16), v)
        mu = jnp.mean(o, axis=1, keepdims=True)
        var = jnp.mean(jnp.square(o - mu), axis=1, keepdims=True)
        on = (o - mu) * lax.rsqrt(var + GN_EPS) * gn_ref[:, hs]
        g = g_ref[:, hs].astype(F32)
        o_ref[:, hs] = (g * jax.nn.sigmoid(g) * on).astype(o_ref.dtype)


def _retention_tables(s):
    f32 = np.float32
    half = HEAD_DIM // 2
    pos = np.arange(s, dtype=f32)
    theta = (f32(10000.0) ** (-np.linspace(0.0, 1.0, half, dtype=f32))).astype(f32)
    ang = (pos[:, None] * theta[None, :]).astype(f32)
    cos, sin = np.cos(ang).astype(f32), np.sin(ang).astype(f32)
    cos2 = np.concatenate([cos, cos], axis=1)
    sin2 = np.concatenate([-sin, sin], axis=1)
    log_gamma = np.log1p(-(f32(2.0) ** (-5.0 - np.arange(RET_HEADS, dtype=f32)))).astype(f32)
    idx = np.arange(BLK, dtype=f32)
    diff = idx[:, None] - idx[None, :]
    intra = np.where(diff >= 0, np.exp(np.maximum(diff, 0.0)[None] * log_gamma[:, None, None]), 0.0).astype(f32)
    q_decay = np.exp((idx[None, :] + 1.0) * log_gamma[:, None]).astype(f32)
    k_decay = np.exp((BLK - 1.0 - idx[None, :]) * log_gamma[:, None]).astype(f32)
    chunk_decay = np.exp(BLK * log_gamma).astype(f32)
    full = (RET_HEADS, BLK, HEAD_DIM)
    return tuple(jnp.asarray(np.ascontiguousarray(t)) for t in (
        cos2, sin2, intra,
        np.broadcast_to(q_decay[:, :, None], full),
        np.broadcast_to(k_decay[:, :, None], full),
        np.broadcast_to(chunk_decay[:, None, None], full)))


def _retention(p, gn_g, tables):
    s = p.shape[0]
    nc = s // BLK
    cos2, sin2, intra, qd, kd, cd = tables
    col = lambda off: pl.BlockSpec((BLK, D_RET), lambda n: (n, off // D_RET))
    per_head = pl.BlockSpec((RET_HEADS, BLK, HEAD_DIM), lambda n: (0, 0, 0))
    pos_spec = pl.BlockSpec((BLK, HEAD_DIM), lambda n: (n, 0))
    return pl.pallas_call(
        _ret_kernel,
        grid=(nc,),
        in_specs=[col(OFF_R_Q), col(OFF_R_K), col(OFF_R_V), col(OFF_R_G),
                  pos_spec, pos_spec, per_head, per_head, per_head, per_head,
                  pl.BlockSpec((1, D_RET), lambda n: (0, 0))],
        out_specs=pl.BlockSpec((BLK, D_RET), lambda n: (n, 0)),
        out_shape=jax.ShapeDtypeStruct((s, D_RET), BF16),
        scratch_shapes=[pltpu.VMEM((RET_HEADS, HEAD_DIM, HEAD_DIM), F32)],
        compiler_params=_params("arbitrary"),
        name="retention",
    )(p, p, p, p, cos2, sin2, intra, qd, kd, cd, gn_g)


def _kv_up_kernel(c_ref, g_ref, w_ref, k_ref, vt_ref):
    c = c_ref[...].astype(F32)
    y = c * lax.rsqrt(jnp.mean(jnp.square(c), axis=1, keepdims=True) + RMS_EPS) * g_ref[...]
    kv = _dot(y.astype(BF16), w_ref[...])
    k_ref[...] = (kv[:, :D_DSA] * (HEAD_DIM ** -0.5 * LOG2_E)).astype(k_ref.dtype)
    vt_ref[0] = kv[:, D_DSA:].T.astype(vt_ref.dtype)


def _kv_up(p, g, w):
    s = p.shape[0]
    n = w.shape[1]
    return pl.pallas_call(
        _kv_up_kernel,
        grid=(s // KEY_CHUNK,),
        in_specs=[pl.BlockSpec((KEY_CHUNK, KV_RANK), lambda i: (i, OFF_D_CKV // KV_RANK)),
                  pl.BlockSpec((1, KV_RANK), lambda i: (0, 0)),
                  pl.BlockSpec((KV_RANK, n), lambda i: (0, 0))],
        out_specs=[pl.BlockSpec((KEY_CHUNK, D_DSA), lambda i: (i, 0)),
                   pl.BlockSpec((1, D_DSA, KEY_CHUNK), lambda i: (i, 0, 0))],
        out_shape=[jax.ShapeDtypeStruct((s, D_DSA), BF16),
                   jax.ShapeDtypeStruct((s // KEY_CHUNK, D_DSA, KEY_CHUNK), BF16)],
        compiler_params=_params("parallel"),
        name="kv_up",
    )(p, g, w)


def _ordered_bits_to_float(u):
    return pltpu.bitcast(u ^ ((u >> 31) & 0x7FFFFFFF), F32)


GROUPS_PER_CHUNK = KEY_CHUNK // (32 * 8)


def _bit_transpose32(words):
    a = list(words)
    j, mask = 16, 0x0000FFFF
    while j:
        k = 0
        while k < 32:
            t = (a[k] ^ (a[k + j] >> j)) & mask
            a[k] = a[k] ^ t
            a[k + j] = a[k + j] ^ (t << j)
            k = (k + j + 1) & ~j
        j >>= 1
        mask = (mask ^ (mask << j)) & 0xFFFFFFFF
    return a


def _dsa_kernel(q_ref, qi_ref, tq_ref, tail_ref, k_ref, vt_ref, o_ref,
                score_ref, planes_ref, alive_ref, bias_ref, acc_ref, lg_a, lg_b, *, topk):
    i = pl.program_id(0)
    n_chunks = ((i + 1) * BLK + KEY_CHUNK - 1) // KEY_CHUNK
    n_kc = score_ref.shape[0]

    w_t = tq_ref[...].astype(F32).T[IDX_DIM:IDX_DIM + IDX_HEADS, :] * IDX_SCALE
    qi = qi_ref[...]
    qi_rows = jnp.concatenate([qi[:, h * IDX_DIM:(h + 1) * IDX_DIM] for h in range(IDX_HEADS)], axis=0)

    def chunk_scores(c):
        off = pl.multiple_of(c * KEY_CHUNK, KEY_CHUNK)
        rel = jnp.maximum(_dot_nt(tail_ref[pl.ds(off, KEY_CHUNK), 0:IDX_DIM], qi_rows), 0.0)
        score = rel[:, 0:BLK] * w_t[0:1, :]
        for h in range(1, IDX_HEADS):
            score = score + rel[:, h * BLK:(h + 1) * BLK] * w_t[h:h + 1, :]
        return score

    def store_chunk(c, score):
        score = jnp.where(score == 0.0, 0.0, score)
        score_ref[c] = score
        bits = pltpu.bitcast(score, I32)
        u = bits ^ ((bits >> 31) | INT_MIN)
        for g in range(GROUPS_PER_CHUNK):
            words = [u[(g * 32 + j) * 8:(g * 32 + j + 1) * 8, :] for j in range(32)]
            planes = _bit_transpose32(words)
            for b in range(32):
                planes_ref[c * GROUPS_PER_CHUNK + g, b] = planes[b]

    def score_chunk(c, _):
        store_chunk(c, chunk_scores(c))
        return 0

    lax.fori_loop(0, n_chunks - 1, score_chunk, 0)
    last = n_chunks - 1
    key_pos = last * KEY_CHUNK + lax.broadcasted_iota(I32, (KEY_CHUNK, BLK), 0)
    t_col = i * BLK + lax.broadcasted_iota(I32, (KEY_CHUNK, BLK), 1)
    store_chunk(last, jnp.where(key_pos <= t_col, chunk_scores(last), -jnp.inf))

    def init_alive(c, _):
        for g in range(GROUPS_PER_CHUNK):
            alive_ref[c * GROUPS_PER_CHUNK + g] = jnp.full((8, BLK), -1, I32)
        return 0

    lax.fori_loop(0, n_chunks, init_alive, 0)

    @pl.when(n_chunks % 2 == 1)
    def _():
        for g in range(GROUPS_PER_CHUNK):
            alive_ref[n_chunks * GROUPS_PER_CHUNK + g] = jnp.zeros((8, BLK), I32)
            planes_ref[n_chunks * GROUPS_PER_CHUNK + g] = jnp.zeros((32, 8, BLK), I32)

    def decide(above, t_bits, ones, bit):
        take = (above + ones) >= topk
        return (jnp.where(take, above, above + ones), jnp.where(take, t_bits | bit, t_bits),
                jnp.where(take, 0, -1))

    groups_per_step = 2 * GROUPS_PER_CHUNK

    def sweep(plane_prev, plane, drop_prev):
        def step(p, acc):
            for g in range(groups_per_step):
                gi = p * groups_per_step + g
                alive = alive_ref[gi]
                if plane_prev is not None:
                    alive = alive & (planes_ref[gi, plane_prev] ^ drop_prev)
                    alive_ref[gi] = alive
                if plane is not None:
                    alive = alive & planes_ref[gi, plane]
                acc = acc + lax.population_count(alive)
            return acc

        acc = lax.fori_loop(0, (n_chunks + 1) // 2, step, jnp.zeros((8, BLK), I32))
        return jnp.sum(acc, axis=0, keepdims=True)

    zero = jnp.zeros((1, BLK), I32)
    state = decide(zero, zero, sweep(None, 0, None), INT_MIN)

    def bit_step(b, state):
        above, t_bits, drop = state
        return decide(above, t_bits, sweep(b - 1, b, drop), jnp.int32(1) << (31 - b))

    above, t_bits, drop = lax.fori_loop(1, 32, bit_step, state)
    n_eq = sweep(31, None, drop)
    need = topk - above
    thr = t_bits ^ INT_MIN
    real = thr > KEY_OF_NEG_INF
    thr_f = jnp.where(real, _ordered_bits_to_float(thr), jnp.finfo(F32).min)

    def bias_chunk(c, _):
        bias_ref[c] = jnp.where(score_ref[c] >= thr_f, 0.0, NEG_BIG)
        return 0

    lax.fori_loop(0, n_chunks, bias_chunk, 0)

    has_tie = jnp.max(jnp.where(jnp.logical_and(real, n_eq > need), 1, 0)) > 0

    @pl.when(has_tie)
    def _():
        need_f = need.astype(F32)
        r = lax.broadcasted_iota(I32, (BLK, BLK), 0)
        cc = lax.broadcasted_iota(I32, (BLK, BLK), 1)
        upto = (cc <= r).astype(BF16)

        def tie_chunk(c, run):
            sc = score_ref[c]
            for u in range(KEY_CHUNK // BLK):
                st = sc[u * BLK:(u + 1) * BLK, :]
                eq = jnp.logical_and(st == thr_f, real)
                eqf = jnp.where(eq, 1.0, 0.0)
                rank = _dot(upto, eqf.astype(BF16)) + run
                sel = jnp.logical_or(st > thr_f, jnp.logical_and(eq, rank <= need_f))
                bias_ref[c, u * BLK:(u + 1) * BLK, :] = jnp.where(
                    real, jnp.where(sel, 0.0, NEG_BIG), jnp.where(st >= thr_f, 0.0, NEG_BIG))
                run = run + jnp.sum(eqf, axis=0, keepdims=True)
            return run

        lax.fori_loop(0, n_chunks, tie_chunk, jnp.zeros((1, BLK), F32))

    acc_ref[...] = jnp.zeros_like(acc_ref)
    bias_ref[n_kc] = jnp.full((KEY_CHUNK, BLK), NEG_BIG, F32)
    q = q_ref[...]

    def logits_into(lg, c):
        kc = jnp.minimum(c, n_chunks - 1)
        off = pl.multiple_of(kc * KEY_CHUNK, KEY_CHUNK)
        bias = bias_ref[jnp.where(c < n_chunks, c, n_kc)]
        for h in range(DSA_HEADS):
            hs = slice(h * HEAD_DIM, (h + 1) * HEAD_DIM)
            lg[h] = _dot_nt(k_ref[pl.ds(off, KEY_CHUNK), hs], q[:, hs]) + bias

    def reduce_from(lg, c, ms, ls):
        vc = jnp.minimum(c, n_chunks - 1)
        new_m, new_l = [], []
        for h in range(DSA_HEADS):
            hs = slice(h * HEAD_DIM, (h + 1) * HEAD_DIM)
            logits = lg[h]
            m_new = jnp.maximum(ms[h], jnp.max(logits, axis=0, keepdims=True))
            alpha = jnp.exp2(ms[h] - m_new)
            pr = jnp.exp2(logits - m_new)
            new_l.append(alpha * ls[h] + jnp.sum(pr, axis=0, keepdims=True))
            acc_ref[h] = alpha * acc_ref[h] + _dot(vt_ref[vc, hs, :], pr.astype(BF16))
            new_m.append(m_new)
        return tuple(new_m), tuple(new_l)

    def attn_pair(pair, carry):
        ms, ls = carry
        c = 2 * pair
        logits_into(lg_b, c + 1)
        ms, ls = reduce_from(lg_a, c, ms, ls)
        logits_into(lg_a, c + 2)
        return reduce_from(lg_b, c + 1, ms, ls)

    logits_into(lg_a, 0)
    m0 = tuple(jnp.full((1, BLK), NEG_BIG, F32) for _ in range(DSA_HEADS))
    l0 = tuple(jnp.zeros((1, BLK), F32) for _ in range(DSA_HEADS))
    _, ls = lax.fori_loop(0, (n_chunks + 1) // 2, attn_pair, (m0, l0))
    for h in range(DSA_HEADS):
        o_ref[:, h * HEAD_DIM:(h + 1) * HEAD_DIM] = (acc_ref[h] / ls[h]).T.astype(o_ref.dtype)


def _dsa(p, k, vt):
    s = p.shape[0]
    nb = s // BLK
    topk = min(TOPK_MAX, s // 4)
    n_kc = s // KEY_CHUNK
    return pl.pallas_call(
        functools.partial(_dsa_kernel, topk=topk),
        grid=(nb,),
        in_specs=[pl.BlockSpec((BLK, D_DSA), lambda i: (i, OFF_D_Q // D_DSA)),
                  pl.BlockSpec((BLK, IDX_HEADS * IDX_DIM), lambda i: (i, OFF_D_QI // (IDX_HEADS * IDX_DIM))),
                  pl.BlockSpec((BLK, BLK), lambda i: (i, OFF_TAIL // BLK)),
                  _resident((s, BLK), lambda i: (0, OFF_TAIL // BLK)),
                  _resident((s, D_DSA), lambda i: (0, 0)),
                  _resident((n_kc, D_DSA, KEY_CHUNK), lambda i: (0, 0, 0))],
        out_specs=pl.BlockSpec((BLK, D_DSA), lambda i: (i, 0)),
        out_shape=jax.ShapeDtypeStruct((s, D_DSA), BF16),
        scratch_shapes=[pltpu.VMEM((n_kc, KEY_CHUNK, BLK), F32),
                        pltpu.VMEM((n_kc * GROUPS_PER_CHUNK, 32, 8, BLK), I32),
                        pltpu.VMEM((n_kc * GROUPS_PER_CHUNK, 8, BLK), I32),
                        pltpu.VMEM((n_kc + 1, KEY_CHUNK, BLK), F32),
                        pltpu.VMEM((DSA_HEADS, HEAD_DIM, BLK), F32),
                        pltpu.VMEM((DSA_HEADS, KEY_CHUNK, BLK), F32),
                        pltpu.VMEM((DSA_HEADS, KEY_CHUNK, BLK), F32)],
        compiler_params=_params("arbitrary"),
        name="dsa",
    )(p, p, p, p, k, vt)


def _layer_norm(r, g, b):
    mu = jnp.mean(r, axis=1, keepdims=True)
    var = jnp.mean(jnp.square(r - mu), axis=1, keepdims=True)
    return (r - mu) * lax.rsqrt(var + LN_EPS) * g + b


def _first_max_of4(vals):
    a, b, c, d = vals
    m = jnp.maximum(jnp.maximum(a, b), jnp.maximum(c, d))
    idx = jnp.where(a == m, 0, jnp.where(b == m, 1, jnp.where(c == m, 2, 3)))
    return m, idx


def _router_gates(logits_t):
    mx = jnp.max(logits_t, axis=0, keepdims=True)
    e = jnp.exp(logits_t - mx)
    probs = e / jnp.sum(e, axis=0, keepdims=True)
    rows = [probs[j:j + 1, :] for j in range(N_EXPERTS)]
    m1s, m2s, i1s, i2s, scores = [], [], [], [], []
    for g in range(N_GROUPS):
        vals = rows[g * EXPERTS_PER_GROUP:(g + 1) * EXPERTS_PER_GROUP]
        m1, i1 = _first_max_of4(vals)
        rest = [jnp.where(i1 == j, -1.0, vals[j]) for j in range(EXPERTS_PER_GROUP)]
        m2, i2 = _first_max_of4(rest)
        m1s.append(m1); m2s.append(m2); i1s.append(i1); i2s.append(i2); scores.append(m1 + m2)
    best, g_sel = _first_max_of4(scores)
    pick = lambda xs: jnp.where(g_sel == 0, xs[0], jnp.where(g_sel == 1, xs[1], jnp.where(g_sel == 2, xs[2], xs[3])))
    m1, m2, i1, i2 = pick(m1s), pick(m2s), pick(i1s), pick(i2s)
    den = m1 + m2
    w1, w2 = m1 / den, m2 / den
    e1 = g_sel * EXPERTS_PER_GROUP + i1
    e2 = g_sel * EXPERTS_PER_GROUP + i2
    gates = [jnp.where(e1 == j, w1, 0.0) + jnp.where(e2 == j, w2, 0.0) for j in range(N_EXPERTS)]
    return jnp.concatenate(gates, axis=0)


def _out_kernel(ysb_ref, yret_ref, ydsa_ref, wo_ref, x_ref, g_ref, b_ref, wr_ref,
                x1_ref, gates_ref, gates_t_ref, cnt_ref):
    mix = (_dot(ysb_ref[...], wo_ref[0:D_SB, :]) + _dot(yret_ref[...], wo_ref[D_SB:D_SB + D_RET, :])
           + _dot(ydsa_ref[...], wo_ref[D_SB + D_RET:, :]))
    x1 = _layer_norm(DEEPNORM_ALPHA * x_ref[...] + mix, g_ref[...], b_ref[...])
    x1_ref[...] = x1
    logits_t = lax.dot_general(wr_ref[...], x1, (((1,), (1,)), ((), ())),
                               preferred_element_type=F32, precision=lax.Precision.HIGHEST)
    gates_t = _router_gates(logits_t)
    tm = x1.shape[0]
    gates_t_ref[...] = gates_t
    padded = jnp.concatenate([gates_t, jnp.zeros((BLK - N_EXPERTS, tm), F32)], axis=0)
    gates_ref[...] = padded.T
    chosen = jnp.sum(jnp.where(gates_t > 0.0, 1, 0), axis=1, keepdims=True)
    cnt_ref[0] = jnp.broadcast_to(chosen, (N_EXPERTS, BLK))


def _out_proj(ysb, yret, ydsa, wo, x, g, b, wr_t):
    s = x.shape[0]
    tm = MOE_TILE
    row = lambda n: pl.BlockSpec((tm, n), lambda i: (i, 0))
    whole = lambda a: _resident(a.shape, lambda i: (0, 0))
    return pl.pallas_call(
        _out_kernel,
        grid=(s // tm,),
        in_specs=[row(D_SB), row(D_RET), row(D_DSA), whole(wo),
                  row(D_MODEL), whole(g), whole(b), whole(wr_t)],
        out_specs=[row(D_MODEL), row(BLK), pl.BlockSpec((N_EXPERTS, tm), lambda i: (0, i)),
                   pl.BlockSpec((1, N_EXPERTS, BLK), lambda i: (i, 0, 0))],
        out_shape=[jax.ShapeDtypeStruct((s, D_MODEL), F32), jax.ShapeDtypeStruct((s, BLK), F32),
                   jax.ShapeDtypeStruct((N_EXPERTS, s), F32),
                   jax.ShapeDtypeStruct((s // tm, N_EXPERTS, BLK), I32)],
        compiler_params=_params("parallel"),
        name="out_proj",
    )(ysb, yret, ydsa, wo, x, g, b, wr_t)


LOCAL_ROWS = 2 * MOE_TILE + N_EXPERTS * WIN


def _round_up(x, m):
    return (x + m - 1) // m * m


def _sorted_rows(s):
    n_tiles = s // MOE_TILE
    return _round_up(2 * s + n_tiles * N_EXPERTS * (WIN - 1) + N_EXPERTS * (ROW_TILE - 1), ROW_TILE)


def _moe_plan(cnt, s):
    n_row_tiles = _sorted_rows(s) // ROW_TILE
    seg = _round_up(cnt, WIN)
    rows_e = jnp.sum(seg, axis=0)
    region = _round_up(rows_e, ROW_TILE)
    region_off = jnp.cumsum(region) - region
    dest = region_off[None, :] + jnp.cumsum(seg, axis=0) - seg
    tiles_e = region // ROW_TILE
    tile_end = jnp.cumsum(tiles_e)
    k = jnp.arange(n_row_tiles, dtype=I32)
    tile_expert = jnp.minimum(jnp.sum((k[:, None] >= tile_end[None, :]).astype(I32), axis=1), N_EXPERTS - 1)
    first = (tile_end - tiles_e)[tile_expert]
    valid = jnp.clip(rows_e[tile_expert] - (k - first) * ROW_TILE, 0, ROW_TILE)
    tile_valid = jnp.where(k < tile_end[-1], valid, 0).astype(I32)
    fill = jnp.concatenate([region_off + rows_e, (region - rows_e) // WIN,
                            tile_end[-1:], n_row_tiles - tile_end[-1:]]).astype(I32)
    return cnt.reshape(-1).astype(I32), dest.reshape(-1).astype(I32), fill, tile_expert, tile_valid


def _slot_offsets(cnt_sm, tile):
    offs, o = [], 0
    for e in range(N_EXPERTS):
        offs.append(o)
        o = o + _round_up(cnt_sm[tile * N_EXPERTS + e], WIN)
    return offs


def _window_copies(cnt_sm, dest_sm, tile, offs, local_ref, sorted_hbm, sem, to_sorted):
    total = 0
    for e in range(N_EXPERTS):
        n_win = (cnt_sm[tile * N_EXPERTS + e] + WIN - 1) // WIN
        base_local, base_sorted = offs[e], dest_sm[tile * N_EXPERTS + e]

        def issue(j, _, base_local=base_local, base_sorted=base_sorted):
            loc = local_ref.at[pl.ds(pl.multiple_of(base_local + j * WIN, WIN), WIN)]
            srt = sorted_hbm.at[pl.ds(pl.multiple_of(base_sorted + j * WIN, WIN), WIN)]
            if to_sorted:
                pltpu.make_async_copy(loc, srt, sem).start()
            else:
                pltpu.make_async_copy(srt, loc, sem).start()
            return 0

        lax.fori_loop(0, n_win, issue, 0)
        total = total + n_win
    return total


def _wait_windows(total, local_ref, sorted_hbm, sem):
    def wait(j, _):
        pltpu.make_async_copy(local_ref.at[pl.ds(0, WIN)], sorted_hbm.at[pl.ds(0, WIN)], sem).wait()
        return 0

    lax.fori_loop(0, total, wait, 0)


def _zero_fill(fill_sm, xs_hbm, zero_ref, sem_win, sem_tile):
    zero_ref[...] = jnp.zeros_like(zero_ref)
    win_copy = lambda row: pltpu.make_async_copy(
        zero_ref.at[pl.ds(0, WIN)], xs_hbm.at[pl.ds(pl.multiple_of(row, WIN), WIN)], sem_win)
    tile_copy = lambda row: pltpu.make_async_copy(
        zero_ref, xs_hbm.at[pl.ds(pl.multiple_of(row, ROW_TILE), ROW_TILE)], sem_tile)
    n_pad = 0
    for e in range(N_EXPERTS):
        first, n_win = fill_sm[e], fill_sm[N_EXPERTS + e]

        def issue(j, _, first=first):
            win_copy(first + j * WIN).start()
            return 0

        lax.fori_loop(0, n_win, issue, 0)
        n_pad = n_pad + n_win
    first_tile, n_tail = fill_sm[2 * N_EXPERTS], fill_sm[2 * N_EXPERTS + 1]

    def issue_tile(j, _):
        tile_copy((first_tile + j) * ROW_TILE).start()
        return 0

    def wait_win(j, _):
        win_copy(0).wait()
        return 0

    def wait_tile(j, _):
        tile_copy(0).wait()
        return 0

    lax.fori_loop(0, n_tail, issue_tile, 0)
    lax.fori_loop(0, n_pad, wait_win, 0)
    lax.fori_loop(0, n_tail, wait_tile, 0)


def _dispatch_kernel(cnt_sm, dest_sm, fill_sm, x_ref, gt_ref, xs_hbm, local_ref, zero_ref, sem, sem_win, sem_tile):
    tile = pl.program_id(0)

    @pl.when(tile == 0)
    def _():
        _zero_fill(fill_sm, xs_hbm, zero_ref, sem_win, sem_tile)

    offs = _slot_offsets(cnt_sm, tile)
    chosen = gt_ref[...] > 0.0
    t_r = lax.broadcasted_iota(I32, (MOE_TILE, MOE_TILE), 0)
    t_c = lax.broadcasted_iota(I32, (MOE_TILE, MOE_TILE), 1)
    earlier = _dot(jnp.where(chosen, 1.0, 0.0).astype(BF16), (t_r < t_c).astype(BF16))
    e_id = lax.broadcasted_iota(I32, (N_EXPERTS, 1), 0)
    slot = jnp.zeros((N_EXPERTS, 1), I32)
    for e in range(N_EXPERTS):
        slot = jnp.where(e_id == e, offs[e], slot)
    pos = earlier + slot.astype(F32)
    p_lo = jnp.min(jnp.where(chosen, pos, float(LOCAL_ROWS)), axis=0, keepdims=True).astype(I32)
    p_hi = jnp.max(jnp.where(chosen, pos, -1.0), axis=0, keepdims=True).astype(I32)
    row = lax.broadcasted_iota(I32, (LOCAL_ROWS, MOE_TILE), 0)
    onehot = jnp.where(row == p_lo, 1.0, jnp.where(row == p_hi, 1.0, 0.0)).astype(BF16)
    local_ref[...] = _dot(onehot, x_ref[...].astype(BF16)).astype(BF16)
    total = _window_copies(cnt_sm, dest_sm, tile, offs, local_ref, xs_hbm, sem, True)
    _wait_windows(total, local_ref, xs_hbm, sem)


def _dispatch(x1, gates_t, cnt_flat, dest_flat, fill):
    s = x1.shape[0]
    return pl.pallas_call(
        _dispatch_kernel,
        grid_spec=pltpu.PrefetchScalarGridSpec(
            num_scalar_prefetch=3,
            grid=(s // MOE_TILE,),
            in_specs=[pl.BlockSpec((MOE_TILE, D_MODEL), lambda i, *_: (i, 0)),
                      pl.BlockSpec((N_EXPERTS, MOE_TILE), lambda i, *_: (0, i))],
            out_specs=pl.BlockSpec(memory_space=pl.ANY),
            scratch_shapes=[pltpu.VMEM((LOCAL_ROWS, D_MODEL), BF16), pltpu.VMEM((ROW_TILE, D_MODEL), BF16),
                            pltpu.SemaphoreType.DMA(()), pltpu.SemaphoreType.DMA(()),
                            pltpu.SemaphoreType.DMA(())]),
        out_shape=jax.ShapeDtypeStruct((_sorted_rows(s), D_MODEL), BF16),
        compiler_params=_params("arbitrary"),
        name="moe_dispatch",
    )(cnt_flat, dest_flat, fill, x1, gates_t)


def _expert_kernel(te_sm, tv_sm, xs_ref, wg_ref, wu_ref, wd_ref, y_ref, wg_b, wu_b, wd_b):
    k = pl.program_id(0)
    valid = tv_sm[k]
    new_expert = jnp.logical_or(k == 0, te_sm[k] != te_sm[jnp.maximum(k - 1, 0)])

    @pl.when(jnp.logical_and(valid > 0, new_expert))
    def _():
        wg_b[...] = wg_ref[0, 0].astype(BF16)
        wu_b[...] = wu_ref[0, 0].astype(BF16)
        wd_b[...] = wd_ref[0, 0].astype(BF16)

    @pl.when(valid > 0)
    def _():
        x = xs_ref[...]
        hg = _dot(x, wg_b[...])
        hu = _dot(x, wu_b[...])
        act = hg * jax.nn.sigmoid(hg) * hu
        y_ref[...] = _dot(act.astype(BF16), wd_b[...]).astype(y_ref.dtype)

    @pl.when(valid == 0)
    def _():
        y_ref[...] = jnp.zeros_like(y_ref)


def _experts(xs, tile_expert, tile_valid, w_gate, w_up, w_down, layer):
    rows = xs.shape[0]
    w_in_spec = pl.BlockSpec((1, 1, D_MODEL, D_FF_EXPERT), lambda k, te, tv: (layer, te[k], 0, 0))
    return pl.pallas_call(
        _expert_kernel,
        grid_spec=pltpu.PrefetchScalarGridSpec(
            num_scalar_prefetch=2,
            grid=(rows // ROW_TILE,),
            in_specs=[pl.BlockSpec((ROW_TILE, D_MODEL), lambda k, te, tv: (k, 0)),
                      w_in_spec, w_in_spec,
                      pl.BlockSpec((1, 1, D_FF_EXPERT, D_MODEL), lambda k, te, tv: (layer, te[k], 0, 0))],
            out_specs=pl.BlockSpec((ROW_TILE, D_MODEL), lambda k, te, tv: (k, 0)),
            scratch_shapes=[pltpu.VMEM((D_MODEL, D_FF_EXPERT), BF16), pltpu.VMEM((D_MODEL, D_FF_EXPERT), BF16),
                            pltpu.VMEM((D_FF_EXPERT, D_MODEL), BF16)]),
        out_shape=jax.ShapeDtypeStruct((rows, D_MODEL), BF16),
        compiler_params=_params("arbitrary"),
        name="moe_experts",
    )(tile_expert, tile_valid, xs, w_gate, w_up, w_down)


def _combine_kernel(cnt_sm, dest_sm, x_ref, gates_ref, y_hbm, g_ref, b_ref, o_ref, local_ref, sem):
    tile = pl.program_id(0)

    @pl.when(tile == 0)
    def _():
        local_ref[...] = jnp.zeros_like(local_ref)

    offs = _slot_offsets(cnt_sm, tile)
    total = _window_copies(cnt_sm, dest_sm, tile, offs, local_ref, y_hbm, sem, False)

    gates = gates_ref[...]
    chosen = gates > 0.0
    t_r = lax.broadcasted_iota(I32, (MOE_TILE, MOE_TILE), 0)
    t_c = lax.broadcasted_iota(I32, (MOE_TILE, MOE_TILE), 1)
    earlier = _dot((t_c < t_r).astype(BF16), jnp.where(chosen, 1.0, 0.0).astype(BF16))
    e_id = lax.broadcasted_iota(I32, (1, BLK), 1)
    slot = jnp.zeros((1, BLK), I32)
    for e in range(N_EXPERTS):
        slot = jnp.where(e_id == e, offs[e], slot)
    pos = jnp.where(chosen, earlier + slot.astype(F32), -1.0)
    p_lo = jnp.min(jnp.where(chosen, pos, float(LOCAL_ROWS)), axis=1, keepdims=True)
    p_hi = jnp.max(pos, axis=1, keepdims=True)
    w_lo = jnp.sum(jnp.where(pos == p_lo, gates, 0.0), axis=1, keepdims=True)
    w_hi = jnp.sum(jnp.where(pos == p_hi, gates, 0.0), axis=1, keepdims=True)
    col = lax.broadcasted_iota(I32, (MOE_TILE, LOCAL_ROWS), 1)
    weights = jnp.where(col == p_lo.astype(I32), w_lo, jnp.where(col == p_hi.astype(I32), w_hi, 0.0))

    _wait_windows(total, local_ref, y_hbm, sem)
    ffn = _dot(weights.astype(BF16), local_ref[...])
    o_ref[...] = _layer_norm(DEEPNORM_ALPHA * x_ref[...] + ffn, g_ref[...], b_ref[...])


def _combine(x1, gates, y, cnt_flat, dest_flat, g, b):
    s = x1.shape[0]
    row = lambda n: pl.BlockSpec((MOE_TILE, n), lambda i, *_: (i, 0))
    vec = pl.BlockSpec((1, D_MODEL), lambda i, *_: (0, 0))
    return pl.pallas_call(
        _combine_kernel,
        grid_spec=pltpu.PrefetchScalarGridSpec(
            num_scalar_prefetch=2,
            grid=(s // MOE_TILE,),
            in_specs=[row(D_MODEL), row(BLK), pl.BlockSpec(memory_space=pl.ANY), vec, vec],
            out_specs=row(D_MODEL),
            scratch_shapes=[pltpu.VMEM((LOCAL_ROWS, D_MODEL), BF16), pltpu.SemaphoreType.DMA(())]),
        out_shape=jax.ShapeDtypeStruct((s, D_MODEL), F32),
        compiler_params=_params("arbitrary"),
        name="moe_combine",
    )(cnt_flat, dest_flat, x1, gates, y, g, b)


def _moe(x1, gates, gates_t, cnt, w_gate, w_up, w_down, g, b, layer):
    s = x1.shape[0]
    cnt_flat, dest_flat, fill, tile_expert, tile_valid = _moe_plan(cnt[:, :, 0], s)
    xs = _dispatch(x1, gates_t, cnt_flat, dest_flat, fill)
    y = _experts(xs, tile_expert, tile_valid, w_gate, w_up, w_down, layer)
    return _combine(x1, gates, y, cnt_flat, dest_flat, g, b)


def _reorder_w_in(w):
    sb = w[:, 0:3 * D_SB]
    o = 3 * D_SB
    ret = w[:, o:o + 4 * D_RET]
    o += 4 * D_RET
    d_q = w[:, o:o + D_DSA]
    o += D_DSA
    d_ckv = w[:, o:o + KV_RANK]
    o += KV_RANK
    d_qi = w[:, o:o + IDX_HEADS * IDX_DIM]
    o += IDX_HEADS * IDX_DIM
    tail = w[:, o:]
    parts = [sb, d_q, ret, d_qi, d_ckv, tail]
    width = sum(a.shape[1] for a in parts)
    parts.append(jnp.zeros((w.shape[0], D_PROJ - width), w.dtype))
    return jnp.concatenate(parts, axis=1).astype(BF16)


def kernel(x, w_in, w_kv_up, kv_norm_g, ret_gn_g, w_o, ln1_g, ln1_b, w_router, w_gate, w_up, w_down,
           ln2_g, ln2_b):
    b, s, _ = x.shape
    assert b == 1 and s % KEY_CHUNK == 0
    h = x[0]
    tables = _retention_tables(s)
    wr_t = w_router.T
    for l in range(DEPTH):
        p = _proj(h, _reorder_w_in(w_in[l]))
        y_sb = _stick_breaking(p)
        y_ret = _retention(p, ret_gn_g[l][None, :], tables)
        w_kv = w_kv_up[l].reshape(KV_RANK, DSA_HEADS, 2, HEAD_DIM)
        w_kv = jnp.concatenate([w_kv[:, :, 0, :].reshape(KV_RANK, D_DSA),
                                w_kv[:, :, 1, :].reshape(KV_RANK, D_DSA)], axis=1).astype(BF16)
        k_dsa, vt_dsa = _kv_up(p, kv_norm_g[l][None, :], w_kv)
        y_dsa = _dsa(p, k_dsa, vt_dsa)
        wo = w_o[l].astype(BF16)
        x1, gates, gates_t, cnt = _out_proj(y_sb, y_ret, y_dsa, wo, h, ln1_g[l][None, :], ln1_b[l][None, :], wr_t)
        h = _moe(x1, gates, gates_t, cnt, w_gate, w_up, w_down, ln2_g[l][None, :], ln2_b[l][None, :], l)
    return h[None]
```

```python
import functools

import numpy as np
import jax
import jax.numpy as jnp
from jax import lax
from jax.experimental import pallas as pl
from jax.experimental.pallas import tpu as pltpu

F32 = jnp.float32
BF16 = jnp.bfloat16
I32 = jnp.int32

D_MODEL = 2048
HEAD_DIM = 128
SB_HEADS = 6
RET_HEADS = 4
DSA_HEADS = 6
D_SB = SB_HEADS * HEAD_DIM
D_RET = RET_HEADS * HEAD_DIM
D_DSA = DSA_HEADS * HEAD_DIM
KV_RANK = 256
IDX_HEADS = 8
IDX_DIM = 64
IDX_SCALE = IDX_DIM ** -0.5 * IDX_HEADS ** -0.5
TOPK_MAX = 256
BLK = 128
N_EXPERTS = 16
N_GROUPS = 4
EXPERTS_PER_GROUP = N_EXPERTS // N_GROUPS
D_FF_EXPERT = 512
LN_EPS = 1e-5
RMS_EPS = 1e-6
GN_EPS = 1e-6
DEPTH = 2
DEEPNORM_ALPHA = (2 * DEPTH) ** 0.25

OFF_SB_Q = 0
OFF_SB_K = OFF_SB_Q + D_SB
OFF_SB_V = OFF_SB_K + D_SB
OFF_D_Q = OFF_SB_V + D_SB
OFF_R_Q = OFF_D_Q + D_DSA
OFF_R_K = OFF_R_Q + D_RET
OFF_R_V = OFF_R_K + D_RET
OFF_R_G = OFF_R_V + D_RET
OFF_D_QI = OFF_R_G + D_RET
OFF_D_CKV = OFF_D_QI + IDX_HEADS * IDX_DIM
OFF_TAIL = OFF_D_CKV + KV_RANK
D_PROJ = 6144

KEY_CHUNK = 512
MOE_TILE = 512
WIN = 16
ROW_TILE = 512
VMEM_LIMIT = 56 * 1024 * 1024
LOG2_E = 1.4426950408889634
NEG_BIG = -1e30
EXP_UNDERFLOW = -87.4
KEY_OF_NEG_INF = -2139095041
INT_MIN = -2147483648


def _dot(a, b):
    return jnp.dot(a, b, preferred_element_type=F32)


def _dot_nt(a, b):
    return lax.dot_general(a, b, (((1,), (1,)), ((), ())), preferred_element_type=F32)


def _dot_tn(a, b):
    return lax.dot_general(a, b, (((0,), (0,)), ((), ())), preferred_element_type=F32)


def _params(*sem):
    return pltpu.CompilerParams(dimension_semantics=sem, vmem_limit_bytes=VMEM_LIMIT)


def _resident(shape, index_map):
    return pl.BlockSpec(shape, index_map, pipeline_mode=pl.Buffered(1))


def _proj_kernel(x_ref, w_ref, o_ref):
    o_ref[...] = _dot(x_ref[...].astype(BF16), w_ref[...]).astype(o_ref.dtype)


def _proj(x, w):
    s, d = x.shape
    n = w.shape[1]
    tm = min(1024, s)
    tn = 1536
    return pl.pallas_call(
        _proj_kernel,
        grid=(s // tm, n // tn),
        in_specs=[pl.BlockSpec((tm, d), lambda i, j: (i, 0)),
                  pl.BlockSpec((d, tn), lambda i, j: (0, j))],
        out_specs=pl.BlockSpec((tm, tn), lambda i, j: (i, j)),
        out_shape=jax.ShapeDtypeStruct((s, n), BF16),
        compiler_params=_params("parallel", "arbitrary"),
        name="proj",
    )(x, w)


def _sb_kernel(q_ref, k_ref, v_ref, o_ref, acc_ref):
    i = pl.program_id(0)
    q = q_ref[...]
    scale = HEAD_DIM ** -0.5
    key_pos = lax.broadcasted_iota(I32, (BLK, BLK), 0)
    qry_pos = lax.broadcasted_iota(I32, (BLK, BLK), 1)
    later = (qry_pos > key_pos).astype(BF16)

    def key_tile(j, first_query, cs):
        off = pl.multiple_of(j * BLK, BLK)
        strict = (off + key_pos) < (first_query + qry_pos)
        pvs, new_cs = [], []
        for h in range(SB_HEADS):
            hs = slice(h * HEAD_DIM, (h + 1) * HEAD_DIM)
            z = _dot_nt(k_ref[pl.ds(off, BLK), hs], q[:, hs]) * scale
            sp = jnp.maximum(z, 0.0) + jnp.log1p(jnp.exp(-jnp.abs(z)))
            log_rem = jnp.where(strict, -sp, 0.0)
            hi = log_rem.astype(BF16)
            lo = (log_rem - hi.astype(F32)).astype(BF16)
            after = _dot(later, hi) + _dot(later, lo)
            a = jnp.where(strict, jnp.exp(z - sp + after + cs[h]), 0.0)
            pvs.append(_dot_tn(a.astype(BF16), v_ref[pl.ds(off, BLK), hs]))
            new_cs.append(cs[h] + jnp.sum(log_rem, axis=0, keepdims=True))
        return pvs, tuple(new_cs)

    cs = tuple(jnp.zeros((1, BLK), F32) for _ in range(SB_HEADS))
    pv_a, cs = key_tile(i, i * BLK, cs)
    pv_b, cs = key_tile(jnp.maximum(i - 1, 0), jnp.where(i >= 1, i * BLK, -BLK), cs)
    for h in range(SB_HEADS):
        acc_ref[h] = pv_a[h] + pv_b[h]

    def cond(carry):
        j, cs = carry
        c_max = functools.reduce(jnp.maximum, cs)
        return jnp.logical_and(j >= 0, jnp.max(c_max) > EXP_UNDERFLOW)

    def body(carry):
        j, cs = carry
        pvs, cs = key_tile(j, i * BLK, cs)
        for h in range(SB_HEADS):
            acc_ref[h] += pvs[h]
        return j - 1, cs

    lax.while_loop(cond, body, (i - 2, cs))
    for h in range(SB_HEADS):
        o_ref[:, h * HEAD_DIM:(h + 1) * HEAD_DIM] = acc_ref[h].astype(o_ref.dtype)


def _stick_breaking(p):
    s = p.shape[0]
    nb = s // BLK
    return pl.pallas_call(
        _sb_kernel,
        grid=(nb,),
        in_specs=[pl.BlockSpec((BLK, D_SB), lambda i: (i, OFF_SB_Q // D_SB)),
                  _resident((s, D_SB), lambda i: (0, OFF_SB_K // D_SB)),
                  _resident((s, D_SB), lambda i: (0, OFF_SB_V // D_SB))],
        out_specs=pl.BlockSpec((BLK, D_SB), lambda i: (i, 0)),
        out_shape=jax.ShapeDtypeStruct((s, D_SB), BF16),
        scratch_shapes=[pltpu.VMEM((SB_HEADS, BLK, HEAD_DIM), F32)],
        compiler_params=_params("arbitrary"),
        name="stick_breaking",
    )(p, p, p)


def _ret_kernel(q_ref, k_ref, v_ref, g_ref, cos_ref, sin_ref, intra_ref, qd_ref, kd_ref, cd_ref,
                gn_ref, o_ref, state_ref):
    n = pl.program_id(0)

    @pl.when(n == 0)
    def _():
        state_ref[...] = jnp.zeros_like(state_ref)

    cos = cos_ref[...]
    sin = sin_ref[...]
    for h in range(RET_HEADS):
        hs = slice(h * HEAD_DIM, (h + 1) * HEAD_DIM)
        q = q_ref[:, hs].astype(F32)
        k = k_ref[:, hs].astype(F32)
        v = v_ref[:, hs]
        qr = q * cos + pltpu.roll(q, HEAD_DIM // 2, 1) * sin
        kr = (k * cos + pltpu.roll(k, HEAD_DIM // 2, 1) * sin) * (HEAD_DIM ** -0.5)
        scores = _dot_nt(qr.astype(BF16), kr.astype(BF16)) * intra_ref[h]
        state = state_ref[h]
        o = (_dot(scores.astype(BF16), v)
             + _dot((qr * qd_ref[h]).astype(BF16), state.astype(BF16)))
        state_ref[h] = cd_ref[h] * state + _dot_tn((kr * kd_ref[h]).astype(BF16), v)
        mu = jnp.mean(o, axis=1, keepdims=True)
        var = jnp.mean(jnp.square(o - mu), axis=1, keepdims=True)
        on = (o - mu) * lax.rsqrt(var + GN_EPS) * gn_ref[:, hs]
        g = g_ref[:, hs].astype(F32)
        o_ref[:, hs] = (g * jax.nn.sigmoid(g) * on).astype(o_ref.dtype)


def _retention_tables(s):
    f32 = np.float32
    half = HEAD_DIM // 2
    pos = np.arange(s, dtype=f32)
    theta = (f32(10000.0) ** (-np.linspace(0.0, 1.0, half, dtype=f32))).astype(f32)
    ang = (pos[:, None] * theta[None, :]).astype(f32)
    cos, sin = np.cos(ang).astype(f32), np.sin(ang).astype(f32)
    cos2 = np.concatenate([cos, cos], axis=1)
    sin2 = np.concatenate([-sin, sin], axis=1)
    log_gamma = np.log1p(-(f32(2.0) ** (-5.0 - np.arange(RET_HEADS, dtype=f32)))).astype(f32)
    idx = np.arange(BLK, dtype=f32)
    diff = idx[:, None] - idx[None, :]
    intra = np.where(diff >= 0, np.exp(np.maximum(diff, 0.0)[None] * log_gamma[:, None, None]), 0.0).astype(f32)
    q_decay = np.exp((idx[None, :] + 1.0) * log_gamma[:, None]).astype(f32)
    k_decay = np.exp((BLK - 1.0 - idx[None, :]) * log_gamma[:, None]).astype(f32)
    chunk_decay = np.exp(BLK * log_gamma).astype(f32)
    full = (RET_HEADS, BLK, HEAD_DIM)
    return tuple(jnp.asarray(np.ascontiguousarray(t)) for t in (
        cos2, sin2, intra,
        np.broadcast_to(q_decay[:, :, None], full),
        np.broadcast_to(k_decay[:, :, None], full),
        np.broadcast_to(chunk_decay[:, None, None], full)))


def _retention(p, gn_g, tables):
    s = p.shape[0]
    nc = s // BLK
    cos2, sin2, intra, qd, kd, cd = tables
    col = lambda off: pl.BlockSpec((BLK, D_RET), lambda n: (n, off // D_RET))
    per_head = pl.BlockSpec((RET_HEADS, BLK, HEAD_DIM), lambda n: (0, 0, 0))
    pos_spec = pl.BlockSpec((BLK, HEAD_DIM), lambda n: (n, 0))
    return pl.pallas_call(
        _ret_kernel,
        grid=(nc,),
        in_specs=[col(OFF_R_Q), col(OFF_R_K), col(OFF_R_V), col(OFF_R_G),
                  pos_spec, pos_spec, per_head, per_head, per_head, per_head,
                  pl.BlockSpec((1, D_RET), lambda n: (0, 0))],
        out_specs=pl.BlockSpec((BLK, D_RET), lambda n: (n, 0)),
        out_shape=jax.ShapeDtypeStruct((s, D_RET), BF16),
        scratch_shapes=[pltpu.VMEM((RET_HEADS, HEAD_DIM, HEAD_DIM), F32)],
        compiler_params=_params("arbitrary"),
        name="retention",
    )(p, p, p, p, cos2, sin2, intra, qd, kd, cd, gn_g)


def _kv_up_kernel(c_ref, g_ref, w_ref, k_ref, vt_ref):
    c = c_ref[...].astype(F32)
    y = c * lax.rsqrt(jnp.mean(jnp.square(c), axis=1, keepdims=True) + RMS_EPS) * g_ref[...]
    kv = _dot(y.astype(BF16), w_ref[...])
    k_ref[...] = (kv[:, :D_DSA] * (HEAD_DIM ** -0.5 * LOG2_E)).astype(k_ref.dtype)
    v_t = kv[:, D_DSA:].T
    ones = jnp.ones((V_ROWS - HEAD_DIM, v_t.shape[1]), vt_ref.dtype)
    for h in range(DSA_HEADS):
        vt_ref[0, h * V_ROWS:h * V_ROWS + HEAD_DIM, :] = v_t[h * HEAD_DIM:(h + 1) * HEAD_DIM, :].astype(vt_ref.dtype)
        vt_ref[0, h * V_ROWS + HEAD_DIM:(h + 1) * V_ROWS, :] = ones


def _kv_up(p, g, w):
    s = p.shape[0]
    n = w.shape[1]
    return pl.pallas_call(
        _kv_up_kernel,
        grid=(s // KEY_CHUNK,),
        in_specs=[pl.BlockSpec((KEY_CHUNK, KV_RANK), lambda i: (i, OFF_D_CKV // KV_RANK)),
                  pl.BlockSpec((1, KV_RANK), lambda i: (0, 0)),
                  pl.BlockSpec((KV_RANK, n), lambda i: (0, 0))],
        out_specs=[pl.BlockSpec((KEY_CHUNK, D_DSA), lambda i: (i, 0)),
                   pl.BlockSpec((1, DSA_HEADS * V_ROWS, KEY_CHUNK), lambda i: (i, 0, 0))],
        out_shape=[jax.ShapeDtypeStruct((s, D_DSA), BF16),
                   jax.ShapeDtypeStruct((s // KEY_CHUNK, DSA_HEADS * V_ROWS, KEY_CHUNK), BF16)],
        compiler_params=_params("parallel"),
        name="kv_up",
    )(p, g, w)


def _ordered_bits_to_float(u):
    return pltpu.bitcast(u ^ ((u >> 31) & 0x7FFFFFFF), F32)


V_ROWS = HEAD_DIM + 16
GROUPS_PER_CHUNK = KEY_CHUNK // (32 * 8)


def _bit_transpose32(words):
    a = list(words)
    j, mask = 16, 0x0000FFFF
    while j:
        k = 0
        while k < 32:
            t = (a[k] ^ (a[k + j] >> j)) & mask
            a[k] = a[k] ^ t
            a[k + j] = a[k + j] ^ (t << j)
            k = (k + j + 1) & ~j
        j >>= 1
        mask = (mask ^ (mask << j)) & 0xFFFFFFFF
    return a


def _chunks_of_block(blk):
    return ((blk + 1) * BLK + KEY_CHUNK - 1) // KEY_CHUNK


def _dsa_kernel(q_ref, qi_ref, tq_ref, qi_next_ref, tq_next_ref, tail_ref, k_ref, vt_ref, o_ref,
                score_ref, planes_ref, alive_ref, bias_ref, acc_ref, lg_a, lg_b, *, topk):
    i = pl.program_id(0)
    nxt = jnp.minimum(i + 1, pl.num_programs(0) - 1)
    n_chunks = _chunks_of_block(i)
    n_kc = score_ref.shape[0]

    def indexer(qi_blk_ref, tq_blk_ref, blk):
        w_t = tq_blk_ref[...].astype(F32).T[IDX_DIM:IDX_DIM + IDX_HEADS, :] * IDX_SCALE
        qi = qi_blk_ref[...]
        qi_rows = jnp.concatenate([qi[:, h * IDX_DIM:(h + 1) * IDX_DIM] for h in range(IDX_HEADS)], axis=0)

        def chunk_scores(c):
            off = pl.multiple_of(c * KEY_CHUNK, KEY_CHUNK)
            rel = jnp.maximum(_dot_nt(tail_ref[pl.ds(off, KEY_CHUNK), 0:IDX_DIM], qi_rows), 0.0)
            score = rel[:, 0:BLK] * w_t[0:1, :]
            for h in range(1, IDX_HEADS):
                score = score + rel[:, h * BLK:(h + 1) * BLK] * w_t[h:h + 1, :]
            return score

        def store_chunk(c, score):
            score = jnp.where(score == 0.0, 0.0, score)
            score_ref[c] = score
            bits = pltpu.bitcast(score, I32)
            u = bits ^ ((bits >> 31) | INT_MIN)
            for g in range(GROUPS_PER_CHUNK):
                words = [u[(g * 32 + j) * 8:(g * 32 + j + 1) * 8, :] for j in range(32)]
                planes = _bit_transpose32(words)
                for b in range(32):
                    planes_ref[c * GROUPS_PER_CHUNK + g, b] = planes[b]

        def full_chunk(c):
            store_chunk(c, chunk_scores(c))

        def last_chunk():
            last = _chunks_of_block(blk) - 1
            key_pos = last * KEY_CHUNK + lax.broadcasted_iota(I32, (KEY_CHUNK, BLK), 0)
            t_col = blk * BLK + lax.broadcasted_iota(I32, (KEY_CHUNK, BLK), 1)
            store_chunk(last, jnp.where(key_pos <= t_col, chunk_scores(last), -jnp.inf))

        return full_chunk, last_chunk

    @pl.when(i == 0)
    def _():
        indexer(qi_ref, tq_ref, 0)[1]()

    def init_alive(c, _):
        for g in range(GROUPS_PER_CHUNK):
            alive_ref[c * GROUPS_PER_CHUNK + g] = jnp.full((8, BLK), -1, I32)
        return 0

    lax.fori_loop(0, n_chunks, init_alive, 0)

    @pl.when(n_chunks % 2 == 1)
    def _():
        for g in range(GROUPS_PER_CHUNK):
            alive_ref[n_chunks * GROUPS_PER_CHUNK + g] = jnp.zeros((8, BLK), I32)
            planes_ref[n_chunks * GROUPS_PER_CHUNK + g] = jnp.zeros((32, 8, BLK), I32)

    def decide(above, t_bits, ones, bit):
        take = (above + ones) >= topk
        return (jnp.where(take, above, above + ones), jnp.where(take, t_bits | bit, t_bits),
                jnp.where(take, 0, -1))

    groups_per_step = 2 * GROUPS_PER_CHUNK

    def sweep(plane_prev, plane, drop_prev):
        def step(p, acc):
            for g in range(groups_per_step):
                gi = p * groups_per_step + g
                alive = alive_ref[gi]
                if plane_prev is not None:
                    alive = alive & (planes_ref[gi, plane_prev] ^ drop_prev)
                    alive_ref[gi] = alive
                if plane is not None:
                    alive = alive & planes_ref[gi, plane]
                acc = acc + lax.population_count(alive)
            return acc

        acc = lax.fori_loop(0, (n_chunks + 1) // 2, step, jnp.zeros((8, BLK), I32))
        return jnp.sum(acc, axis=0, keepdims=True)

    zero = jnp.zeros((1, BLK), I32)
    state = decide(zero, zero, sweep(None, 0, None), INT_MIN)

    def bit_step(b, state):
        above, t_bits, drop = state
        return decide(above, t_bits, sweep(b - 1, b, drop), jnp.int32(1) << (31 - b))

    above, t_bits, drop = lax.fori_loop(1, 32, bit_step, state)
    n_eq = sweep(31, None, drop)
    need = topk - above
    thr = t_bits ^ INT_MIN
    real = thr > KEY_OF_NEG_INF
    thr_f = jnp.where(real, _ordered_bits_to_float(thr), jnp.finfo(F32).min)

    def bias_chunk(c, _):
        bias_ref[c] = jnp.where(score_ref[c] >= thr_f, 0.0, NEG_BIG)
        return 0

    lax.fori_loop(0, n_chunks, bias_chunk, 0)

    has_tie = jnp.max(jnp.where(jnp.logical_and(real, n_eq > need), 1, 0)) > 0

    @pl.when(has_tie)
    def _():
        need_f = need.astype(F32)
        r = lax.broadcasted_iota(I32, (BLK, BLK), 0)
        cc = lax.broadcasted_iota(I32, (BLK, BLK), 1)
        upto = (cc <= r).astype(BF16)

        def tie_chunk(c, run):
            sc = score_ref[c]
            for u in range(KEY_CHUNK // BLK):
                st = sc[u * BLK:(u + 1) * BLK, :]
                eq = jnp.logical_and(st == thr_f, real)
                eqf = jnp.where(eq, 1.0, 0.0)
                rank = _dot(upto, eqf.astype(BF16)) + run
                sel = jnp.logical_or(st > thr_f, jnp.logical_and(eq, rank <= need_f))
                bias_ref[c, u * BLK:(u + 1) * BLK, :] = jnp.where(
                    real, jnp.where(sel, 0.0, NEG_BIG), jnp.where(st >= thr_f, 0.0, NEG_BIG))
                run = run + jnp.sum(eqf, axis=0, keepdims=True)
            return run

        lax.fori_loop(0, n_chunks, tie_chunk, jnp.zeros((1, BLK), F32))

    acc_ref[...] = jnp.zeros_like(acc_ref)
    bias_ref[n_kc] = jnp.full((KEY_CHUNK, BLK), NEG_BIG, F32)
    q = q_ref[...]

    def logits_into(lg, c):
        kc = jnp.minimum(c, n_chunks - 1)
        off = pl.multiple_of(kc * KEY_CHUNK, KEY_CHUNK)
        bias = bias_ref[jnp.where(c < n_chunks, c, n_kc)]
        for h in range(DSA_HEADS):
            hs = slice(h * HEAD_DIM, (h + 1) * HEAD_DIM)
            lg[h] = _dot_nt(k_ref[pl.ds(off, KEY_CHUNK), hs], q[:, hs]) + bias

    def reduce_from(lg, c, ms):
        vc = jnp.minimum(c, n_chunks - 1)
        new_m = []
        for h in range(DSA_HEADS):
            logits = lg[h]
            m_new = jnp.maximum(ms[h], jnp.max(logits, axis=0, keepdims=True))
            alpha = jnp.exp2(ms[h] - m_new)
            pr = jnp.exp2(logits - m_new).astype(BF16)
            acc_ref[h] = alpha * acc_ref[h] + _dot(vt_ref[vc, h * V_ROWS:(h + 1) * V_ROWS, :], pr)
            new_m.append(m_new)
        return tuple(new_m)

    next_full, next_last = indexer(qi_next_ref, tq_next_ref, nxt)
    last_full = _chunks_of_block(nxt) - 2
    n_pairs = (n_chunks + 1) // 2

    def attn_pair(pair, ms):
        c = 2 * pair
        logits_into(lg_b, c + 1)
        ms = reduce_from(lg_a, c, ms)
        next_full(jnp.maximum(jnp.minimum(c, last_full), 0))
        logits_into(lg_a, c + 2)
        ms = reduce_from(lg_b, c + 1, ms)
        next_full(jnp.maximum(jnp.minimum(c + 1, last_full), 0))
        return ms

    logits_into(lg_a, 0)
    lax.fori_loop(0, n_pairs, attn_pair, tuple(jnp.full((1, BLK), NEG_BIG, F32) for _ in range(DSA_HEADS)))
    for h in range(DSA_HEADS):
        weighted = acc_ref[h, 0:HEAD_DIM, :]
        denom = acc_ref[h, HEAD_DIM:HEAD_DIM + 1, :]
        o_ref[:, h * HEAD_DIM:(h + 1) * HEAD_DIM] = (weighted / denom).T.astype(o_ref.dtype)

    def rest(c, _):
        next_full(c)
        return 0

    lax.fori_loop(2 * n_pairs, last_full + 1, rest, 0)
    next_last()


def _dsa(p, k, vt):
    s = p.shape[0]
    nb = s // BLK
    topk = min(TOPK_MAX, s // 4)
    n_kc = s // KEY_CHUNK
    return pl.pallas_call(
        functools.partial(_dsa_kernel, topk=topk),
        grid=(nb,),
        in_specs=[pl.BlockSpec((BLK, D_DSA), lambda i: (i, OFF_D_Q // D_DSA)),
                  pl.BlockSpec((BLK, IDX_HEADS * IDX_DIM), lambda i: (i, OFF_D_QI // (IDX_HEADS * IDX_DIM))),
                  pl.BlockSpec((BLK, BLK), lambda i: (i, OFF_TAIL // BLK)),
                  pl.BlockSpec((BLK, IDX_HEADS * IDX_DIM),
                               lambda i: (jnp.minimum(i + 1, nb - 1), OFF_D_QI // (IDX_HEADS * IDX_DIM))),
                  pl.BlockSpec((BLK, BLK), lambda i: (jnp.minimum(i + 1, nb - 1), OFF_TAIL // BLK)),
                  _resident((s, BLK), lambda i: (0, OFF_TAIL // BLK)),
                  _resident((s, D_DSA), lambda i: (0, 0)),
                  _resident((n_kc, DSA_HEADS * V_ROWS, KEY_CHUNK), lambda i: (0, 0, 0))],
        out_specs=pl.BlockSpec((BLK, D_DSA), lambda i: (i, 0)),
        out_shape=jax.ShapeDtypeStruct((s, D_DSA), BF16),
        scratch_shapes=[pltpu.VMEM((n_kc, KEY_CHUNK, BLK), F32),
                        pltpu.VMEM((n_kc * GROUPS_PER_CHUNK, 32, 8, BLK), I32),
                        pltpu.VMEM((n_kc * GROUPS_PER_CHUNK, 8, BLK), I32),
                        pltpu.VMEM((n_kc + 1, KEY_CHUNK, BLK), F32),
                        pltpu.VMEM((DSA_HEADS, V_ROWS, BLK), F32),
                        pltpu.VMEM((DSA_HEADS, KEY_CHUNK, BLK), F32),
                        pltpu.VMEM((DSA_HEADS, KEY_CHUNK, BLK), F32)],
        compiler_params=_params("arbitrary"),
        name="dsa",
    )(p, p, p, p, p, p, k, vt)


def _layer_norm(r, g, b):
    mu = jnp.mean(r, axis=1, keepdims=True)
    var = jnp.mean(jnp.square(r - mu), axis=1, keepdims=True)
    return (r - mu) * lax.rsqrt(var + LN_EPS) * g + b


def _first_max_of4(vals):
    a, b, c, d = vals
    m = jnp.maximum(jnp.maximum(a, b), jnp.maximum(c, d))
    idx = jnp.where(a == m, 0, jnp.where(b == m, 1, jnp.where(c == m, 2, 3)))
    return m, idx


def _router_gates(logits_t):
    mx = jnp.max(logits_t, axis=0, keepdims=True)
    e = jnp.exp(logits_t - mx)
    probs = e / jnp.sum(e, axis=0, keepdims=True)
    rows = [probs[j:j + 1, :] for j in range(N_EXPERTS)]
    m1s, m2s, i1s, i2s, scores = [], [], [], [], []
    for g in range(N_GROUPS):
        vals = rows[g * EXPERTS_PER_GROUP:(g + 1) * EXPERTS_PER_GROUP]
        m1, i1 = _first_max_of4(vals)
        rest = [jnp.where(i1 == j, -1.0, vals[j]) for j in range(EXPERTS_PER_GROUP)]
        m2, i2 = _first_max_of4(rest)
        m1s.append(m1); m2s.append(m2); i1s.append(i1); i2s.append(i2); scores.append(m1 + m2)
    best, g_sel = _first_max_of4(scores)
    pick = lambda xs: jnp.where(g_sel == 0, xs[0], jnp.where(g_sel == 1, xs[1], jnp.where(g_sel == 2, xs[2], xs[3])))
    m1, m2, i1, i2 = pick(m1s), pick(m2s), pick(i1s), pick(i2s)
    den = m1 + m2
    w1, w2 = m1 / den, m2 / den
    e1 = g_sel * EXPERTS_PER_GROUP + i1
    e2 = g_sel * EXPERTS_PER_GROUP + i2
    gates = [jnp.where(e1 == j, w1, 0.0) + jnp.where(e2 == j, w2, 0.0) for j in range(N_EXPERTS)]
    return jnp.concatenate(gates, axis=0)


def _out_kernel(ysb_ref, yret_ref, ydsa_ref, wo_ref, x_ref, g_ref, b_ref, wr_ref,
                x1_ref, gates_ref, gates_t_ref, cnt_ref):
    mix = (_dot(ysb_ref[...], wo_ref[0:D_SB, :]) + _dot(yret_ref[...], wo_ref[D_SB:D_SB + D_RET, :])
           + _dot(ydsa_ref[...], wo_ref[D_SB + D_RET:, :]))
    x1 = _layer_norm(DEEPNORM_ALPHA * x_ref[...] + mix, g_ref[...], b_ref[...])
    x1_ref[...] = x1
    logits_t = lax.dot_general(wr_ref[...], x1, (((1,), (1,)), ((), ())),
                               preferred_element_type=F32, precision=lax.Precision.HIGHEST)
    gates_t = _router_gates(logits_t)
    tm = x1.shape[0]
    gates_t_ref[...] = gates_t
    padded = jnp.concatenate([gates_t, jnp.zeros((BLK - N_EXPERTS, tm), F32)], axis=0)
    gates_ref[...] = padded.T
    chosen = jnp.sum(jnp.where(gates_t > 0.0, 1, 0), axis=1, keepdims=True)
    cnt_ref[0] = jnp.broadcast_to(chosen, (N_EXPERTS, BLK))


def _out_proj(ysb, yret, ydsa, wo, x, g, b, wr_t):
    s = x.shape[0]
    tm = MOE_TILE
    row = lambda n: pl.BlockSpec((tm, n), lambda i: (i, 0))
    whole = lambda a: _resident(a.shape, lambda i: (0, 0))
    return pl.pallas_call(
        _out_kernel,
        grid=(s // tm,),
        in_specs=[row(D_SB), row(D_RET), row(D_DSA), whole(wo),
                  row(D_MODEL), whole(g), whole(b), whole(wr_t)],
        out_specs=[row(D_MODEL), row(BLK), pl.BlockSpec((N_EXPERTS, tm), lambda i: (0, i)),
                   pl.BlockSpec((1, N_EXPERTS, BLK), lambda i: (i, 0, 0))],
        out_shape=[jax.ShapeDtypeStruct((s, D_MODEL), F32), jax.ShapeDtypeStruct((s, BLK), F32),
                   jax.ShapeDtypeStruct((N_EXPERTS, s), F32),
                   jax.ShapeDtypeStruct((s // tm, N_EXPERTS, BLK), I32)],
        compiler_params=_params("parallel"),
        name="out_proj",
    )(ysb, yret, ydsa, wo, x, g, b, wr_t)


LOCAL_ROWS = 2 * MOE_TILE + N_EXPERTS * WIN


def _round_up(x, m):
    return (x + m - 1) // m * m


def _sorted_rows(s):
    n_tiles = s // MOE_TILE
    return _round_up(2 * s + n_tiles * N_EXPERTS * (WIN - 1) + N_EXPERTS * (ROW_TILE - 1), ROW_TILE)


def _moe_plan(cnt, s):
    n_row_tiles = _sorted_rows(s) // ROW_TILE
    seg = _round_up(cnt, WIN)
    rows_e = jnp.sum(seg, axis=0)
    region = _round_up(rows_e, ROW_TILE)
    region_off = jnp.cumsum(region) - region
    dest = region_off[None, :] + jnp.cumsum(seg, axis=0) - seg
    tiles_e = region // ROW_TILE
    tile_end = jnp.cumsum(tiles_e)
    k = jnp.arange(n_row_tiles, dtype=I32)
    tile_expert = jnp.minimum(jnp.sum((k[:, None] >= tile_end[None, :]).astype(I32), axis=1), N_EXPERTS - 1)
    first = (tile_end - tiles_e)[tile_expert]
    valid = jnp.clip(rows_e[tile_expert] - (k - first) * ROW_TILE, 0, ROW_TILE)
    tile_valid = jnp.where(k < tile_end[-1], valid, 0).astype(I32)
    fill = jnp.concatenate([region_off + rows_e, (region - rows_e) // WIN,
                            tile_end[-1:], n_row_tiles - tile_end[-1:]]).astype(I32)
    return cnt.reshape(-1).astype(I32), dest.reshape(-1).astype(I32), fill, tile_expert, tile_valid


def _slot_offsets(cnt_sm, tile):
    offs, o = [], 0
    for e in range(N_EXPERTS):
        offs.append(o)
        o = o + _round_up(cnt_sm[tile * N_EXPERTS + e], WIN)
    return offs


def _window_copies(cnt_sm, dest_sm, tile, offs, local_ref, sorted_hbm, sem, to_sorted):
    total = 0
    for e in range(N_EXPERTS):
        n_win = (cnt_sm[tile * N_EXPERTS + e] + WIN - 1) // WIN
        base_local, base_sorted = offs[e], dest_sm[tile * N_EXPERTS + e]

        def issue(j, _, base_local=base_local, base_sorted=base_sorted):
            loc = local_ref.at[pl.ds(pl.multiple_of(base_local + j * WIN, WIN), WIN)]
            srt = sorted_hbm.at[pl.ds(pl.multiple_of(base_sorted + j * WIN, WIN), WIN)]
            if to_sorted:
                pltpu.make_async_copy(loc, srt, sem).start()
            else:
                pltpu.make_async_copy(srt, loc, sem).start()
            return 0

        lax.fori_loop(0, n_win, issue, 0)
        total = total + n_win
    return total


def _wait_windows(total, local_ref, sorted_hbm, sem):
    def wait(j, _):
        pltpu.make_async_copy(local_ref.at[pl.ds(0, WIN)], sorted_hbm.at[pl.ds(0, WIN)], sem).wait()
        return 0

    lax.fori_loop(0, total, wait, 0)


def _zero_fill(fill_sm, xs_hbm, zero_ref, sem_win, sem_tile):
    zero_ref[...] = jnp.zeros_like(zero_ref)
    win_copy = lambda row: pltpu.make_async_copy(
        zero_ref.at[pl.ds(0, WIN)], xs_hbm.at[pl.ds(pl.multiple_of(row, WIN), WIN)], sem_win)
    tile_copy = lambda row: pltpu.make_async_copy(
        zero_ref, xs_hbm.at[pl.ds(pl.multiple_of(row, ROW_TILE), ROW_TILE)], sem_tile)
    n_pad = 0
    for e in range(N_EXPERTS):
        first, n_win = fill_sm[e], fill_sm[N_EXPERTS + e]

        def issue(j, _, first=first):
            win_copy(first + j * WIN).start()
            return 0

        lax.fori_loop(0, n_win, issue, 0)
        n_pad = n_pad + n_win
    first_tile, n_tail = fill_sm[2 * N_EXPERTS], fill_sm[2 * N_EXPERTS + 1]

    def issue_tile(j, _):
        tile_copy((first_tile + j) * ROW_TILE).start()
        return 0

    def wait_win(j, _):
        win_copy(0).wait()
        return 0

    def wait_tile(j, _):
        tile_copy(0).wait()
        return 0

    lax.fori_loop(0, n_tail, issue_tile, 0)
    lax.fori_loop(0, n_pad, wait_win, 0)
    lax.fori_loop(0, n_tail, wait_tile, 0)


def _dispatch_kernel(cnt_sm, dest_sm, fill_sm, x_ref, gt_ref, xs_hbm, local_ref, zero_ref, sem, sem_win, sem_tile):
    tile = pl.program_id(0)

    @pl.when(tile == 0)
    def _():
        _zero_fill(fill_sm, xs_hbm, zero_ref, sem_win, sem_tile)

    offs = _slot_offsets(cnt_sm, tile)
    chosen = gt_ref[...] > 0.0
    t_r = lax.broadcasted_iota(I32, (MOE_TILE, MOE_TILE), 0)
    t_c = lax.broadcasted_iota(I32, (MOE_TILE, MOE_TILE), 1)
    earlier = _dot(jnp.where(chosen, 1.0, 0.0).astype(BF16), (t_r < t_c).astype(BF16))
    e_id = lax.broadcasted_iota(I32, (N_EXPERTS, 1), 0)
    slot = jnp.zeros((N_EXPERTS, 1), I32)
    for e in range(N_EXPERTS):
        slot = jnp.where(e_id == e, offs[e], slot)
    pos = earlier + slot.astype(F32)
    p_lo = jnp.min(jnp.where(chosen, pos, float(LOCAL_ROWS)), axis=0, keepdims=True).astype(I32)
    p_hi = jnp.max(jnp.where(chosen, pos, -1.0), axis=0, keepdims=True).astype(I32)
    row = lax.broadcasted_iota(I32, (LOCAL_ROWS, MOE_TILE), 0)
    onehot = jnp.where(row == p_lo, 1.0, jnp.where(row == p_hi, 1.0, 0.0)).astype(BF16)
    local_ref[...] = _dot(onehot, x_ref[...].astype(BF16)).astype(BF16)
    total = _window_copies(cnt_sm, dest_sm, tile, offs, local_ref, xs_hbm, sem, True)
    _wait_windows(total, local_ref, xs_hbm, sem)


def _dispatch(x1, gates_t, cnt_flat, dest_flat, fill):
    s = x1.shape[0]
    return pl.pallas_call(
        _dispatch_kernel,
        grid_spec=pltpu.PrefetchScalarGridSpec(
            num_scalar_prefetch=3,
            grid=(s // MOE_TILE,),
            in_specs=[pl.BlockSpec((MOE_TILE, D_MODEL), lambda i, *_: (i, 0)),
                      pl.BlockSpec((N_EXPERTS, MOE_TILE), lambda i, *_: (0, i))],
            out_specs=pl.BlockSpec(memory_space=pl.ANY),
            scratch_shapes=[pltpu.VMEM((LOCAL_ROWS, D_MODEL), BF16), pltpu.VMEM((ROW_TILE, D_MODEL), BF16),
                            pltpu.SemaphoreType.DMA(()), pltpu.SemaphoreType.DMA(()),
                            pltpu.SemaphoreType.DMA(())]),
        out_shape=jax.ShapeDtypeStruct((_sorted_rows(s), D_MODEL), BF16),
        compiler_params=_params("arbitrary"),
        name="moe_dispatch",
    )(cnt_flat, dest_flat, fill, x1, gates_t)


def _expert_kernel(te_sm, tv_sm, xs_ref, wg_ref, wu_ref, wd_ref, y_ref, wg_b, wu_b, wd_b):
    k = pl.program_id(0)
    valid = tv_sm[k]
    new_expert = jnp.logical_or(k == 0, te_sm[k] != te_sm[jnp.maximum(k - 1, 0)])

    @pl.when(jnp.logical_and(valid > 0, new_expert))
    def _():
        wg_b[...] = wg_ref[0, 0].astype(BF16)
        wu_b[...] = wu_ref[0, 0].astype(BF16)
        wd_b[...] = wd_ref[0, 0].astype(BF16)

    @pl.when(valid > 0)
    def _():
        x = xs_ref[...]
        hg = _dot(x, wg_b[...])
        hu = _dot(x, wu_b[...])
        act = hg * jax.nn.sigmoid(hg) * hu
        y_ref[...] = _dot(act.astype(BF16), wd_b[...]).astype(y_ref.dtype)

    @pl.when(valid == 0)
    def _():
        y_ref[...] = jnp.zeros_like(y_ref)


def _experts(xs, tile_expert, tile_valid, w_gate, w_up, w_down, layer):
    rows = xs.shape[0]
    w_in_spec = pl.BlockSpec((1, 1, D_MODEL, D_FF_EXPERT), lambda k, te, tv: (layer, te[k], 0, 0))
    return pl.pallas_call(
        _expert_kernel,
        grid_spec=pltpu.PrefetchScalarGridSpec(
            num_scalar_prefetch=2,
            grid=(rows // ROW_TILE,),
            in_specs=[pl.BlockSpec((ROW_TILE, D_MODEL), lambda k, te, tv: (k, 0)),
                      w_in_spec, w_in_spec,
                      pl.BlockSpec((1, 1, D_FF_EXPERT, D_MODEL), lambda k, te, tv: (layer, te[k], 0, 0))],
            out_specs=pl.BlockSpec((ROW_TILE, D_MODEL), lambda k, te, tv: (k, 0)),
            scratch_shapes=[pltpu.VMEM((D_MODEL, D_FF_EXPERT), BF16), pltpu.VMEM((D_MODEL, D_FF_EXPERT), BF16),
                            pltpu.VMEM((D_FF_EXPERT, D_MODEL), BF16)]),
        out_shape=jax.ShapeDtypeStruct((rows, D_MODEL), BF16),
        compiler_params=_params("arbitrary"),
        name="moe_experts",
    )(tile_expert, tile_valid, xs, w_gate, w_up, w_down)


def _combine_kernel(cnt_sm, dest_sm, x_ref, gates_ref, y_hbm, g_ref, b_ref, o_ref, local_ref, sem):
    tile = pl.program_id(0)

    @pl.when(tile == 0)
    def _():
        local_ref[...] = jnp.zeros_like(local_ref)

    offs = _slot_offsets(cnt_sm, tile)
    total = _window_copies(cnt_sm, dest_sm, tile, offs, local_ref, y_hbm, sem, False)

    gates = gates_ref[...]
    chosen = gates > 0.0
    t_r = lax.broadcasted_iota(I32, (MOE_TILE, MOE_TILE), 0)
    t_c = lax.broadcasted_iota(I32, (MOE_TILE, MOE_TILE), 1)
    earlier = _dot((t_c < t_r).astype(BF16), jnp.where(chosen, 1.0, 0.0).astype(BF16))
    e_id = lax.broadcasted_iota(I32, (1, BLK), 1)
    slot = jnp.zeros((1, BLK), I32)
    for e in range(N_EXPERTS):
        slot = jnp.where(e_id == e, offs[e], slot)
    pos = jnp.where(chosen, earlier + slot.astype(F32), -1.0)
    p_lo = jnp.min(jnp.where(chosen, pos, float(LOCAL_ROWS)), axis=1, keepdims=True)
    p_hi = jnp.max(pos, axis=1, keepdims=True)
    w_lo = jnp.sum(jnp.where(pos == p_lo, gates, 0.0), axis=1, keepdims=True)
    w_hi = jnp.sum(jnp.where(pos == p_hi, gates, 0.0), axis=1, keepdims=True)
    col = lax.broadcasted_iota(I32, (MOE_TILE, LOCAL_ROWS), 1)
    weights = jnp.where(col == p_lo.astype(I32), w_lo, jnp.where(col == p_hi.astype(I32), w_hi, 0.0))

    _wait_windows(total, local_ref, y_hbm, sem)
    ffn = _dot(weights.astype(BF16), local_ref[...])
    o_ref[...] = _layer_norm(DEEPNORM_ALPHA * x_ref[...] + ffn, g_ref[...], b_ref[...])


def _combine(x1, gates, y, cnt_flat, dest_flat, g, b):
    s = x1.shape[0]
    row = lambda n: pl.BlockSpec((MOE_TILE, n), lambda i, *_: (i, 0))
    vec = pl.BlockSpec((1, D_MODEL), lambda i, *_: (0, 0))
    return pl.pallas_call(
        _combine_kernel,
        grid_spec=pltpu.PrefetchScalarGridSpec(
            num_scalar_prefetch=2,
            grid=(s // MOE_TILE,),
            in_specs=[row(D_MODEL), row(BLK), pl.BlockSpec(memory_space=pl.ANY), vec, vec],
            out_specs=row(D_MODEL),
            scratch_shapes=[pltpu.VMEM((LOCAL_ROWS, D_MODEL), BF16), pltpu.SemaphoreType.DMA(())]),
        out_shape=jax.ShapeDtypeStruct((s, D_MODEL), F32),
        compiler_params=_params("arbitrary"),
        name="moe_combine",
    )(cnt_flat, dest_flat, x1, gates, y, g, b)


def _moe(x1, gates, gates_t, cnt, w_gate, w_up, w_down, g, b, layer):
    s = x1.shape[0]
    cnt_flat, dest_flat, fill, tile_expert, tile_valid = _moe_plan(cnt[:, :, 0], s)
    xs = _dispatch(x1, gates_t, cnt_flat, dest_flat, fill)
    y = _experts(xs, tile_expert, tile_valid, w_gate, w_up, w_down, layer)
    return _combine(x1, gates, y, cnt_flat, dest_flat, g, b)


def _reorder_w_in(w):
    sb = w[:, 0:3 * D_SB]
    o = 3 * D_SB
    ret = w[:, o:o + 4 * D_RET]
    o += 4 * D_RET
    d_q = w[:, o:o + D_DSA]
    o += D_DSA
    d_ckv = w[:, o:o + KV_RANK]
    o += KV_RANK
    d_qi = w[:, o:o + IDX_HEADS * IDX_DIM]
    o += IDX_HEADS * IDX_DIM
    tail = w[:, o:]
    parts = [sb, d_q, ret, d_qi, d_ckv, tail]
    width = sum(a.shape[1] for a in parts)
    parts.append(jnp.zeros((w.shape[0], D_PROJ - width), w.dtype))
    return jnp.concatenate(parts, axis=1).astype(BF16)


def kernel(x, w_in, w_kv_up, kv_norm_g, ret_gn_g, w_o, ln1_g, ln1_b, w_router, w_gate, w_up, w_down,
           ln2_g, ln2_b):
    b, s, _ = x.shape
    assert b == 1 and s % KEY_CHUNK == 0
    h = x[0]
    tables = _retention_tables(s)
    wr_t = w_router.T
    for l in range(DEPTH):
        p = _proj(h, _reorder_w_in(w_in[l]))
        y_sb = _stick_breaking(p)
        y_ret = _retention(p, ret_gn_g[l][None, :], tables)
        w_kv = w_kv_up[l].reshape(KV_RANK, DSA_HEADS, 2, HEAD_DIM)
        w_kv = jnp.concatenate([w_kv[:, :, 0, :].reshape(KV_RANK, D_DSA),
                                w_kv[:, :, 1, :].reshape(KV_RANK, D_DSA)], axis=1).astype(BF16)
        k_dsa, vt_dsa = _kv_up(p, kv_norm_g[l][None, :], w_kv)
        y_dsa = _dsa(p, k_dsa, vt_dsa)
        wo = w_o[l].astype(BF16)
        x1, gates, gates_t, cnt = _out_proj(y_sb, y_ret, y_dsa, wo, h, ln1_g[l][None, :], ln1_b[l][None, :], wr_t)
        h = _moe(x1, gates, gates_t, cnt, w_gate, w_up, w_down, ln2_g[l][None, :], ln2_b[l][None, :], l)
    return h[None]
```

```python
import functools

import numpy as np
import jax
import jax.numpy as jnp
from jax import lax
from jax.experimental import pallas as pl
from jax.experimental.pallas import tpu as pltpu

F32 = jnp.float32
BF16 = jnp.bfloat16
I32 = jnp.int32

D_MODEL = 2048
HEAD_DIM = 128
SB_HEADS = 6
RET_HEADS = 4
DSA_HEADS = 6
D_SB = SB_HEADS * HEAD_DIM
D_RET = RET_HEADS * HEAD_DIM
D_DSA = DSA_HEADS * HEAD_DIM
KV_RANK = 256
IDX_HEADS = 8
IDX_DIM = 64
IDX_SCALE = IDX_DIM ** -0.5 * IDX_HEADS ** -0.5
TOPK_MAX = 256
BLK = 128
N_EXPERTS = 16
N_GROUPS = 4
EXPERTS_PER_GROUP = N_EXPERTS // N_GROUPS
D_FF_EXPERT = 512
LN_EPS = 1e-5
RMS_EPS = 1e-6
GN_EPS = 1e-6
DEPTH = 2
DEEPNORM_ALPHA = (2 * DEPTH) ** 0.25

OFF_SB_Q = 0
OFF_SB_K = OFF_SB_Q + D_SB
OFF_SB_V = OFF_SB_K + D_SB
OFF_D_Q = OFF_SB_V + D_SB
OFF_R_Q = OFF_D_Q + D_DSA
OFF_R_K = OFF_R_Q + D_RET
OFF_R_V = OFF_R_K + D_RET
OFF_R_G = OFF_R_V + D_RET
OFF_D_QI = OFF_R_G + D_RET
OFF_D_CKV = OFF_D_QI + IDX_HEADS * IDX_DIM
OFF_TAIL = OFF_D_CKV + KV_RANK
D_PROJ = 6144

KEY_CHUNK = 512
MOE_TILE = 512
WIN = 16
ROW_TILE = 512
VMEM_LIMIT = 56 * 1024 * 1024
LOG2_E = 1.4426950408889634
NEG_BIG = -1e30
EXP_UNDERFLOW = -87.4
KEY_OF_NEG_INF = -2139095041
INT_MIN = -2147483648


def _dot(a, b):
    return jnp.dot(a, b, preferred_element_type=F32)


def _dot_nt(a, b):
    return lax.dot_general(a, b, (((1,), (1,)), ((), ())), preferred_element_type=F32)


def _dot_tn(a, b):
    return lax.dot_general(a, b, (((0,), (0,)), ((), ())), preferred_element_type=F32)


def _params(*sem):
    return pltpu.CompilerParams(dimension_semantics=sem, vmem_limit_bytes=VMEM_LIMIT)


def _resident(shape, index_map):
    return pl.BlockSpec(shape, index_map, pipeline_mode=pl.Buffered(1))


def _proj_kernel(x_ref, w_ref, o_ref):
    o_ref[...] = _dot(x_ref[...].astype(BF16), w_ref[0]).astype(o_ref.dtype)


def _proj(x, w, layer):
    s, d = x.shape
    n = w.shape[2]
    tm = min(1024, s)
    tn = 1536
    return pl.pallas_call(
        _proj_kernel,
        grid=(s // tm, n // tn),
        in_specs=[pl.BlockSpec((tm, d), lambda i, j: (i, 0)),
                  pl.BlockSpec((1, d, tn), lambda i, j: (layer, 0, j))],
        out_specs=pl.BlockSpec((tm, tn), lambda i, j: (i, j)),
        out_shape=jax.ShapeDtypeStruct((s, n), BF16),
        compiler_params=_params("parallel", "arbitrary"),
        name="proj",
    )(x, w)


def _sb_kernel(q_ref, k_ref, v_ref, o_ref, acc_ref):
    i = pl.program_id(0)
    q = q_ref[...]
    scale = HEAD_DIM ** -0.5
    key_pos = lax.broadcasted_iota(I32, (BLK, BLK), 0)
    qry_pos = lax.broadcasted_iota(I32, (BLK, BLK), 1)
    later = (qry_pos > key_pos).astype(BF16)

    def key_tile(j, first_query, cs):
        off = pl.multiple_of(j * BLK, BLK)
        strict = (off + key_pos) < (first_query + qry_pos)
        pvs, new_cs = [], []
        for h in range(SB_HEADS):
            hs = slice(h * HEAD_DIM, (h + 1) * HEAD_DIM)
            z = _dot_nt(k_ref[pl.ds(off, BLK), hs], q[:, hs]) * scale
            sp = jnp.maximum(z, 0.0) + jnp.log1p(jnp.exp(-jnp.abs(z)))
            log_rem = jnp.where(strict, -sp, 0.0)
            hi = log_rem.astype(BF16)
            lo = (log_rem - hi.astype(F32)).astype(BF16)
            after = _dot(later, hi) + _dot(later, lo)
            a = jnp.where(strict, jnp.exp(z - sp + after + cs[h]), 0.0)
            pvs.append(_dot_tn(a.astype(BF16), v_ref[pl.ds(off, BLK), hs]))
            new_cs.append(cs[h] + jnp.sum(log_rem, axis=0, keepdims=True))
        return pvs, tuple(new_cs)

    cs = tuple(jnp.zeros((1, BLK), F32) for _ in range(SB_HEADS))
    pv_a, cs = key_tile(i, i * BLK, cs)
    pv_b, cs = key_tile(jnp.maximum(i - 1, 0), jnp.where(i >= 1, i * BLK, -BLK), cs)
    for h in range(SB_HEADS):
        acc_ref[h] = pv_a[h] + pv_b[h]

    def cond(carry):
        j, cs = carry
        c_max = functools.reduce(jnp.maximum, cs)
        return jnp.logical_and(j >= 0, jnp.max(c_max) > EXP_UNDERFLOW)

    def body(carry):
        j, cs = carry
        pvs, cs = key_tile(j, i * BLK, cs)
        for h in range(SB_HEADS):
            acc_ref[h] += pvs[h]
        return j - 1, cs

    lax.while_loop(cond, body, (i - 2, cs))
    for h in range(SB_HEADS):
        o_ref[:, h * HEAD_DIM:(h + 1) * HEAD_DIM] = acc_ref[h].astype(o_ref.dtype)


def _stick_breaking(p):
    s = p.shape[0]
    nb = s // BLK
    return pl.pallas_call(
        _sb_kernel,
        grid=(nb,),
        in_specs=[pl.BlockSpec((BLK, D_SB), lambda i: (i, OFF_SB_Q // D_SB)),
                  _resident((s, D_SB), lambda i: (0, OFF_SB_K // D_SB)),
                  _resident((s, D_SB), lambda i: (0, OFF_SB_V // D_SB))],
        out_specs=pl.BlockSpec((BLK, D_SB), lambda i: (i, 0)),
        out_shape=jax.ShapeDtypeStruct((s, D_SB), BF16),
        scratch_shapes=[pltpu.VMEM((SB_HEADS, BLK, HEAD_DIM), F32)],
        compiler_params=_params("arbitrary"),
        name="stick_breaking",
    )(p, p, p)


def _ret_kernel(q_ref, k_ref, v_ref, g_ref, cos_ref, sin_ref, intra_ref, qd_ref, kd_ref, cd_ref,
                gn_ref, o_ref, state_ref):
    n = pl.program_id(0)

    @pl.when(n == 0)
    def _():
        state_ref[...] = jnp.zeros_like(state_ref)

    cos = cos_ref[...]
    sin = sin_ref[...]
    for h in range(RET_HEADS):
        hs = slice(h * HEAD_DIM, (h + 1) * HEAD_DIM)
        q = q_ref[:, hs].astype(F32)
        k = k_ref[:, hs].astype(F32)
        v = v_ref[:, hs]
        qr = q * cos + pltpu.roll(q, HEAD_DIM // 2, 1) * sin
        kr = (k * cos + pltpu.roll(k, HEAD_DIM // 2, 1) * sin) * (HEAD_DIM ** -0.5)
        scores = _dot_nt(qr.astype(BF16), kr.astype(BF16)) * intra_ref[h]
        state = state_ref[h]
        o = (_dot(scores.astype(BF16), v)
             + _dot((qr * qd_ref[h]).astype(BF16), state.astype(BF16)))
        state_ref[h] = cd_ref[h] * state + _dot_tn((kr * kd_ref[h]).astype(BF16), v)
        mu = jnp.mean(o, axis=1, keepdims=True)
        var = jnp.mean(jnp.square(o - mu), axis=1, keepdims=True)
        on = (o - mu) * lax.rsqrt(var + GN_EPS) * gn_ref[:, hs]
        g = g_ref[:, hs].astype(F32)
        o_ref[:, hs] = (g * jax.nn.sigmoid(g) * on).astype(o_ref.dtype)


def _retention_tables(s):
    f32 = np.float32
    half = HEAD_DIM // 2
    pos = np.arange(s, dtype=f32)
    theta = (f32(10000.0) ** (-np.linspace(0.0, 1.0, half, dtype=f32))).astype(f32)
    ang = (pos[:, None] * theta[None, :]).astype(f32)
    cos, sin = np.cos(ang).astype(f32), np.sin(ang).astype(f32)
    cos2 = np.concatenate([cos, cos], axis=1)
    sin2 = np.concatenate([-sin, sin], axis=1)
    log_gamma = np.log1p(-(f32(2.0) ** (-5.0 - np.arange(RET_HEADS, dtype=f32)))).astype(f32)
    idx = np.arange(BLK, dtype=f32)
    diff = idx[:, None] - idx[None, :]
    intra = np.where(diff >= 0, np.exp(np.maximum(diff, 0.0)[None] * log_gamma[:, None, None]), 0.0).astype(f32)
    q_decay = np.exp((idx[None, :] + 1.0) * log_gamma[:, None]).astype(f32)
    k_decay = np.exp((BLK - 1.0 - idx[None, :]) * log_gamma[:, None]).astype(f32)
    chunk_decay = np.exp(BLK * log_gamma).astype(f32)
    full = (RET_HEADS, BLK, HEAD_DIM)
    return tuple(jnp.asarray(np.ascontiguousarray(t)) for t in (
        cos2, sin2, intra,
        np.broadcast_to(q_decay[:, :, None], full),
        np.broadcast_to(k_decay[:, :, None], full),
        np.broadcast_to(chunk_decay[:, None, None], full)))


def _retention(p, gn_g, tables):
    s = p.shape[0]
    nc = s // BLK
    cos2, sin2, intra, qd, kd, cd = tables
    col = lambda off: pl.BlockSpec((BLK, D_RET), lambda n: (n, off // D_RET))
    per_head = pl.BlockSpec((RET_HEADS, BLK, HEAD_DIM), lambda n: (0, 0, 0))
    pos_spec = pl.BlockSpec((BLK, HEAD_DIM), lambda n: (n, 0))
    return pl.pallas_call(
        _ret_kernel,
        grid=(nc,),
        in_specs=[col(OFF_R_Q), col(OFF_R_K), col(OFF_R_V), col(OFF_R_G),
                  pos_spec, pos_spec, per_head, per_head, per_head, per_head,
                  pl.BlockSpec((1, D_RET), lambda n: (0, 0))],
        out_specs=pl.BlockSpec((BLK, D_RET), lambda n: (n, 0)),
        out_shape=jax.ShapeDtypeStruct((s, D_RET), BF16),
        scratch_shapes=[pltpu.VMEM((RET_HEADS, HEAD_DIM, HEAD_DIM), F32)],
        compiler_params=_params("arbitrary"),
        name="retention",
    )(p, p, p, p, cos2, sin2, intra, qd, kd, cd, gn_g)


def _kv_up_kernel(c_ref, g_ref, w_ref, k_ref, vt_ref):
    c = c_ref[...].astype(F32)
    y = c * lax.rsqrt(jnp.mean(jnp.square(c), axis=1, keepdims=True) + RMS_EPS) * g_ref[...]
    kv = _dot(y.astype(BF16), w_ref[...])
    k_ref[...] = (kv[:, :D_DSA] * (HEAD_DIM ** -0.5 * LOG2_E)).astype(k_ref.dtype)
    v_t = kv[:, D_DSA:].T
    ones = jnp.ones((V_ROWS - HEAD_DIM, v_t.shape[1]), vt_ref.dtype)
    for h in range(DSA_HEADS):
        vt_ref[0, h * V_ROWS:h * V_ROWS + HEAD_DIM, :] = v_t[h * HEAD_DIM:(h + 1) * HEAD_DIM, :].astype(vt_ref.dtype)
        vt_ref[0, h * V_ROWS + HEAD_DIM:(h + 1) * V_ROWS, :] = ones


def _kv_up(p, g, w):
    s = p.shape[0]
    n = w.shape[1]
    return pl.pallas_call(
        _kv_up_kernel,
        grid=(s // KEY_CHUNK,),
        in_specs=[pl.BlockSpec((KEY_CHUNK, KV_RANK), lambda i: (i, OFF_D_CKV // KV_RANK)),
                  pl.BlockSpec((1, KV_RANK), lambda i: (0, 0)),
                  pl.BlockSpec((KV_RANK, n), lambda i: (0, 0))],
        out_specs=[pl.BlockSpec((KEY_CHUNK, D_DSA), lambda i: (i, 0)),
                   pl.BlockSpec((1, DSA_HEADS * V_ROWS, KEY_CHUNK), lambda i: (i, 0, 0))],
        out_shape=[jax.ShapeDtypeStruct((s, D_DSA), BF16),
                   jax.ShapeDtypeStruct((s // KEY_CHUNK, DSA_HEADS * V_ROWS, KEY_CHUNK), BF16)],
        compiler_params=_params("parallel"),
        name="kv_up",
    )(p, g, w)


def _ordered_bits_to_float(u):
    return pltpu.bitcast(u ^ ((u >> 31) & 0x7FFFFFFF), F32)


V_ROWS = HEAD_DIM + 16
GROUPS_PER_CHUNK = KEY_CHUNK // (32 * 8)


def _bit_transpose32(words):
    a = list(words)
    j, mask = 16, 0x0000FFFF
    while j:
        k = 0
        while k < 32:
            t = (a[k] ^ (a[k + j] >> j)) & mask
            a[k] = a[k] ^ t
            a[k + j] = a[k + j] ^ (t << j)
            k = (k + j + 1) & ~j
        j >>= 1
        mask = (mask ^ (mask << j)) & 0xFFFFFFFF
    return a


def _chunks_of_block(blk):
    return ((blk + 1) * BLK + KEY_CHUNK - 1) // KEY_CHUNK


def _dsa_kernel(q_ref, qi_ref, tq_ref, qi_next_ref, tq_next_ref, tail_ref, k_ref, vt_ref, o_ref,
                score_ref, planes_ref, alive_ref, bias_ref, acc_ref, lg_a, lg_b, *, topk):
    i = pl.program_id(0)
    nxt = jnp.minimum(i + 1, pl.num_programs(0) - 1)
    n_chunks = _chunks_of_block(i)
    n_kc = score_ref.shape[0]

    def indexer(qi_blk_ref, tq_blk_ref, blk):
        w_t = tq_blk_ref[...].astype(F32).T[IDX_DIM:IDX_DIM + IDX_HEADS, :] * IDX_SCALE
        qi = qi_blk_ref[...]
        qi_rows = jnp.concatenate([qi[:, h * IDX_DIM:(h + 1) * IDX_DIM] for h in range(IDX_HEADS)], axis=0)

        def chunk_scores(c):
            off = pl.multiple_of(c * KEY_CHUNK, KEY_CHUNK)
            rel = jnp.maximum(_dot_nt(tail_ref[pl.ds(off, KEY_CHUNK), 0:IDX_DIM], qi_rows), 0.0)
            score = rel[:, 0:BLK] * w_t[0:1, :]
            for h in range(1, IDX_HEADS):
                score = score + rel[:, h * BLK:(h + 1) * BLK] * w_t[h:h + 1, :]
            return score

        def store_chunk(c, score):
            score = jnp.where(score == 0.0, 0.0, score)
            score_ref[c] = score
            bits = pltpu.bitcast(score, I32)
            u = bits ^ ((bits >> 31) | INT_MIN)
            for g in range(GROUPS_PER_CHUNK):
                words = [u[(g * 32 + j) * 8:(g * 32 + j + 1) * 8, :] for j in range(32)]
                planes = _bit_transpose32(words)
                for b in range(32):
                    planes_ref[c * GROUPS_PER_CHUNK + g, b] = planes[b]

        def full_chunk(c):
            store_chunk(c, chunk_scores(c))

        def last_chunk():
            last = _chunks_of_block(blk) - 1
            key_pos = last * KEY_CHUNK + lax.broadcasted_iota(I32, (KEY_CHUNK, BLK), 0)
            t_col = blk * BLK + lax.broadcasted_iota(I32, (KEY_CHUNK, BLK), 1)
            store_chunk(last, jnp.where(key_pos <= t_col, chunk_scores(last), -jnp.inf))

        return full_chunk, last_chunk

    @pl.when(i == 0)
    def _():
        indexer(qi_ref, tq_ref, 0)[1]()

    def init_alive(c, _):
        for g in range(GROUPS_PER_CHUNK):
            alive_ref[c * GROUPS_PER_CHUNK + g] = jnp.full((8, BLK), -1, I32)
        return 0

    lax.fori_loop(0, n_chunks, init_alive, 0)

    @pl.when(n_chunks % 2 == 1)
    def _():
        for g in range(GROUPS_PER_CHUNK):
            alive_ref[n_chunks * GROUPS_PER_CHUNK + g] = jnp.zeros((8, BLK), I32)
            planes_ref[n_chunks * GROUPS_PER_CHUNK + g] = jnp.zeros((32, 8, BLK), I32)

    def decide(above, t_bits, ones, bit):
        take = (above + ones) >= topk
        return (jnp.where(take, above, above + ones), jnp.where(take, t_bits | bit, t_bits),
                jnp.where(take, 0, -1))

    groups_per_step = 2 * GROUPS_PER_CHUNK

    def sweep(plane_prev, plane, drop_prev):
        def step(p, acc):
            for g in range(groups_per_step):
                gi = p * groups_per_step + g
                alive = alive_ref[gi]
                if plane_prev is not None:
                    alive = alive & (planes_ref[gi, plane_prev] ^ drop_prev)
                    alive_ref[gi] = alive
                if plane is not None:
                    alive = alive & planes_ref[gi, plane]
                acc = acc + lax.population_count(alive)
            return acc

        acc = lax.fori_loop(0, (n_chunks + 1) // 2, step, jnp.zeros((8, BLK), I32))
        return jnp.sum(acc, axis=0, keepdims=True)

    zero = jnp.zeros((1, BLK), I32)
    state = decide(zero, zero, sweep(None, 0, None), INT_MIN)

    def bit_step(b, state):
        above, t_bits, drop = state
        return decide(above, t_bits, sweep(b - 1, b, drop), jnp.int32(1) << (31 - b))

    above, t_bits, drop = lax.fori_loop(1, 32, bit_step, state)
    n_eq = sweep(31, None, drop)
    need = topk - above
    thr = t_bits ^ INT_MIN
    real = thr > KEY_OF_NEG_INF
    thr_f = jnp.where(real, _ordered_bits_to_float(thr), jnp.finfo(F32).min)

    def bias_chunk(c, _):
        bias_ref[c] = jnp.where(score_ref[c] >= thr_f, 0.0, NEG_BIG)
        return 0

    lax.fori_loop(0, n_chunks, bias_chunk, 0)

    has_tie = jnp.max(jnp.where(jnp.logical_and(real, n_eq > need), 1, 0)) > 0

    @pl.when(has_tie)
    def _():
        need_f = need.astype(F32)
        r = lax.broadcasted_iota(I32, (BLK, BLK), 0)
        cc = lax.broadcasted_iota(I32, (BLK, BLK), 1)
        upto = (cc <= r).astype(BF16)

        def tie_chunk(c, run):
            sc = score_ref[c]
            for u in range(KEY_CHUNK // BLK):
                st = sc[u * BLK:(u + 1) * BLK, :]
                eq = jnp.logical_and(st == thr_f, real)
                eqf = jnp.where(eq, 1.0, 0.0)
                rank = _dot(upto, eqf.astype(BF16)) + run
                sel = jnp.logical_or(st > thr_f, jnp.logical_and(eq, rank <= need_f))
                bias_ref[c, u * BLK:(u + 1) * BLK, :] = jnp.where(
                    real, jnp.where(sel, 0.0, NEG_BIG), jnp.where(st >= thr_f, 0.0, NEG_BIG))
                run = run + jnp.sum(eqf, axis=0, keepdims=True)
            return run

        lax.fori_loop(0, n_chunks, tie_chunk, jnp.zeros((1, BLK), F32))

    acc_ref[...] = jnp.zeros_like(acc_ref)
    bias_ref[n_kc] = jnp.full((KEY_CHUNK, BLK), NEG_BIG, F32)
    q = q_ref[...]

    def logits_into(lg, c):
        kc = jnp.minimum(c, n_chunks - 1)
        off = pl.multiple_of(kc * KEY_CHUNK, KEY_CHUNK)
        bias = bias_ref[jnp.where(c < n_chunks, c, n_kc)]
        for h in range(DSA_HEADS):
            hs = slice(h * HEAD_DIM, (h + 1) * HEAD_DIM)
            lg[h] = _dot_nt(k_ref[pl.ds(off, KEY_CHUNK), hs], q[:, hs]) + bias

    def reduce_from(lg, c, ms):
        vc = jnp.minimum(c, n_chunks - 1)
        new_m = []
        for h in range(DSA_HEADS):
            logits = lg[h]
            m_new = jnp.maximum(ms[h], jnp.max(logits, axis=0, keepdims=True))
            alpha = jnp.exp2(ms[h] - m_new)
            pr = jnp.exp2(logits - m_new).astype(BF16)
            acc_ref[h] = alpha * acc_ref[h] + _dot(vt_ref[vc, h * V_ROWS:(h + 1) * V_ROWS, :], pr)
            new_m.append(m_new)
        return tuple(new_m)

    next_full, next_last = indexer(qi_next_ref, tq_next_ref, nxt)
    last_full = _chunks_of_block(nxt) - 2
    n_pairs = (n_chunks + 1) // 2

    def attn_pair(pair, ms):
        c = 2 * pair
        logits_into(lg_b, c + 1)
        ms = reduce_from(lg_a, c, ms)
        next_full(jnp.maximum(jnp.minimum(c, last_full), 0))
        logits_into(lg_a, c + 2)
        ms = reduce_from(lg_b, c + 1, ms)
        next_full(jnp.maximum(jnp.minimum(c + 1, last_full), 0))
        return ms

    logits_into(lg_a, 0)
    lax.fori_loop(0, n_pairs, attn_pair, tuple(jnp.full((1, BLK), NEG_BIG, F32) for _ in range(DSA_HEADS)))
    for h in range(DSA_HEADS):
        weighted = acc_ref[h, 0:HEAD_DIM, :]
        denom = acc_ref[h, HEAD_DIM:HEAD_DIM + 1, :]
        o_ref[:, h * HEAD_DIM:(h + 1) * HEAD_DIM] = (weighted / denom).T.astype(o_ref.dtype)

    def rest(c, _):
        next_full(c)
        return 0

    lax.fori_loop(2 * n_pairs, last_full + 1, rest, 0)
    next_last()


def _dsa(p, k, vt):
    s = p.shape[0]
    nb = s // BLK
    topk = min(TOPK_MAX, s // 4)
    n_kc = s // KEY_CHUNK
    return pl.pallas_call(
        functools.partial(_dsa_kernel, topk=topk),
        grid=(nb,),
        in_specs=[pl.BlockSpec((BLK, D_DSA), lambda i: (i, OFF_D_Q // D_DSA)),
                  pl.BlockSpec((BLK, IDX_HEADS * IDX_DIM), lambda i: (i, OFF_D_QI // (IDX_HEADS * IDX_DIM))),
                  pl.BlockSpec((BLK, BLK), lambda i: (i, OFF_TAIL // BLK)),
                  pl.BlockSpec((BLK, IDX_HEADS * IDX_DIM),
                               lambda i: (jnp.minimum(i + 1, nb - 1), OFF_D_QI // (IDX_HEADS * IDX_DIM))),
                  pl.BlockSpec((BLK, BLK), lambda i: (jnp.minimum(i + 1, nb - 1), OFF_TAIL // BLK)),
                  _resident((s, BLK), lambda i: (0, OFF_TAIL // BLK)),
                  _resident((s, D_DSA), lambda i: (0, 0)),
                  _resident((n_kc, DSA_HEADS * V_ROWS, KEY_CHUNK), lambda i: (0, 0, 0))],
        out_specs=pl.BlockSpec((BLK, D_DSA), lambda i: (i, 0)),
        out_shape=jax.ShapeDtypeStruct((s, D_DSA), BF16),
        scratch_shapes=[pltpu.VMEM((n_kc, KEY_CHUNK, BLK), F32),
                        pltpu.VMEM((n_kc * GROUPS_PER_CHUNK, 32, 8, BLK), I32),
                        pltpu.VMEM((n_kc * GROUPS_PER_CHUNK, 8, BLK), I32),
                        pltpu.VMEM((n_kc + 1, KEY_CHUNK, BLK), F32),
                        pltpu.VMEM((DSA_HEADS, V_ROWS, BLK), F32),
                        pltpu.VMEM((DSA_HEADS, KEY_CHUNK, BLK), F32),
                        pltpu.VMEM((DSA_HEADS, KEY_CHUNK, BLK), F32)],
        compiler_params=_params("arbitrary"),
        name="dsa",
    )(p, p, p, p, p, p, k, vt)


def _layer_norm(r, g, b):
    mu = jnp.mean(r, axis=1, keepdims=True)
    var = jnp.mean(jnp.square(r - mu), axis=1, keepdims=True)
    return (r - mu) * lax.rsqrt(var + LN_EPS) * g + b


def _first_max_of4(vals):
    a, b, c, d = vals
    m = jnp.maximum(jnp.maximum(a, b), jnp.maximum(c, d))
    idx = jnp.where(a == m, 0, jnp.where(b == m, 1, jnp.where(c == m, 2, 3)))
    return m, idx


def _router_gates(logits_t):
    mx = jnp.max(logits_t, axis=0, keepdims=True)
    e = jnp.exp(logits_t - mx)
    probs = e / jnp.sum(e, axis=0, keepdims=True)
    rows = [probs[j:j + 1, :] for j in range(N_EXPERTS)]
    m1s, m2s, i1s, i2s, scores = [], [], [], [], []
    for g in range(N_GROUPS):
        vals = rows[g * EXPERTS_PER_GROUP:(g + 1) * EXPERTS_PER_GROUP]
        m1, i1 = _first_max_of4(vals)
        rest = [jnp.where(i1 == j, -1.0, vals[j]) for j in range(EXPERTS_PER_GROUP)]
        m2, i2 = _first_max_of4(rest)
        m1s.append(m1); m2s.append(m2); i1s.append(i1); i2s.append(i2); scores.append(m1 + m2)
    best, g_sel = _first_max_of4(scores)
    pick = lambda xs: jnp.where(g_sel == 0, xs[0], jnp.where(g_sel == 1, xs[1], jnp.where(g_sel == 2, xs[2], xs[3])))
    m1, m2, i1, i2 = pick(m1s), pick(m2s), pick(i1s), pick(i2s)
    den = m1 + m2
    w1, w2 = m1 / den, m2 / den
    e1 = g_sel * EXPERTS_PER_GROUP + i1
    e2 = g_sel * EXPERTS_PER_GROUP + i2
    gates = [jnp.where(e1 == j, w1, 0.0) + jnp.where(e2 == j, w2, 0.0) for j in range(N_EXPERTS)]
    return jnp.concatenate(gates, axis=0)


def _out_kernel(ysb_ref, yret_ref, ydsa_ref, wo_f32_ref, x_ref, g_ref, b_ref, wr_ref,
                x1_ref, gates_ref, gates_t_ref, cnt_ref, wo_ref):
    @pl.when(pl.program_id(0) == 0)
    def _():
        wo_ref[...] = wo_f32_ref[0].astype(BF16)

    mix = (_dot(ysb_ref[...], wo_ref[0:D_SB, :]) + _dot(yret_ref[...], wo_ref[D_SB:D_SB + D_RET, :])
           + _dot(ydsa_ref[...], wo_ref[D_SB + D_RET:, :]))
    x1 = _layer_norm(DEEPNORM_ALPHA * x_ref[...] + mix, g_ref[...], b_ref[...])
    x1_ref[...] = x1
    logits_t = lax.dot_general(wr_ref[...], x1, (((1,), (1,)), ((), ())),
                               preferred_element_type=F32, precision=lax.Precision.HIGHEST)
    gates_t = _router_gates(logits_t)
    tm = x1.shape[0]
    gates_t_ref[...] = gates_t
    padded = jnp.concatenate([gates_t, jnp.zeros((BLK - N_EXPERTS, tm), F32)], axis=0)
    gates_ref[...] = padded.T
    chosen = jnp.sum(jnp.where(gates_t > 0.0, 1, 0), axis=1, keepdims=True)
    cnt_ref[0] = jnp.broadcast_to(chosen, (N_EXPERTS, BLK))


def _out_proj(ysb, yret, ydsa, w_o, layer, x, g, b, wr_t):
    s = x.shape[0]
    tm = MOE_TILE
    row = lambda n: pl.BlockSpec((tm, n), lambda i: (i, 0))
    whole = lambda a: _resident(a.shape, lambda i: (0, 0))
    return pl.pallas_call(
        _out_kernel,
        grid=(s // tm,),
        in_specs=[row(D_SB), row(D_RET), row(D_DSA),
                  _resident((1,) + w_o.shape[1:], lambda i: (layer, 0, 0)),
                  row(D_MODEL), whole(g), whole(b), whole(wr_t)],
        out_specs=[row(D_MODEL), row(BLK), pl.BlockSpec((N_EXPERTS, tm), lambda i: (0, i)),
                   pl.BlockSpec((1, N_EXPERTS, BLK), lambda i: (i, 0, 0))],
        out_shape=[jax.ShapeDtypeStruct((s, D_MODEL), F32), jax.ShapeDtypeStruct((s, BLK), F32),
                   jax.ShapeDtypeStruct((N_EXPERTS, s), F32),
                   jax.ShapeDtypeStruct((s // tm, N_EXPERTS, BLK), I32)],
        scratch_shapes=[pltpu.VMEM(w_o.shape[1:], BF16)],
        compiler_params=_params("arbitrary"),
        name="out_proj",
    )(ysb, yret, ydsa, w_o, x, g, b, wr_t)


LOCAL_ROWS = 2 * MOE_TILE + N_EXPERTS * WIN


def _round_up(x, m):
    return (x + m - 1) // m * m


def _sorted_rows(s):
    n_tiles = s // MOE_TILE
    return _round_up(2 * s + n_tiles * N_EXPERTS * (WIN - 1) + N_EXPERTS * (ROW_TILE - 1), ROW_TILE)


def _moe_plan(cnt, s):
    n_row_tiles = _sorted_rows(s) // ROW_TILE
    seg = _round_up(cnt, WIN)
    rows_e = jnp.sum(seg, axis=0)
    region = _round_up(rows_e, ROW_TILE)
    region_off = jnp.cumsum(region) - region
    dest = region_off[None, :] + jnp.cumsum(seg, axis=0) - seg
    tiles_e = region // ROW_TILE
    tile_end = jnp.cumsum(tiles_e)
    k = jnp.arange(n_row_tiles, dtype=I32)
    tile_expert = jnp.minimum(jnp.sum((k[:, None] >= tile_end[None, :]).astype(I32), axis=1), N_EXPERTS - 1)
    first = (tile_end - tiles_e)[tile_expert]
    valid = jnp.clip(rows_e[tile_expert] - (k - first) * ROW_TILE, 0, ROW_TILE)
    tile_valid = jnp.where(k < tile_end[-1], valid, 0).astype(I32)
    fill = jnp.concatenate([region_off + rows_e, (region - rows_e) // WIN,
                            tile_end[-1:], n_row_tiles - tile_end[-1:]]).astype(I32)
    return cnt.reshape(-1).astype(I32), dest.reshape(-1).astype(I32), fill, tile_expert, tile_valid


def _slot_offsets(cnt_sm, tile):
    offs, o = [], 0
    for e in range(N_EXPERTS):
        offs.append(o)
        o = o + _round_up(cnt_sm[tile * N_EXPERTS + e], WIN)
    return offs


def _window_copies(cnt_sm, dest_sm, tile, offs, local_ref, sorted_hbm, sem, to_sorted):
    total = 0
    for e in range(N_EXPERTS):
        n_win = (cnt_sm[tile * N_EXPERTS + e] + WIN - 1) // WIN
        base_local, base_sorted = offs[e], dest_sm[tile * N_EXPERTS + e]

        def issue(j, _, base_local=base_local, base_sorted=base_sorted):
            loc = local_ref.at[pl.ds(pl.multiple_of(base_local + j * WIN, WIN), WIN)]
            srt = sorted_hbm.at[pl.ds(pl.multiple_of(base_sorted + j * WIN, WIN), WIN)]
            if to_sorted:
                pltpu.make_async_copy(loc, srt, sem).start()
            else:
                pltpu.make_async_copy(srt, loc, sem).start()
            return 0

        lax.fori_loop(0, n_win, issue, 0)
        total = total + n_win
    return total


def _n_windows(cnt_sm, tile):
    total = 0
    for e in range(N_EXPERTS):
        total = total + (cnt_sm[tile * N_EXPERTS + e] + WIN - 1) // WIN
    return total


def _wait_windows(total, local_ref, sorted_hbm, sem):
    def wait(j, _):
        pltpu.make_async_copy(local_ref.at[pl.ds(0, WIN)], sorted_hbm.at[pl.ds(0, WIN)], sem).wait()
        return 0

    lax.fori_loop(0, total, wait, 0)


def _zero_fill(fill_sm, xs_hbm, zero_ref, sem_win, sem_tile):
    zero_ref[...] = jnp.zeros_like(zero_ref)
    win_copy = lambda row: pltpu.make_async_copy(
        zero_ref.at[pl.ds(0, WIN)], xs_hbm.at[pl.ds(pl.multiple_of(row, WIN), WIN)], sem_win)
    tile_copy = lambda row: pltpu.make_async_copy(
        zero_ref, xs_hbm.at[pl.ds(pl.multiple_of(row, ROW_TILE), ROW_TILE)], sem_tile)
    n_pad = 0
    for e in range(N_EXPERTS):
        first, n_win = fill_sm[e], fill_sm[N_EXPERTS + e]

        def issue(j, _, first=first):
            win_copy(first + j * WIN).start()
            return 0

        lax.fori_loop(0, n_win, issue, 0)
        n_pad = n_pad + n_win
    first_tile, n_tail = fill_sm[2 * N_EXPERTS], fill_sm[2 * N_EXPERTS + 1]

    def issue_tile(j, _):
        tile_copy((first_tile + j) * ROW_TILE).start()
        return 0

    def wait_win(j, _):
        win_copy(0).wait()
        return 0

    def wait_tile(j, _):
        tile_copy(0).wait()
        return 0

    lax.fori_loop(0, n_tail, issue_tile, 0)
    lax.fori_loop(0, n_pad, wait_win, 0)
    lax.fori_loop(0, n_tail, wait_tile, 0)


def _dispatch_kernel(cnt_sm, dest_sm, fill_sm, x_ref, gt_ref, xs_hbm, local_ref, zero_ref, sem, sem_win, sem_tile):
    tile = pl.program_id(0)

    @pl.when(tile == 0)
    def _():
        _zero_fill(fill_sm, xs_hbm, zero_ref, sem_win, sem_tile)

    offs = _slot_offsets(cnt_sm, tile)
    chosen = gt_ref[...] > 0.0
    t_r = lax.broadcasted_iota(I32, (MOE_TILE, MOE_TILE), 0)
    t_c = lax.broadcasted_iota(I32, (MOE_TILE, MOE_TILE), 1)
    earlier = _dot(jnp.where(chosen, 1.0, 0.0).astype(BF16), (t_r < t_c).astype(BF16))
    e_id = lax.broadcasted_iota(I32, (N_EXPERTS, 1), 0)
    slot = jnp.zeros((N_EXPERTS, 1), I32)
    for e in range(N_EXPERTS):
        slot = jnp.where(e_id == e, offs[e], slot)
    pos = earlier + slot.astype(F32)
    p_lo = jnp.min(jnp.where(chosen, pos, float(LOCAL_ROWS)), axis=0, keepdims=True).astype(I32)
    p_hi = jnp.max(jnp.where(chosen, pos, -1.0), axis=0, keepdims=True).astype(I32)
    row = lax.broadcasted_iota(I32, (LOCAL_ROWS, MOE_TILE), 0)
    onehot = jnp.where(row == p_lo, 1.0, jnp.where(row == p_hi, 1.0, 0.0)).astype(BF16)
    buf = tile % 2
    mine, mine_sem = local_ref.at[buf], sem.at[buf]
    other, other_sem = local_ref.at[1 - buf], sem.at[1 - buf]
    mine[...] = _dot(onehot, x_ref[...].astype(BF16)).astype(BF16)
    total = _window_copies(cnt_sm, dest_sm, tile, offs, mine, xs_hbm, mine_sem, True)

    @pl.when(tile >= 1)
    def _():
        _wait_windows(_n_windows(cnt_sm, tile - 1), other, xs_hbm, other_sem)

    @pl.when(tile == pl.num_programs(0) - 1)
    def _():
        _wait_windows(total, mine, xs_hbm, mine_sem)


def _dispatch(x1, gates_t, cnt_flat, dest_flat, fill):
    s = x1.shape[0]
    return pl.pallas_call(
        _dispatch_kernel,
        grid_spec=pltpu.PrefetchScalarGridSpec(
            num_scalar_prefetch=3,
            grid=(s // MOE_TILE,),
            in_specs=[pl.BlockSpec((MOE_TILE, D_MODEL), lambda i, *_: (i, 0)),
                      pl.BlockSpec((N_EXPERTS, MOE_TILE), lambda i, *_: (0, i))],
            out_specs=pl.BlockSpec(memory_space=pl.ANY),
            scratch_shapes=[pltpu.VMEM((2, LOCAL_ROWS, D_MODEL), BF16), pltpu.VMEM((ROW_TILE, D_MODEL), BF16),
                            pltpu.SemaphoreType.DMA((2,)), pltpu.SemaphoreType.DMA(()),
                            pltpu.SemaphoreType.DMA(())]),
        out_shape=jax.ShapeDtypeStruct((_sorted_rows(s), D_MODEL), BF16),
        compiler_params=_params("arbitrary"),
        name="moe_dispatch",
    )(cnt_flat, dest_flat, fill, x1, gates_t)


def _expert_kernel(te_sm, tv_sm, xs_ref, wg_ref, wu_ref, wd_ref, y_ref, wg_b, wu_b, wd_b):
    k = pl.program_id(0)
    valid = tv_sm[k]
    new_expert = jnp.logical_or(k == 0, te_sm[k] != te_sm[jnp.maximum(k - 1, 0)])

    @pl.when(jnp.logical_and(valid > 0, new_expert))
    def _():
        wg_b[...] = wg_ref[0, 0].astype(BF16)
        wu_b[...] = wu_ref[0, 0].astype(BF16)
        wd_b[...] = wd_ref[0, 0].astype(BF16)

    @pl.when(valid > 0)
    def _():
        x = xs_ref[...]
        hg = _dot(x, wg_b[...])
        hu = _dot(x, wu_b[...])
        act = hg * jax.nn.sigmoid(hg) * hu
        y_ref[...] = _dot(act.astype(BF16), wd_b[...]).astype(y_ref.dtype)

    @pl.when(valid == 0)
    def _():
        y_ref[...] = jnp.zeros_like(y_ref)


def _experts(xs, tile_expert, tile_valid, w_gate, w_up, w_down, layer):
    rows = xs.shape[0]
    w_in_spec = pl.BlockSpec((1, 1, D_MODEL, D_FF_EXPERT), lambda k, te, tv: (layer, te[k], 0, 0))
    return pl.pallas_call(
        _expert_kernel,
        grid_spec=pltpu.PrefetchScalarGridSpec(
            num_scalar_prefetch=2,
            grid=(rows // ROW_TILE,),
            in_specs=[pl.BlockSpec((ROW_TILE, D_MODEL), lambda k, te, tv: (k, 0)),
                      w_in_spec, w_in_spec,
                      pl.BlockSpec((1, 1, D_FF_EXPERT, D_MODEL), lambda k, te, tv: (layer, te[k], 0, 0))],
            out_specs=pl.BlockSpec((ROW_TILE, D_MODEL), lambda k, te, tv: (k, 0)),
            scratch_shapes=[pltpu.VMEM((D_MODEL, D_FF_EXPERT), BF16), pltpu.VMEM((D_MODEL, D_FF_EXPERT), BF16),
                            pltpu.VMEM((D_FF_EXPERT, D_MODEL), BF16)]),
        out_shape=jax.ShapeDtypeStruct((rows, D_MODEL), BF16),
        compiler_params=_params("arbitrary"),
        name="moe_experts",
    )(tile_expert, tile_valid, xs, w_gate, w_up, w_down)


def _combine_kernel(cnt_sm, dest_sm, x_ref, gates_ref, y_hbm, g_ref, b_ref, o_ref, local_ref, sem):
    tile = pl.program_id(0)

    buf = tile % 2
    mine, mine_sem = local_ref.at[buf], sem.at[buf]
    offs = _slot_offsets(cnt_sm, tile)

    @pl.when(tile == 0)
    def _():
        local_ref[...] = jnp.zeros_like(local_ref)
        _window_copies(cnt_sm, dest_sm, tile, offs, mine, y_hbm, mine_sem, False)

    @pl.when(tile + 1 < pl.num_programs(0))
    def _():
        _window_copies(cnt_sm, dest_sm, tile + 1, _slot_offsets(cnt_sm, tile + 1),
                       local_ref.at[1 - buf], y_hbm, sem.at[1 - buf], False)

    gates = gates_ref[...]
    chosen = gates > 0.0
    t_r = lax.broadcasted_iota(I32, (MOE_TILE, MOE_TILE), 0)
    t_c = lax.broadcasted_iota(I32, (MOE_TILE, MOE_TILE), 1)
    earlier = _dot((t_c < t_r).astype(BF16), jnp.where(chosen, 1.0, 0.0).astype(BF16))
    e_id = lax.broadcasted_iota(I32, (1, BLK), 1)
    slot = jnp.zeros((1, BLK), I32)
    for e in range(N_EXPERTS):
        slot = jnp.where(e_id == e, offs[e], slot)
    pos = jnp.where(chosen, earlier + slot.astype(F32), -1.0)
    p_lo = jnp.min(jnp.where(chosen, pos, float(LOCAL_ROWS)), axis=1, keepdims=True)
    p_hi = jnp.max(pos, axis=1, keepdims=True)
    w_lo = jnp.sum(jnp.where(pos == p_lo, gates, 0.0), axis=1, keepdims=True)
    w_hi = jnp.sum(jnp.where(pos == p_hi, gates, 0.0), axis=1, keepdims=True)
    col = lax.broadcasted_iota(I32, (MOE_TILE, LOCAL_ROWS), 1)
    weights = jnp.where(col == p_lo.astype(I32), w_lo, jnp.where(col == p_hi.astype(I32), w_hi, 0.0))

    _wait_windows(_n_windows(cnt_sm, tile), mine, y_hbm, mine_sem)
    ffn = _dot(weights.astype(BF16), mine[...])
    o_ref[...] = _layer_norm(DEEPNORM_ALPHA * x_ref[...] + ffn, g_ref[...], b_ref[...])


def _combine(x1, gates, y, cnt_flat, dest_flat, g, b):
    s = x1.shape[0]
    row = lambda n: pl.BlockSpec((MOE_TILE, n), lambda i, *_: (i, 0))
    vec = pl.BlockSpec((1, D_MODEL), lambda i, *_: (0, 0))
    return pl.pallas_call(
        _combine_kernel,
        grid_spec=pltpu.PrefetchScalarGridSpec(
            num_scalar_prefetch=2,
            grid=(s // MOE_TILE,),
            in_specs=[row(D_MODEL), row(BLK), pl.BlockSpec(memory_space=pl.ANY), vec, vec],
            out_specs=row(D_MODEL),
            scratch_shapes=[pltpu.VMEM((2, LOCAL_ROWS, D_MODEL), BF16), pltpu.SemaphoreType.DMA((2,))]),
        out_shape=jax.ShapeDtypeStruct((s, D_MODEL), F32),
        compiler_params=_params("arbitrary"),
        name="moe_combine",
    )(cnt_flat, dest_flat, x1, gates, y, g, b)


def _moe(x1, gates, gates_t, cnt, w_gate, w_up, w_down, g, b, layer):
    s = x1.shape[0]
    cnt_flat, dest_flat, fill, tile_expert, tile_valid = _moe_plan(cnt[:, :, 0], s)
    xs = _dispatch(x1, gates_t, cnt_flat, dest_flat, fill)
    y = _experts(xs, tile_expert, tile_valid, w_gate, w_up, w_down, layer)
    return _combine(x1, gates, y, cnt_flat, dest_flat, g, b)


COL_BLOCK = 256
D_IN = 3 * D_SB + 4 * D_RET + D_DSA + KV_RANK + IDX_HEADS * IDX_DIM + IDX_DIM + IDX_HEADS


def _column_block_order():
    src = {}
    o = 0
    for name, width in (("sb", 3 * D_SB), ("ret", 4 * D_RET), ("d_q", D_DSA), ("d_ckv", KV_RANK),
                        ("d_qi", IDX_HEADS * IDX_DIM), ("tail", COL_BLOCK)):
        src[name] = list(range(o // COL_BLOCK, (o + width) // COL_BLOCK))
        o += width
    order = src["sb"] + src["d_q"] + src["ret"] + src["d_qi"] + src["d_ckv"] + src["tail"]
    assert len(order) == D_PROJ // COL_BLOCK
    return np.asarray(order, np.int32)


def _reorder_kernel(order_sm, w_ref, o_ref):
    j = pl.program_id(1)
    col = order_sm[j] * COL_BLOCK + lax.broadcasted_iota(I32, (1, COL_BLOCK), 1)
    o_ref[0] = jnp.where(col < D_IN, w_ref[0], 0.0).astype(o_ref.dtype)


def _reorder_w_in(w_in):
    depth, d, _ = w_in.shape
    return pl.pallas_call(
        _reorder_kernel,
        grid_spec=pltpu.PrefetchScalarGridSpec(
            num_scalar_prefetch=1,
            grid=(depth, D_PROJ // COL_BLOCK),
            in_specs=[pl.BlockSpec((1, d, COL_BLOCK), lambda l, j, order: (l, 0, order[j]))],
            out_specs=pl.BlockSpec((1, d, COL_BLOCK), lambda l, j, order: (l, 0, j))),
        out_shape=jax.ShapeDtypeStruct((depth, d, D_PROJ), BF16),
        compiler_params=_params("parallel", "arbitrary"),
        name="reorder_w_in",
    )(jnp.asarray(_column_block_order()), w_in)


def kernel(x, w_in, w_kv_up, kv_norm_g, ret_gn_g, w_o, ln1_g, ln1_b, w_router, w_gate, w_up, w_down,
           ln2_g, ln2_b):
    b, s, _ = x.shape
    assert b == 1 and s % KEY_CHUNK == 0
    h = x[0]
    tables = _retention_tables(s)
    wr_t = w_router.T
    w_proj = _reorder_w_in(w_in)
    for l in range(DEPTH):
        p = _proj(h, w_proj, l)
        y_sb = _stick_breaking(p)
        y_ret = _retention(p, ret_gn_g[l][None, :], tables)
        w_kv = w_kv_up[l].reshape(KV_RANK, DSA_HEADS, 2, HEAD_DIM)
        w_kv = jnp.concatenate([w_kv[:, :, 0, :].reshape(KV_RANK, D_DSA),
                                w_kv[:, :, 1, :].reshape(KV_RANK, D_DSA)], axis=1).astype(BF16)
        k_dsa, vt_dsa = _kv_up(p, kv_norm_g[l][None, :], w_kv)
        y_dsa = _dsa(p, k_dsa, vt_dsa)
        x1, gates, gates_t, cnt = _out_proj(y_sb, y_ret, y_dsa, w_o, l, h, ln1_g[l][None, :], ln1_b[l][None, :],
                                            wr_t)
        h = _moe(x1, gates, gates_t, cnt, w_gate, w_up, w_down, ln2_g[l][None, :], ln2_b[l][None, :], l)
    return h[None]
```

```python
import functools

import numpy as np
import jax
import jax.numpy as jnp
from jax import lax
from jax.experimental import pallas as pl
from jax.experimental.pallas import tpu as pltpu

F32 = jnp.float32
BF16 = jnp.bfloat16
I32 = jnp.int32

D_MODEL = 2048
HEAD_DIM = 128
SB_HEADS = 6
RET_HEADS = 4
DSA_HEADS = 6
D_SB = SB_HEADS * HEAD_DIM
D_RET = RET_HEADS * HEAD_DIM
D_DSA = DSA_HEADS * HEAD_DIM
KV_RANK = 256
IDX_HEADS = 8
IDX_DIM = 64
IDX_SCALE = IDX_DIM ** -0.5 * IDX_HEADS ** -0.5
TOPK_MAX = 256
BLK = 128
N_EXPERTS = 16
N_GROUPS = 4
EXPERTS_PER_GROUP = N_EXPERTS // N_GROUPS
D_FF_EXPERT = 512
LN_EPS = 1e-5
RMS_EPS = 1e-6
GN_EPS = 1e-6
DEPTH = 2
DEEPNORM_ALPHA = (2 * DEPTH) ** 0.25

OFF_SB_Q = 0
OFF_SB_K = OFF_SB_Q + D_SB
OFF_SB_V = OFF_SB_K + D_SB
OFF_D_Q = OFF_SB_V + D_SB
OFF_R_Q = OFF_D_Q + D_DSA
OFF_R_K = OFF_R_Q + D_RET
OFF_R_V = OFF_R_K + D_RET
OFF_R_G = OFF_R_V + D_RET
OFF_D_QI = OFF_R_G + D_RET
OFF_D_CKV = OFF_D_QI + IDX_HEADS * IDX_DIM
OFF_TAIL = OFF_D_CKV + KV_RANK
D_PROJ = 6144

KEY_CHUNK = 512
MOE_TILE = 512
WIN = 16
ROW_TILE = 512
VMEM_LIMIT = 56 * 1024 * 1024
LOG2_E = 1.4426950408889634
NEG_BIG = -1e30
EXP_UNDERFLOW = -87.4
KEY_OF_NEG_INF = -2139095041
INT_MIN = -2147483648


def _dot(a, b):
    return jnp.dot(a, b, preferred_element_type=F32)


def _dot_nt(a, b):
    return lax.dot_general(a, b, (((1,), (1,)), ((), ())), preferred_element_type=F32)


def _dot_tn(a, b):
    return lax.dot_general(a, b, (((0,), (0,)), ((), ())), preferred_element_type=F32)


def _params(*sem):
    return pltpu.CompilerParams(dimension_semantics=sem, vmem_limit_bytes=VMEM_LIMIT)


def _resident(shape, index_map):
    return pl.BlockSpec(shape, index_map, pipeline_mode=pl.Buffered(1))


def _proj_kernel(x_ref, w_ref, o_ref):
    o_ref[...] = _dot(x_ref[...].astype(BF16), w_ref[0]).astype(o_ref.dtype)


def _proj(x, w, layer):
    s, d = x.shape
    n = w.shape[2]
    tm = min(1024, s)
    tn = 1536
    return pl.pallas_call(
        _proj_kernel,
        grid=(s // tm, n // tn),
        in_specs=[pl.BlockSpec((tm, d), lambda i, j: (i, 0)),
                  pl.BlockSpec((1, d, tn), lambda i, j: (layer, 0, j))],
        out_specs=pl.BlockSpec((tm, tn), lambda i, j: (i, j)),
        out_shape=jax.ShapeDtypeStruct((s, n), BF16),
        compiler_params=_params("parallel", "arbitrary"),
        name="proj",
    )(x, w)


def _sb_kernel(q_ref, k_ref, v_ref, o_ref, acc_ref):
    i = pl.program_id(0)
    q = q_ref[...]
    scale = HEAD_DIM ** -0.5
    key_pos = lax.broadcasted_iota(I32, (BLK, BLK), 0)
    qry_pos = lax.broadcasted_iota(I32, (BLK, BLK), 1)
    later = (qry_pos > key_pos).astype(BF16)

    def key_tile(j, first_query, cs):
        off = pl.multiple_of(j * BLK, BLK)
        strict = (off + key_pos) < (first_query + qry_pos)
        pvs, new_cs = [], []
        for h in range(SB_HEADS):
            hs = slice(h * HEAD_DIM, (h + 1) * HEAD_DIM)
            z = _dot_nt(k_ref[pl.ds(off, BLK), hs], q[:, hs]) * scale
            sp = jnp.maximum(z, 0.0) + jnp.log1p(jnp.exp(-jnp.abs(z)))
            log_rem = jnp.where(strict, -sp, 0.0)
            hi = log_rem.astype(BF16)
            lo = (log_rem - hi.astype(F32)).astype(BF16)
            after = _dot(later, hi) + _dot(later, lo)
            a = jnp.where(strict, jnp.exp(z - sp + after + cs[h]), 0.0)
            pvs.append(_dot_tn(a.astype(BF16), v_ref[pl.ds(off, BLK), hs]))
            new_cs.append(cs[h] + jnp.sum(log_rem, axis=0, keepdims=True))
        return pvs, tuple(new_cs)

    cs = tuple(jnp.zeros((1, BLK), F32) for _ in range(SB_HEADS))
    pv_a, cs = key_tile(i, i * BLK, cs)
    pv_b, cs = key_tile(jnp.maximum(i - 1, 0), jnp.where(i >= 1, i * BLK, -BLK), cs)
    for h in range(SB_HEADS):
        acc_ref[h] = pv_a[h] + pv_b[h]

    def cond(carry):
        j, cs = carry
        c_max = functools.reduce(jnp.maximum, cs)
        return jnp.logical_and(j >= 0, jnp.max(c_max) > EXP_UNDERFLOW)

    def body(carry):
        j, cs = carry
        pvs, cs = key_tile(j, i * BLK, cs)
        for h in range(SB_HEADS):
            acc_ref[h] += pvs[h]
        return j - 1, cs

    lax.while_loop(cond, body, (i - 2, cs))
    for h in range(SB_HEADS):
        o_ref[:, h * HEAD_DIM:(h + 1) * HEAD_DIM] = acc_ref[h].astype(o_ref.dtype)


def _stick_breaking(p):
    s = p.shape[0]
    nb = s // BLK
    return pl.pallas_call(
        _sb_kernel,
        grid=(nb,),
        in_specs=[pl.BlockSpec((BLK, D_SB), lambda i: (i, OFF_SB_Q // D_SB)),
                  _resident((s, D_SB), lambda i: (0, OFF_SB_K // D_SB)),
                  _resident((s, D_SB), lambda i: (0, OFF_SB_V // D_SB))],
        out_specs=pl.BlockSpec((BLK, D_SB), lambda i: (i, 0)),
        out_shape=jax.ShapeDtypeStruct((s, D_SB), BF16),
        scratch_shapes=[pltpu.VMEM((SB_HEADS, BLK, HEAD_DIM), F32)],
        compiler_params=_params("arbitrary"),
        name="stick_breaking",
    )(p, p, p)


def _ret_kernel(q_ref, k_ref, v_ref, g_ref, cos_ref, sin_ref, intra_ref, qd_ref, kd_ref, cd_ref,
                gn_ref, o_ref, state_ref):
    n = pl.program_id(0)

    @pl.when(n == 0)
    def _():
        state_ref[...] = jnp.zeros_like(state_ref)

    cos = cos_ref[...]
    sin = sin_ref[...]
    for h in range(RET_HEADS):
        hs = slice(h * HEAD_DIM, (h + 1) * HEAD_DIM)
        q = q_ref[:, hs].astype(F32)
        k = k_ref[:, hs].astype(F32)
        v = v_ref[:, hs]
        half = HEAD_DIM // 2
        swap = lambda t: jnp.concatenate([t[:, half:], t[:, :half]], axis=1)
        qr = q * cos + swap(q) * sin
        kr = (k * cos + swap(k) * sin) * (HEAD_DIM ** -0.5)
        scores = _dot_nt(qr.astype(BF16), kr.astype(BF16)) * intra_ref[h]
        state = state_ref[h]
        o = (_dot(scores.astype(BF16), v)
             + _dot((qr * qd_ref[h]).astype(BF16), state.astype(BF16)))
        state_ref[h] = cd_ref[h] * state + _dot_tn((kr * kd_ref[h]).astype(BF16), v)
        mu = jnp.mean(o, axis=1, keepdims=True)
        var = jnp.mean(jnp.square(o - mu), axis=1, keepdims=True)
        on = (o - mu) * lax.rsqrt(var + GN_EPS) * gn_ref[:, hs]
        g = g_ref[:, hs].astype(F32)
        o_ref[:, hs] = (g * jax.nn.sigmoid(g) * on).astype(o_ref.dtype)


def _retention_tables(s):
    f32 = np.float32
    half = HEAD_DIM // 2
    pos = np.arange(s, dtype=np.float64)
    theta = 10000.0 ** (-np.linspace(0.0, 1.0, half))
    ang = pos[:, None] * theta[None, :]
    cos, sin = np.cos(ang).astype(f32), np.sin(ang).astype(f32)
    cos2 = np.concatenate([cos, cos], axis=1)
    sin2 = np.concatenate([-sin, sin], axis=1)
    log_gamma = np.log1p(-(2.0 ** (-5.0 - np.arange(RET_HEADS, dtype=np.float64))))
    idx = np.arange(BLK, dtype=np.float64)
    diff = idx[:, None] - idx[None, :]
    intra = np.where(diff >= 0, np.exp(np.maximum(diff, 0.0)[None] * log_gamma[:, None, None]), 0.0).astype(f32)
    q_decay = np.exp((idx[None, :] + 1.0) * log_gamma[:, None]).astype(f32)
    k_decay = np.exp((BLK - 1.0 - idx[None, :]) * log_gamma[:, None]).astype(f32)
    chunk_decay = np.exp(BLK * log_gamma).astype(f32)
    full = (RET_HEADS, BLK, HEAD_DIM)
    return tuple(jnp.asarray(np.ascontiguousarray(t)) for t in (
        cos2, sin2, intra,
        np.broadcast_to(q_decay[:, :, None], full),
        np.broadcast_to(k_decay[:, :, None], full),
        np.broadcast_to(chunk_decay[:, None, None], full)))


def _retention(p, gn_g, tables):
    s = p.shape[0]
    nc = s // BLK
    cos2, sin2, intra, qd, kd, cd = tables
    col = lambda off: pl.BlockSpec((BLK, D_RET), lambda n: (n, off // D_RET))
    per_head = pl.BlockSpec((RET_HEADS, BLK, HEAD_DIM), lambda n: (0, 0, 0))
    pos_spec = pl.BlockSpec((BLK, HEAD_DIM), lambda n: (n, 0))
    return pl.pallas_call(
        _ret_kernel,
        grid=(nc,),
        in_specs=[col(OFF_R_Q), col(OFF_R_K), col(OFF_R_V), col(OFF_R_G),
                  pos_spec, pos_spec, per_head, per_head, per_head, per_head,
                  pl.BlockSpec((1, D_RET), lambda n: (0, 0))],
        out_specs=pl.BlockSpec((BLK, D_RET), lambda n: (n, 0)),
        out_shape=jax.ShapeDtypeStruct((s, D_RET), BF16),
        scratch_shapes=[pltpu.VMEM((RET_HEADS, HEAD_DIM, HEAD_DIM), F32)],
        compiler_params=_params("arbitrary"),
        name="retention",
    )(p, p, p, p, cos2, sin2, intra, qd, kd, cd, gn_g)


def _kv_up_kernel(c_ref, g_ref, w_ref, k_ref, vt_ref):
    c = c_ref[...].astype(F32)
    y = c * lax.rsqrt(jnp.mean(jnp.square(c), axis=1, keepdims=True) + RMS_EPS) * g_ref[...]
    kv = _dot(y.astype(BF16), w_ref[...])
    k_ref[...] = (kv[:, :D_DSA] * (HEAD_DIM ** -0.5 * LOG2_E)).astype(k_ref.dtype)
    v_t = kv[:, D_DSA:].T
    ones = jnp.ones((V_ROWS - HEAD_DIM, v_t.shape[1]), vt_ref.dtype)
    for h in range(DSA_HEADS):
        vt_ref[0, h * V_ROWS:h * V_ROWS + HEAD_DIM, :] = v_t[h * HEAD_DIM:(h + 1) * HEAD_DIM, :].astype(vt_ref.dtype)
        vt_ref[0, h * V_ROWS + HEAD_DIM:(h + 1) * V_ROWS, :] = ones


def _kv_up(p, g, w):
    s = p.shape[0]
    n = w.shape[1]
    return pl.pallas_call(
        _kv_up_kernel,
        grid=(s // KEY_CHUNK,),
        in_specs=[pl.BlockSpec((KEY_CHUNK, KV_RANK), lambda i: (i, OFF_D_CKV // KV_RANK)),
                  pl.BlockSpec((1, KV_RANK), lambda i: (0, 0)),
                  pl.BlockSpec((KV_RANK, n), lambda i: (0, 0))],
        out_specs=[pl.BlockSpec((KEY_CHUNK, D_DSA), lambda i: (i, 0)),
                   pl.BlockSpec((1, DSA_HEADS * V_ROWS, KEY_CHUNK), lambda i: (i, 0, 0))],
        out_shape=[jax.ShapeDtypeStruct((s, D_DSA), BF16),
                   jax.ShapeDtypeStruct((s // KEY_CHUNK, DSA_HEADS * V_ROWS, KEY_CHUNK), BF16)],
        compiler_params=_params("parallel"),
        name="kv_up",
    )(p, g, w)


def _ordered_bits_to_float(u):
    return pltpu.bitcast(u ^ ((u >> 31) & 0x7FFFFFFF), F32)


V_ROWS = HEAD_DIM + 16
GROUPS_PER_CHUNK = KEY_CHUNK // (32 * 8)


def _bit_transpose32(words):
    a = list(words)
    j, mask = 16, 0x0000FFFF
    while j:
        k = 0
        while k < 32:
            t = (a[k] ^ (a[k + j] >> j)) & mask
            a[k] = a[k] ^ t
            a[k + j] = a[k + j] ^ (t << j)
            k = (k + j + 1) & ~j
        j >>= 1
        mask = (mask ^ (mask << j)) & 0xFFFFFFFF
    return a


def _chunks_of_block(blk):
    return ((blk + 1) * BLK + KEY_CHUNK - 1) // KEY_CHUNK


def _dsa_kernel(q_ref, qi_ref, tq_ref, qi_next_ref, tq_next_ref, tail_ref, k_ref, vt_ref, o_ref,
                score_ref, planes_ref, alive_ref, bias_ref, acc_ref, lg_a, lg_b, sel_ref, *, topk):
    i = pl.program_id(0)
    nxt = jnp.minimum(i + 1, pl.num_programs(0) - 1)
    n_chunks = _chunks_of_block(i)
    n_kc = score_ref.shape[0]

    def indexer(qi_blk_ref, tq_blk_ref, blk):
        w_t = tq_blk_ref[...].astype(F32).T[IDX_DIM:IDX_DIM + IDX_HEADS, :] * IDX_SCALE
        qi = qi_blk_ref[...]
        qi_rows = jnp.concatenate([qi[:, h * IDX_DIM:(h + 1) * IDX_DIM] for h in range(IDX_HEADS)], axis=0)

        def chunk_scores(c):
            off = pl.multiple_of(c * KEY_CHUNK, KEY_CHUNK)
            rel = jnp.maximum(_dot_nt(tail_ref[pl.ds(off, KEY_CHUNK), 0:IDX_DIM], qi_rows), 0.0)
            score = rel[:, 0:BLK] * w_t[0:1, :]
            for h in range(1, IDX_HEADS):
                score = score + rel[:, h * BLK:(h + 1) * BLK] * w_t[h:h + 1, :]
            return score

        def store_chunk(c, score):
            score = jnp.where(score == 0.0, 0.0, score)
            score_ref[c] = score
            bits = pltpu.bitcast(score, I32)
            u = bits ^ ((bits >> 31) | INT_MIN)
            for g in range(GROUPS_PER_CHUNK):
                words = [u[(g * 32 + j) * 8:(g * 32 + j + 1) * 8, :] for j in range(32)]
                planes = _bit_transpose32(words)
                for b in range(32):
                    planes_ref[c * GROUPS_PER_CHUNK + g, b] = planes[b]

        def full_chunk(c):
            store_chunk(c, chunk_scores(c))

        def last_chunk():
            last = _chunks_of_block(blk) - 1
            key_pos = last * KEY_CHUNK + lax.broadcasted_iota(I32, (KEY_CHUNK, BLK), 0)
            t_col = blk * BLK + lax.broadcasted_iota(I32, (KEY_CHUNK, BLK), 1)
            store_chunk(last, jnp.where(key_pos <= t_col, chunk_scores(last), -jnp.inf))

        return full_chunk, last_chunk

    @pl.when(i == 0)
    def _():
        indexer(qi_ref, tq_ref, 0)[1]()

    def init_alive(c, _):
        for g in range(GROUPS_PER_CHUNK):
            alive_ref[c * GROUPS_PER_CHUNK + g] = jnp.full((8, BLK), -1, I32)
        return 0

    lax.fori_loop(0, n_chunks, init_alive, 0)

    @pl.when(n_chunks % 2 == 1)
    def _():
        for g in range(GROUPS_PER_CHUNK):
            alive_ref[n_chunks * GROUPS_PER_CHUNK + g] = jnp.zeros((8, BLK), I32)
            planes_ref[n_chunks * GROUPS_PER_CHUNK + g] = jnp.zeros((32, 8, BLK), I32)

    def decide(state, counts, pair):
        above, t_bits = state[0], state[1]
        with_hi, with_both, only_lo = counts
        bit_hi = jnp.int32(1) << (31 - 2 * pair)
        bit_lo = jnp.int32(1) << (30 - 2 * pair)
        take_hi = (above + with_hi) >= topk
        above = jnp.where(take_hi, above, above + with_hi)
        with_lo = jnp.where(take_hi, with_both, only_lo)
        take_lo = (above + with_lo) >= topk
        above = jnp.where(take_lo, above, above + with_lo)
        t_bits = t_bits | jnp.where(take_hi, bit_hi, 0) | jnp.where(take_lo, bit_lo, 0)
        return above, t_bits, jnp.where(take_hi, 0, -1), jnp.where(take_lo, 0, -1)

    groups_per_step = 2 * GROUPS_PER_CHUNK

    def sweep(prev, cur, drop_hi, drop_lo):
        def step(p, accs):
            accs = list(accs)
            for g in range(groups_per_step):
                gi = p * groups_per_step + g
                alive = alive_ref[gi]
                if prev is not None:
                    alive = (alive & (planes_ref[gi, 2 * prev] ^ drop_hi)) & (planes_ref[gi, 2 * prev + 1] ^ drop_lo)
                    alive_ref[gi] = alive
                if cur is None:
                    accs[0] = accs[0] + lax.population_count(alive)
                else:
                    lo = planes_ref[gi, 2 * cur + 1]
                    with_hi = alive & planes_ref[gi, 2 * cur]
                    accs[0] = accs[0] + lax.population_count(with_hi)
                    accs[1] = accs[1] + lax.population_count(with_hi & lo)
                    accs[2] = accs[2] + lax.population_count((alive ^ with_hi) & lo)
            return tuple(accs)

        n_acc = 1 if cur is None else 3
        accs = lax.fori_loop(0, (n_chunks + 1) // 2, step, tuple(jnp.zeros((8, BLK), I32) for _ in range(n_acc)))
        return tuple(jnp.sum(a, axis=0, keepdims=True) for a in accs)

    zero = jnp.zeros((1, BLK), I32)
    state = decide((zero, zero), sweep(None, 0, None, None), 0)

    def pair_step(pair, state):
        return decide(state, sweep(pair - 1, pair, state[2], state[3]), pair)

    _, t_bits, _, _ = lax.fori_loop(1, 16, pair_step, state)
    thr = t_bits ^ INT_MIN
    real = thr > KEY_OF_NEG_INF

    def count_ge(*bits):
        cands = [_ordered_bits_to_float(b) for b in bits]

        def chunk(c, accs):
            sc = score_ref[c]
            return tuple(acc + jnp.sum(jnp.where(sc >= cand, 1, 0).reshape(KEY_CHUNK // 8, 8, BLK), axis=0)
                         for acc, cand in zip(accs, cands))

        accs = lax.fori_loop(0, n_chunks, chunk, tuple(jnp.zeros((8, BLK), I32) for _ in bits))
        return tuple(jnp.sum(acc, axis=0, keepdims=True) for acc in accs)

    def any_lane(mask):
        return jnp.max(jnp.where(mask, 1, 0)) > 0

    n_thr, n_next = count_ge(thr, thr + 1)
    sel_ref[0:1, :] = thr
    sel_ref[1:2, :] = n_thr
    sel_ref[2:3, :] = n_next
    confirmed = jnp.logical_and(n_thr >= topk, n_next < topk)

    @pl.when(any_lane(jnp.logical_and(real, jnp.logical_not(confirmed))))
    def _():
        enough = n_thr >= topk
        one = jnp.ones((1, BLK), I32)

        def up_cond(s):
            return jnp.logical_and(any_lane(s[4] > 0), s[5] < 34)

        def up_body(s):
            base, step, n_base, n_bad, act, it = s
            n_probe, = count_ge(base + step)
            good = jnp.logical_and(act > 0, n_probe >= topk)
            stop = jnp.logical_and(act > 0, n_probe < topk)
            return (jnp.where(good, base + step, base), jnp.where(good, step << 1, step),
                    jnp.where(good, n_probe, n_base), jnp.where(stop, n_probe, n_bad),
                    jnp.where(good, 1, 0), it + 1)

        up = jnp.where(jnp.logical_and(real, enough), 1, 0)
        base, width, n_base, n_bad, _, _ = lax.while_loop(
            up_cond, up_body, (thr, one, n_thr, jnp.zeros((1, BLK), I32), up, jnp.int32(0)))

        def down_cond(s):
            return jnp.logical_and(any_lane(s[5] > 0), s[6] < 34)

        def down_body(s):
            base, width, step, n_base, n_bad, act, it = s
            n_probe, = count_ge(thr - step)
            good = jnp.logical_and(act > 0, n_probe >= topk)
            miss = jnp.logical_and(act > 0, n_probe < topk)
            return (jnp.where(good, thr - step, base), jnp.where(good, jnp.maximum(step >> 1, 1), width),
                    jnp.where(miss, step << 1, step), jnp.where(good, n_probe, n_base),
                    jnp.where(miss, n_probe, n_bad), jnp.where(miss, 1, 0), it + 1)

        down = jnp.logical_and(real, jnp.logical_not(enough))
        base, width, _, n_base, n_bad, _, _ = lax.while_loop(
            down_cond, down_body,
            (base, width, one, n_base, jnp.where(down, n_thr, n_bad), jnp.where(down, 1, 0), jnp.int32(0)))

        def bisect_cond(s):
            return any_lane(jnp.logical_and(real, s[1] > 1))

        def bisect_body(s):
            base, width, n_base, n_bad = s
            half = width >> 1
            act = jnp.logical_and(real, width > 1)
            n_probe, = count_ge(base + half)
            good = jnp.logical_and(act, n_probe >= topk)
            miss = jnp.logical_and(act, n_probe < topk)
            return (jnp.where(good, base + half, base), jnp.where(act, half, width),
                    jnp.where(good, n_probe, n_base), jnp.where(miss, n_probe, n_bad))

        walked = lax.while_loop(bisect_cond, bisect_body, (base, width, n_base, n_bad))
        sel_ref[0:1, :] = walked[0]
        sel_ref[1:2, :] = walked[2]
        sel_ref[2:3, :] = walked[3]

    thr, n_ge, n_gt = sel_ref[0:1, :], sel_ref[1:2, :], sel_ref[2:3, :]
    need = topk - n_gt
    n_eq = n_ge - n_gt
    thr_f = jnp.where(real, _ordered_bits_to_float(thr), jnp.finfo(F32).min)

    def bias_chunk(c, _):
        bias_ref[c] = jnp.where(score_ref[c] >= thr_f, 0.0, NEG_BIG)
        return 0

    lax.fori_loop(0, n_chunks, bias_chunk, 0)

    has_tie = jnp.max(jnp.where(jnp.logical_and(real, n_eq > need), 1, 0)) > 0

    @pl.when(has_tie)
    def _():
        need_f = need.astype(F32)
        r = lax.broadcasted_iota(I32, (BLK, BLK), 0)
        cc = lax.broadcasted_iota(I32, (BLK, BLK), 1)
        upto = (cc <= r).astype(BF16)

        def tie_chunk(c, run):
            sc = score_ref[c]
            for u in range(KEY_CHUNK // BLK):
                st = sc[u * BLK:(u + 1) * BLK, :]
                eq = jnp.logical_and(st == thr_f, real)
                eqf = jnp.where(eq, 1.0, 0.0)
                rank = _dot(upto, eqf.astype(BF16)) + run
                sel = jnp.logical_or(st > thr_f, jnp.logical_and(eq, rank <= need_f))
                bias_ref[c, u * BLK:(u + 1) * BLK, :] = jnp.where(
                    real, jnp.where(sel, 0.0, NEG_BIG), jnp.where(st >= thr_f, 0.0, NEG_BIG))
                run = run + jnp.sum(eqf, axis=0, keepdims=True)
            return run

        lax.fori_loop(0, n_chunks, tie_chunk, jnp.zeros((1, BLK), F32))

    acc_ref[...] = jnp.zeros_like(acc_ref)
    bias_ref[n_kc] = jnp.full((KEY_CHUNK, BLK), NEG_BIG, F32)
    q = q_ref[...]

    def logits_into(lg, c):
        kc = jnp.minimum(c, n_chunks - 1)
        off = pl.multiple_of(kc * KEY_CHUNK, KEY_CHUNK)
        bias = bias_ref[jnp.where(c < n_chunks, c, n_kc)]
        for h in range(DSA_HEADS):
            hs = slice(h * HEAD_DIM, (h + 1) * HEAD_DIM)
            lg[h] = _dot_nt(k_ref[pl.ds(off, KEY_CHUNK), hs], q[:, hs]) + bias

    def reduce_from(lg, c, ms):
        vc = jnp.minimum(c, n_chunks - 1)
        new_m = []
        for h in range(DSA_HEADS):
            logits = lg[h]
            m_new = jnp.maximum(ms[h], jnp.max(logits, axis=0, keepdims=True))
            alpha = jnp.exp2(ms[h] - m_new)
            pr = jnp.exp2(logits - m_new).astype(BF16)
            acc_ref[h] = alpha * acc_ref[h] + _dot(vt_ref[vc, h * V_ROWS:(h + 1) * V_ROWS, :], pr)
            new_m.append(m_new)
        return tuple(new_m)

    next_full, next_last = indexer(qi_next_ref, tq_next_ref, nxt)
    last_full = _chunks_of_block(nxt) - 2
    n_pairs = (n_chunks + 1) // 2

    def attn_pair(pair, ms):
        c = 2 * pair
        logits_into(lg_b, c + 1)
        ms = reduce_from(lg_a, c, ms)
        next_full(jnp.maximum(jnp.minimum(c, last_full), 0))
        logits_into(lg_a, c + 2)
        ms = reduce_from(lg_b, c + 1, ms)
        next_full(jnp.maximum(jnp.minimum(c + 1, last_full), 0))
        return ms

    logits_into(lg_a, 0)
    lax.fori_loop(0, n_pairs, attn_pair, tuple(jnp.full((1, BLK), NEG_BIG, F32) for _ in range(DSA_HEADS)))
    for h in range(DSA_HEADS):
        weighted = acc_ref[h, 0:HEAD_DIM, :]
        denom = acc_ref[h, HEAD_DIM:HEAD_DIM + 1, :]
        o_ref[:, h * HEAD_DIM:(h + 1) * HEAD_DIM] = (weighted / denom).T.astype(o_ref.dtype)

    def rest(c, _):
        next_full(c)
        return 0

    lax.fori_loop(2 * n_pairs, last_full + 1, rest, 0)
    next_last()


def _dsa(p, k, vt):
    s = p.shape[0]
    nb = s // BLK
    topk = min(TOPK_MAX, s // 4)
    n_kc = s // KEY_CHUNK
    return pl.pallas_call(
        functools.partial(_dsa_kernel, topk=topk),
        grid=(nb,),
        in_specs=[pl.BlockSpec((BLK, D_DSA), lambda i: (i, OFF_D_Q // D_DSA)),
                  pl.BlockSpec((BLK, IDX_HEADS * IDX_DIM), lambda i: (i, OFF_D_QI // (IDX_HEADS * IDX_DIM))),
                  pl.BlockSpec((BLK, BLK), lambda i: (i, OFF_TAIL // BLK)),
                  pl.BlockSpec((BLK, IDX_HEADS * IDX_DIM),
                               lambda i: (jnp.minimum(i + 1, nb - 1), OFF_D_QI // (IDX_HEADS * IDX_DIM))),
                  pl.BlockSpec((BLK, BLK), lambda i: (jnp.minimum(i + 1, nb - 1), OFF_TAIL // BLK)),
                  _resident((s, BLK), lambda i: (0, OFF_TAIL // BLK)),
                  _resident((s, D_DSA), lambda i: (0, 0)),
                  _resident((n_kc, DSA_HEADS * V_ROWS, KEY_CHUNK), lambda i: (0, 0, 0))],
        out_specs=pl.BlockSpec((BLK, D_DSA), lambda i: (i, 0)),
        out_shape=jax.ShapeDtypeStruct((s, D_DSA), BF16),
        scratch_shapes=[pltpu.VMEM((n_kc, KEY_CHUNK, BLK), F32),
                        pltpu.VMEM((n_kc * GROUPS_PER_CHUNK, 32, 8, BLK), I32),
                        pltpu.VMEM((n_kc * GROUPS_PER_CHUNK, 8, BLK), I32),
                        pltpu.VMEM((n_kc + 1, KEY_CHUNK, BLK), F32),
                        pltpu.VMEM((DSA_HEADS, V_ROWS, BLK), F32),
                        pltpu.VMEM((DSA_HEADS, KEY_CHUNK, BLK), F32),
                        pltpu.VMEM((DSA_HEADS, KEY_CHUNK, BLK), F32),
                        pltpu.VMEM((8, BLK), I32)],
        compiler_params=_params("arbitrary"),
        name="dsa",
    )(p, p, p, p, p, p, k, vt)


def _layer_norm(r, g, b):
    mu = jnp.mean(r, axis=1, keepdims=True)
    var = jnp.mean(jnp.square(r - mu), axis=1, keepdims=True)
    return (r - mu) * lax.rsqrt(var + LN_EPS) * g + b


def _first_max_of4(vals):
    a, b, c, d = vals
    m = jnp.maximum(jnp.maximum(a, b), jnp.maximum(c, d))
    idx = jnp.where(a == m, 0, jnp.where(b == m, 1, jnp.where(c == m, 2, 3)))
    return m, idx


def _router_gates(logits_t):
    mx = jnp.max(logits_t, axis=0, keepdims=True)
    e = jnp.exp(logits_t - mx)
    probs = e / jnp.sum(e, axis=0, keepdims=True)
    rows = [probs[j:j + 1, :] for j in range(N_EXPERTS)]
    m1s, m2s, i1s, i2s, scores = [], [], [], [], []
    for g in range(N_GROUPS):
        vals = rows[g * EXPERTS_PER_GROUP:(g + 1) * EXPERTS_PER_GROUP]
        m1, i1 = _first_max_of4(vals)
        rest = [jnp.where(i1 == j, -1.0, vals[j]) for j in range(EXPERTS_PER_GROUP)]
        m2, i2 = _first_max_of4(rest)
        m1s.append(m1); m2s.append(m2); i1s.append(i1); i2s.append(i2); scores.append(m1 + m2)
    best, g_sel = _first_max_of4(scores)
    pick = lambda xs: jnp.where(g_sel == 0, xs[0], jnp.where(g_sel == 1, xs[1], jnp.where(g_sel == 2, xs[2], xs[3])))
    m1, m2, i1, i2 = pick(m1s), pick(m2s), pick(i1s), pick(i2s)
    den = m1 + m2
    w1, w2 = m1 / den, m2 / den
    e1 = g_sel * EXPERTS_PER_GROUP + i1
    e2 = g_sel * EXPERTS_PER_GROUP + i2
    gates = [jnp.where(e1 == j, w1, 0.0) + jnp.where(e2 == j, w2, 0.0) for j in range(N_EXPERTS)]
    return jnp.concatenate(gates, axis=0)


def _out_kernel(ysb_ref, yret_ref, ydsa_ref, wo_f32_ref, x_ref, g_ref, b_ref, wr_ref,
                x1_ref, gates_ref, gates_t_ref, cnt_ref, wo_ref):
    @pl.when(pl.program_id(0) == 0)
    def _():
        wo_ref[...] = wo_f32_ref[0].astype(BF16)

    mix = (_dot(ysb_ref[...], wo_ref[0:D_SB, :]) + _dot(yret_ref[...], wo_ref[D_SB:D_SB + D_RET, :])
           + _dot(ydsa_ref[...], wo_ref[D_SB + D_RET:, :]))
    x1 = _layer_norm(DEEPNORM_ALPHA * x_ref[...] + mix, g_ref[...], b_ref[...])
    x1_ref[...] = x1
    logits_t = lax.dot_general(wr_ref[...], x1, (((1,), (1,)), ((), ())),
                               preferred_element_type=F32, precision=lax.Precision.HIGHEST)
    gates_t = _router_gates(logits_t)
    tm = x1.shape[0]
    gates_t_ref[...] = gates_t
    padded = jnp.concatenate([gates_t, jnp.zeros((BLK - N_EXPERTS, tm), F32)], axis=0)
    gates_ref[...] = padded.T
    chosen = jnp.sum(jnp.where(gates_t > 0.0, 1, 0), axis=1, keepdims=True)
    cnt_ref[0] = jnp.broadcast_to(chosen, (N_EXPERTS, BLK))


def _out_proj(ysb, yret, ydsa, w_o, layer, x, g, b, wr_t):
    s = x.shape[0]
    tm = MOE_TILE
    row = lambda n: pl.BlockSpec((tm, n), lambda i: (i, 0))
    whole = lambda a: _resident(a.shape, lambda i: (0, 0))
    return pl.pallas_call(
        _out_kernel,
        grid=(s // tm,),
        in_specs=[row(D_SB), row(D_RET), row(D_DSA),
                  _resident((1,) + w_o.shape[1:], lambda i: (layer, 0, 0)),
                  row(D_MODEL), whole(g), whole(b), whole(wr_t)],
        out_specs=[row(D_MODEL), row(BLK), pl.BlockSpec((N_EXPERTS, tm), lambda i: (0, i)),
                   pl.BlockSpec((1, N_EXPERTS, BLK), lambda i: (i, 0, 0))],
        out_shape=[jax.ShapeDtypeStruct((s, D_MODEL), F32), jax.ShapeDtypeStruct((s, BLK), F32),
                   jax.ShapeDtypeStruct((N_EXPERTS, s), F32),
                   jax.ShapeDtypeStruct((s // tm, N_EXPERTS, BLK), I32)],
        scratch_shapes=[pltpu.VMEM(w_o.shape[1:], BF16)],
        compiler_params=_params("arbitrary"),
        name="out_proj",
    )(ysb, yret, ydsa, w_o, x, g, b, wr_t)


LOCAL_ROWS = 2 * MOE_TILE + N_EXPERTS * WIN


def _round_up(x, m):
    return (x + m - 1) // m * m


def _sorted_rows(s):
    n_tiles = s // MOE_TILE
    return _round_up(2 * s + n_tiles * N_EXPERTS * (WIN - 1) + N_EXPERTS * (ROW_TILE - 1), ROW_TILE)


def _moe_plan(cnt, s):
    n_row_tiles = _sorted_rows(s) // ROW_TILE
    seg = _round_up(cnt, WIN)
    rows_e = jnp.sum(seg, axis=0)
    region = _round_up(rows_e, ROW_TILE)
    region_off = jnp.cumsum(region) - region
    dest = region_off[None, :] + jnp.cumsum(seg, axis=0) - seg
    tiles_e = region // ROW_TILE
    tile_end = jnp.cumsum(tiles_e)
    k = jnp.arange(n_row_tiles, dtype=I32)
    tile_expert = jnp.minimum(jnp.sum((k[:, None] >= tile_end[None, :]).astype(I32), axis=1), N_EXPERTS - 1)
    first = (tile_end - tiles_e)[tile_expert]
    valid = jnp.clip(rows_e[tile_expert] - (k - first) * ROW_TILE, 0, ROW_TILE)
    tile_valid = jnp.where(k < tile_end[-1], valid, 0).astype(I32)
    fill = jnp.concatenate([region_off + rows_e, (region - rows_e) // WIN,
                            tile_end[-1:], n_row_tiles - tile_end[-1:]]).astype(I32)
    used = rows_e > 0
    e_id = jnp.arange(N_EXPERTS, dtype=I32)
    buffer_e = (jnp.cumsum(used) - used) % 2
    later_used = jnp.where(jnp.logical_and(used[None, :], e_id[None, :] > e_id[:, None]), e_id[None, :], N_EXPERTS)
    next_e = jnp.min(later_used, axis=1)
    next_e = jnp.where(next_e == N_EXPERTS, -1, next_e)
    tile_info = jnp.stack([tile_expert, tile_valid, buffer_e[tile_expert], next_e[tile_expert]]).astype(I32)
    return cnt.reshape(-1).astype(I32), dest.reshape(-1).astype(I32), fill, tile_info.reshape(-1)


def _slot_offsets(cnt_sm, tile):
    offs, o = [], 0
    for e in range(N_EXPERTS):
        offs.append(o)
        o = o + _round_up(cnt_sm[tile * N_EXPERTS + e], WIN)
    return offs


def _window_copies(cnt_sm, dest_sm, tile, offs, local_ref, sorted_hbm, sem, to_sorted):
    total = 0
    for e in range(N_EXPERTS):
        n_win = (cnt_sm[tile * N_EXPERTS + e] + WIN - 1) // WIN
        base_local, base_sorted = offs[e], dest_sm[tile * N_EXPERTS + e]

        def issue(j, _, base_local=base_local, base_sorted=base_sorted):
            loc = local_ref.at[pl.ds(pl.multiple_of(base_local + j * WIN, WIN), WIN)]
            srt = sorted_hbm.at[pl.ds(pl.multiple_of(base_sorted + j * WIN, WIN), WIN)]
            if to_sorted:
                pltpu.make_async_copy(loc, srt, sem).start()
            else:
                pltpu.make_async_copy(srt, loc, sem).start()
            return 0

        lax.fori_loop(0, n_win, issue, 0)
        total = total + n_win
    return total


def _n_windows(cnt_sm, tile):
    total = 0
    for e in range(N_EXPERTS):
        total = total + (cnt_sm[tile * N_EXPERTS + e] + WIN - 1) // WIN
    return total


def _wait_windows(total, local_ref, sorted_hbm, sem):
    def wait(j, _):
        pltpu.make_async_copy(local_ref.at[pl.ds(0, WIN)], sorted_hbm.at[pl.ds(0, WIN)], sem).wait()
        return 0

    lax.fori_loop(0, total, wait, 0)


def _zero_fill(fill_sm, xs_hbm, zero_ref, sem_win, sem_tile):
    zero_ref[...] = jnp.zeros_like(zero_ref)
    win_copy = lambda row: pltpu.make_async_copy(
        zero_ref.at[pl.ds(0, WIN)], xs_hbm.at[pl.ds(pl.multiple_of(row, WIN), WIN)], sem_win)
    tile_copy = lambda row: pltpu.make_async_copy(
        zero_ref, xs_hbm.at[pl.ds(pl.multiple_of(row, ROW_TILE), ROW_TILE)], sem_tile)
    n_pad = 0
    for e in range(N_EXPERTS):
        first, n_win = fill_sm[e], fill_sm[N_EXPERTS + e]

        def issue(j, _, first=first):
            win_copy(first + j * WIN).start()
            return 0

        lax.fori_loop(0, n_win, issue, 0)
        n_pad = n_pad + n_win
    first_tile, n_tail = fill_sm[2 * N_EXPERTS], fill_sm[2 * N_EXPERTS + 1]

    def issue_tile(j, _):
        tile_copy((first_tile + j) * ROW_TILE).start()
        return 0

    def wait_win(j, _):
        win_copy(0).wait()
        return 0

    def wait_tile(j, _):
        tile_copy(0).wait()
        return 0

    lax.fori_loop(0, n_tail, issue_tile, 0)
    lax.fori_loop(0, n_pad, wait_win, 0)
    lax.fori_loop(0, n_tail, wait_tile, 0)


def _dispatch_kernel(cnt_sm, dest_sm, fill_sm, x_ref, gt_ref, xs_hbm, local_ref, zero_ref, sem, sem_win, sem_tile):
    tile = pl.program_id(0)

    @pl.when(tile == 0)
    def _():
        _zero_fill(fill_sm, xs_hbm, zero_ref, sem_win, sem_tile)

    offs = _slot_offsets(cnt_sm, tile)
    chosen = gt_ref[...] > 0.0
    t_r = lax.broadcasted_iota(I32, (MOE_TILE, MOE_TILE), 0)
    t_c = lax.broadcasted_iota(I32, (MOE_TILE, MOE_TILE), 1)
    earlier = _dot(jnp.where(chosen, 1.0, 0.0).astype(BF16), (t_r < t_c).astype(BF16))
    e_id = lax.broadcasted_iota(I32, (N_EXPERTS, 1), 0)
    slot = jnp.zeros((N_EXPERTS, 1), I32)
    for e in range(N_EXPERTS):
        slot = jnp.where(e_id == e, offs[e], slot)
    pos = earlier + slot.astype(F32)
    p_lo = jnp.min(jnp.where(chosen, pos, float(LOCAL_ROWS)), axis=0, keepdims=True).astype(I32)
    p_hi = jnp.max(jnp.where(chosen, pos, -1.0), axis=0, keepdims=True).astype(I32)
    row = lax.broadcasted_iota(I32, (LOCAL_ROWS, MOE_TILE), 0)
    onehot = jnp.where(row == p_lo, 1.0, jnp.where(row == p_hi, 1.0, 0.0)).astype(BF16)
    buf = tile % 2
    mine, mine_sem = local_ref.at[buf], sem.at[buf]
    other, other_sem = local_ref.at[1 - buf], sem.at[1 - buf]
    mine[...] = _dot(onehot, x_ref[...].astype(BF16)).astype(BF16)
    total = _window_copies(cnt_sm, dest_sm, tile, offs, mine, xs_hbm, mine_sem, True)

    @pl.when(tile >= 1)
    def _():
        _wait_windows(_n_windows(cnt_sm, tile - 1), other, xs_hbm, other_sem)

    @pl.when(tile == pl.num_programs(0) - 1)
    def _():
        _wait_windows(total, mine, xs_hbm, mine_sem)


def _dispatch(x1, gates_t, cnt_flat, dest_flat, fill):
    s = x1.shape[0]
    return pl.pallas_call(
        _dispatch_kernel,
        grid_spec=pltpu.PrefetchScalarGridSpec(
            num_scalar_prefetch=3,
            grid=(s // MOE_TILE,),
            in_specs=[pl.BlockSpec((MOE_TILE, D_MODEL), lambda i, *_: (i, 0)),
                      pl.BlockSpec((N_EXPERTS, MOE_TILE), lambda i, *_: (0, i))],
            out_specs=pl.BlockSpec(memory_space=pl.ANY),
            scratch_shapes=[pltpu.VMEM((2, LOCAL_ROWS, D_MODEL), BF16), pltpu.VMEM((ROW_TILE, D_MODEL), BF16),
                            pltpu.SemaphoreType.DMA((2,)), pltpu.SemaphoreType.DMA(()),
                            pltpu.SemaphoreType.DMA(())]),
        out_shape=jax.ShapeDtypeStruct((_sorted_rows(s), D_MODEL), BF16),
        compiler_params=_params("arbitrary"),
        name="moe_dispatch",
    )(cnt_flat, dest_flat, fill, x1, gates_t)


def _expert_kernel(info_sm, xs_ref, wg_hbm, wu_hbm, wd_hbm, y_ref, wg_f, wu_f, wd_f, wg_b, wu_b, wd_b, sem, *,
                   layer):
    k = pl.program_id(0)
    n = pl.num_programs(0)
    expert, valid, buf, next_expert = info_sm[k], info_sm[n + k], info_sm[2 * n + k], info_sm[3 * n + k]
    new_expert = jnp.logical_or(k == 0, expert != info_sm[jnp.maximum(k - 1, 0)])

    def weight_copies(e, b):
        return (pltpu.make_async_copy(wg_hbm.at[layer, e], wg_f.at[b], sem.at[b, 0]),
                pltpu.make_async_copy(wu_hbm.at[layer, e], wu_f.at[b], sem.at[b, 1]),
                pltpu.make_async_copy(wd_hbm.at[layer, e], wd_f.at[b], sem.at[b, 2]))

    @pl.when(k == 0)
    def _():
        for c in weight_copies(expert, buf):
            c.start()

    @pl.when(jnp.logical_and(valid > 0, new_expert))
    def _():
        for c in weight_copies(expert, buf):
            c.wait()
        wg_b[...] = wg_f[buf].astype(BF16)
        wu_b[...] = wu_f[buf].astype(BF16)
        wd_b[...] = wd_f[buf].astype(BF16)

        @pl.when(next_expert >= 0)
        def _():
            for c in weight_copies(next_expert, 1 - buf):
                c.start()

    @pl.when(valid > 0)
    def _():
        x = xs_ref[...]
        hg = _dot(x, wg_b[...])
        hu = _dot(x, wu_b[...])
        act = hg * jax.nn.sigmoid(hg) * hu
        y_ref[...] = _dot(act.astype(BF16), wd_b[...]).astype(y_ref.dtype)

    @pl.when(valid == 0)
    def _():
        y_ref[...] = jnp.zeros_like(y_ref)


def _experts(xs, tile_info, w_gate, w_up, w_down, layer):
    rows = xs.shape[0]
    hbm = pl.BlockSpec(memory_space=pl.ANY)
    up_shape, down_shape = (D_MODEL, D_FF_EXPERT), (D_FF_EXPERT, D_MODEL)
    return pl.pallas_call(
        functools.partial(_expert_kernel, layer=layer),
        grid_spec=pltpu.PrefetchScalarGridSpec(
            num_scalar_prefetch=1,
            grid=(rows // ROW_TILE,),
            in_specs=[pl.BlockSpec((ROW_TILE, D_MODEL), lambda k, info: (k, 0)), hbm, hbm, hbm],
            out_specs=pl.BlockSpec((ROW_TILE, D_MODEL), lambda k, info: (k, 0)),
            scratch_shapes=[pltpu.VMEM((2,) + up_shape, F32), pltpu.VMEM((2,) + up_shape, F32),
                            pltpu.VMEM((2,) + down_shape, F32),
                            pltpu.VMEM(up_shape, BF16), pltpu.VMEM(up_shape, BF16), pltpu.VMEM(down_shape, BF16),
                            pltpu.SemaphoreType.DMA((2, 3))]),
        out_shape=jax.ShapeDtypeStruct((rows, D_MODEL), BF16),
        compiler_params=_params("arbitrary"),
        name="moe_experts",
    )(tile_info, xs, w_gate, w_up, w_down)


def _combine_kernel(cnt_sm, dest_sm, x_ref, gates_ref, y_hbm, g_ref, b_ref, o_ref, local_ref, sem):
    tile = pl.program_id(0)

    buf = tile % 2
    mine, mine_sem = local_ref.at[buf], sem.at[buf]
    offs = _slot_offsets(cnt_sm, tile)

    @pl.when(tile == 0)
    def _():
        local_ref[...] = jnp.zeros_like(local_ref)
        _window_copies(cnt_sm, dest_sm, tile, offs, mine, y_hbm, mine_sem, False)

    @pl.when(tile + 1 < pl.num_programs(0))
    def _():
        _window_copies(cnt_sm, dest_sm, tile + 1, _slot_offsets(cnt_sm, tile + 1),
                       local_ref.at[1 - buf], y_hbm, sem.at[1 - buf], False)

    gates = gates_ref[...]
    chosen = gates > 0.0
    t_r = lax.broadcasted_iota(I32, (MOE_TILE, MOE_TILE), 0)
    t_c = lax.broadcasted_iota(I32, (MOE_TILE, MOE_TILE), 1)
    earlier = _dot((t_c < t_r).astype(BF16), jnp.where(chosen, 1.0, 0.0).astype(BF16))
    e_id = lax.broadcasted_iota(I32, (1, BLK), 1)
    slot = jnp.zeros((1, BLK), I32)
    for e in range(N_EXPERTS):
        slot = jnp.where(e_id == e, offs[e], slot)
    pos = jnp.where(chosen, earlier + slot.astype(F32), -1.0)
    p_lo = jnp.min(jnp.where(chosen, pos, float(LOCAL_ROWS)), axis=1, keepdims=True)
    p_hi = jnp.max(pos, axis=1, keepdims=True)
    w_lo = jnp.sum(jnp.where(pos == p_lo, gates, 0.0), axis=1, keepdims=True)
    w_hi = jnp.sum(jnp.where(pos == p_hi, gates, 0.0), axis=1, keepdims=True)
    col = lax.broadcasted_iota(I32, (MOE_TILE, LOCAL_ROWS), 1)
    weights = jnp.where(col == p_lo.astype(I32), w_lo, jnp.where(col == p_hi.astype(I32), w_hi, 0.0))

    _wait_windows(_n_windows(cnt_sm, tile), mine, y_hbm, mine_sem)
    ffn = _dot(weights.astype(BF16), mine[...])
    o_ref[...] = _layer_norm(DEEPNORM_ALPHA * x_ref[...] + ffn, g_ref[...], b_ref[...])


def _combine(x1, gates, y, cnt_flat, dest_flat, g, b):
    s = x1.shape[0]
    row = lambda n: pl.BlockSpec((MOE_TILE, n), lambda i, *_: (i, 0))
    vec = pl.BlockSpec((1, D_MODEL), lambda i, *_: (0, 0))
    return pl.pallas_call(
        _combine_kernel,
        grid_spec=pltpu.PrefetchScalarGridSpec(
            num_scalar_prefetch=2,
            grid=(s // MOE_TILE,),
            in_specs=[row(D_MODEL), row(BLK), pl.BlockSpec(memory_space=pl.ANY), vec, vec],
            out_specs=row(D_MODEL),
            scratch_shapes=[pltpu.VMEM((2, LOCAL_ROWS, D_MODEL), BF16), pltpu.SemaphoreType.DMA((2,))]),
        out_shape=jax.ShapeDtypeStruct((s, D_MODEL), F32),
        compiler_params=_params("arbitrary"),
        name="moe_combine",
    )(cnt_flat, dest_flat, x1, gates, y, g, b)


def _moe(x1, gates, gates_t, cnt, w_gate, w_up, w_down, g, b, layer):
    s = x1.shape[0]
    cnt_flat, dest_flat, fill, tile_info = _moe_plan(cnt[:, :, 0], s)
    xs = _dispatch(x1, gates_t, cnt_flat, dest_flat, fill)
    y = _experts(xs, tile_info, w_gate, w_up, w_down, layer)
    return _combine(x1, gates, y, cnt_flat, dest_flat, g, b)


COL_BLOCK = 256
D_IN = 3 * D_SB + 4 * D_RET + D_DSA + KV_RANK + IDX_HEADS * IDX_DIM + IDX_DIM + IDX_HEADS


def _column_block_order():
    src = {}
    o = 0
    for name, width in (("sb", 3 * D_SB), ("ret", 4 * D_RET), ("d_q", D_DSA), ("d_ckv", KV_RANK),
                        ("d_qi", IDX_HEADS * IDX_DIM), ("tail", COL_BLOCK)):
        src[name] = list(range(o // COL_BLOCK, (o + width) // COL_BLOCK))
        o += width
    order = src["sb"] + src["d_q"] + src["ret"] + src["d_qi"] + src["d_ckv"] + src["tail"]
    assert len(order) == D_PROJ // COL_BLOCK
    return np.asarray(order, np.int32)


def _reorder_kernel(order_sm, w_ref, o_ref):
    del order_sm
    o_ref[...] = w_ref[...]


def _reorder_w_in(w_in):
    depth, d, _ = w_in.shape
    w_in = jnp.pad(w_in.astype(BF16), ((0, 0), (0, 0), (0, D_PROJ - D_IN)))
    return pl.pallas_call(
        _reorder_kernel,
        grid_spec=pltpu.PrefetchScalarGridSpec(
            num_scalar_prefetch=1,
            grid=(depth, D_PROJ // COL_BLOCK),
            in_specs=[pl.BlockSpec((1, d, COL_BLOCK), lambda l, j, order: (l, 0, order[j]))],
            out_specs=pl.BlockSpec((1, d, COL_BLOCK), lambda l, j, order: (l, 0, j))),
        out_shape=jax.ShapeDtypeStruct((depth, d, D_PROJ), BF16),
        compiler_params=_params("parallel", "arbitrary"),
        name="reorder_w_in",
    )(jnp.asarray(_column_block_order()), w_in)


def kernel(x, w_in, w_kv_up, kv_norm_g, ret_gn_g, w_o, ln1_g, ln1_b, w_router, w_gate, w_up, w_down,
           ln2_g, ln2_b):
    b, s, _ = x.shape
    assert b == 1 and s % KEY_CHUNK == 0
    h = x[0]
    tables = _retention_tables(s)
    wr_t = w_router.T
    w_proj = _reorder_w_in(w_in)
    for l in range(DEPTH):
        p = _proj(h, w_proj, l)
        y_sb = _stick_breaking(p)
        y_ret = _retention(p, ret_gn_g[l][None, :], tables)
        w_kv = w_kv_up[l].reshape(KV_RANK, DSA_HEADS, 2, HEAD_DIM)
        w_kv = jnp.concatenate([w_kv[:, :, 0, :].reshape(KV_RANK, D_DSA),
                                w_kv[:, :, 1, :].reshape(KV_RANK, D_DSA)], axis=1).astype(BF16)
        k_dsa, vt_dsa = _kv_up(p, kv_norm_g[l][None, :], w_kv)
        y_dsa = _dsa(p, k_dsa, vt_dsa)
        x1, gates, gates_t, cnt = _out_proj(y_sb, y_ret, y_dsa, w_o, l, h, ln1_g[l][None, :], ln1_b[l][None, :],
                                            wr_t)
        h = _moe(x1, gates, gates_t, cnt, w_gate, w_up, w_down, ln2_g[l][None, :], ln2_b[l][None, :], l)
    return h[None]
```

```python
import functools

import numpy as np
import jax
import jax.numpy as jnp
from jax import lax
from jax.experimental import pallas as pl
from jax.experimental.pallas import tpu as pltpu

F32 = jnp.float32
BF16 = jnp.bfloat16
I32 = jnp.int32

D_MODEL = 2048
HEAD_DIM = 128
SB_HEADS = 6
RET_HEADS = 4
DSA_HEADS = 6
D_SB = SB_HEADS * HEAD_DIM
D_RET = RET_HEADS * HEAD_DIM
D_DSA = DSA_HEADS * HEAD_DIM
KV_RANK = 256
IDX_HEADS = 8
IDX_DIM = 64
IDX_SCALE = IDX_DIM ** -0.5 * IDX_HEADS ** -0.5
TOPK_MAX = 256
BLK = 128
N_EXPERTS = 16
N_GROUPS = 4
EXPERTS_PER_GROUP = N_EXPERTS // N_GROUPS
D_FF_EXPERT = 512
LN_EPS = 1e-5
RMS_EPS = 1e-6
GN_EPS = 1e-6
DEPTH = 2
DEEPNORM_ALPHA = (2 * DEPTH) ** 0.25

OFF_SB_Q = 0
OFF_SB_K = OFF_SB_Q + D_SB
OFF_SB_V = OFF_SB_K + D_SB
OFF_D_Q = OFF_SB_V + D_SB
OFF_R_Q = OFF_D_Q + D_DSA
OFF_R_K = OFF_R_Q + D_RET
OFF_R_V = OFF_R_K + D_RET
OFF_R_G = OFF_R_V + D_RET
OFF_D_QI = OFF_R_G + D_RET
OFF_D_CKV = OFF_D_QI + IDX_HEADS * IDX_DIM
OFF_TAIL = OFF_D_CKV + KV_RANK
D_PROJ = 6144

KEY_CHUNK = 512
MOE_TILE = 512
WIN = 16
ROW_TILE = 512
VMEM_LIMIT = 56 * 1024 * 1024
LOG2_E = 1.4426950408889634
NEG_BIG = -1e30
EXP_UNDERFLOW = -87.4
KEY_OF_NEG_INF = -2139095041
INT_MIN = -2147483648


def _dot(a, b):
    return jnp.dot(a, b, preferred_element_type=F32)


def _dot_nt(a, b):
    return lax.dot_general(a, b, (((1,), (1,)), ((), ())), preferred_element_type=F32)


def _dot_tn(a, b):
    return lax.dot_general(a, b, (((0,), (0,)), ((), ())), preferred_element_type=F32)


def _params(*sem):
    return pltpu.CompilerParams(dimension_semantics=sem, vmem_limit_bytes=VMEM_LIMIT)


def _resident(shape, index_map):
    return pl.BlockSpec(shape, index_map, pipeline_mode=pl.Buffered(1))


def _proj_kernel(x_ref, w_ref, o_ref):
    o_ref[...] = _dot(x_ref[...].astype(BF16), w_ref[0]).astype(o_ref.dtype)


def _proj(x, w, layer):
    s, d = x.shape
    n = w.shape[2]
    tm = min(1024, s)
    tn = 1536
    return pl.pallas_call(
        _proj_kernel,
        grid=(s // tm, n // tn),
        in_specs=[pl.BlockSpec((tm, d), lambda i, j: (i, 0)),
                  pl.BlockSpec((1, d, tn), lambda i, j: (layer, 0, j))],
        out_specs=pl.BlockSpec((tm, tn), lambda i, j: (i, j)),
        out_shape=jax.ShapeDtypeStruct((s, n), BF16),
        compiler_params=_params("parallel", "arbitrary"),
        name="proj",
    )(x, w)


def _sb_kernel(q_ref, k_ref, v_ref, o_ref, acc_ref):
    i = pl.program_id(0)
    q = q_ref[...]
    scale = HEAD_DIM ** -0.5
    key_pos = lax.broadcasted_iota(I32, (BLK, BLK), 0)
    qry_pos = lax.broadcasted_iota(I32, (BLK, BLK), 1)
    later = (qry_pos > key_pos).astype(BF16)

    def key_tile(j, first_query, cs):
        off = pl.multiple_of(j * BLK, BLK)
        strict = (off + key_pos) < (first_query + qry_pos)
        pvs, new_cs = [], []
        for h in range(SB_HEADS):
            hs = slice(h * HEAD_DIM, (h + 1) * HEAD_DIM)
            z = _dot_nt(k_ref[pl.ds(off, BLK), hs], q[:, hs]) * scale
            sp = jnp.maximum(z, 0.0) + jnp.log1p(jnp.exp(-jnp.abs(z)))
            log_rem = jnp.where(strict, -sp, 0.0)
            hi = log_rem.astype(BF16)
            lo = (log_rem - hi.astype(F32)).astype(BF16)
            after = _dot(later, hi) + _dot(later, lo)
            a = jnp.where(strict, jnp.exp(z - sp + after + cs[h]), 0.0)
            pvs.append(_dot_tn(a.astype(BF16), v_ref[pl.ds(off, BLK), hs]))
            new_cs.append(cs[h] + jnp.sum(log_rem, axis=0, keepdims=True))
        return pvs, tuple(new_cs)

    cs = tuple(jnp.zeros((1, BLK), F32) for _ in range(SB_HEADS))
    pv_a, cs = key_tile(i, i * BLK, cs)
    pv_b, cs = key_tile(jnp.maximum(i - 1, 0), jnp.where(i >= 1, i * BLK, -BLK), cs)
    for h in range(SB_HEADS):
        acc_ref[h] = pv_a[h] + pv_b[h]

    def cond(carry):
        j, cs = carry
        c_max = functools.reduce(jnp.maximum, cs)
        return jnp.logical_and(j >= 0, jnp.max(c_max) > EXP_UNDERFLOW)

    def body(carry):
        j, cs = carry
        pvs, cs = key_tile(j, i * BLK, cs)
        for h in range(SB_HEADS):
            acc_ref[h] += pvs[h]
        return j - 1, cs

    lax.while_loop(cond, body, (i - 2, cs))
    for h in range(SB_HEADS):
        o_ref[:, h * HEAD_DIM:(h + 1) * HEAD_DIM] = acc_ref[h].astype(o_ref.dtype)


def _stick_breaking(p):
    s = p.shape[0]
    nb = s // BLK
    return pl.pallas_call(
        _sb_kernel,
        grid=(nb,),
        in_specs=[pl.BlockSpec((BLK, D_SB), lambda i: (i, OFF_SB_Q // D_SB)),
                  _resident((s, D_SB), lambda i: (0, OFF_SB_K // D_SB)),
                  _resident((s, D_SB), lambda i: (0, OFF_SB_V // D_SB))],
        out_specs=pl.BlockSpec((BLK, D_SB), lambda i: (i, 0)),
        out_shape=jax.ShapeDtypeStruct((s, D_SB), BF16),
        scratch_shapes=[pltpu.VMEM((SB_HEADS, BLK, HEAD_DIM), F32)],
        compiler_params=_params("arbitrary"),
        name="stick_breaking",
    )(p, p, p)


def _ret_kernel(q_ref, k_ref, v_ref, g_ref, cos_ref, sin_ref, intra_ref, qd_ref, kd_ref, cd_ref,
                gn_ref, o_ref, state_ref):
    n = pl.program_id(0)

    @pl.when(n == 0)
    def _():
        state_ref[...] = jnp.zeros_like(state_ref)

    cos = cos_ref[...]
    sin = sin_ref[...]
    for h in range(RET_HEADS):
        hs = slice(h * HEAD_DIM, (h + 1) * HEAD_DIM)
        q = q_ref[:, hs].astype(F32)
        k = k_ref[:, hs].astype(F32)
        v = v_ref[:, hs]
        half = HEAD_DIM // 2
        swap = lambda t: jnp.concatenate([t[:, half:], t[:, :half]], axis=1)
        qr = q * cos + swap(q) * sin
        kr = (k * cos + swap(k) * sin) * (HEAD_DIM ** -0.5)
        scores = _dot_nt(qr.astype(BF16), kr.astype(BF16)) * intra_ref[h]
        state = state_ref[h]
        o = (_dot(scores.astype(BF16), v)
             + _dot((qr * qd_ref[h]).astype(BF16), state.astype(BF16)))
        state_ref[h] = cd_ref[h] * state + _dot_tn((kr * kd_ref[h]).astype(BF16), v)
        mu = jnp.mean(o, axis=1, keepdims=True)
        var = jnp.mean(jnp.square(o - mu), axis=1, keepdims=True)
        on = (o - mu) * lax.rsqrt(var + GN_EPS) * gn_ref[:, hs]
        g = g_ref[:, hs].astype(F32)
        o_ref[:, hs] = (g * jax.nn.sigmoid(g) * on).astype(o_ref.dtype)


def _retention_tables(s):
    f32 = np.float32
    half = HEAD_DIM // 2
    pos = np.arange(s, dtype=np.float64)
    theta = 10000.0 ** (-np.linspace(0.0, 1.0, half))
    ang = pos[:, None] * theta[None, :]
    cos, sin = np.cos(ang).astype(f32), np.sin(ang).astype(f32)
    cos2 = np.concatenate([cos, cos], axis=1)
    sin2 = np.concatenate([-sin, sin], axis=1)
    log_gamma = np.log1p(-(2.0 ** (-5.0 - np.arange(RET_HEADS, dtype=np.float64))))
    idx = np.arange(BLK, dtype=np.float64)
    diff = idx[:, None] - idx[None, :]
    intra = np.where(diff >= 0, np.exp(np.maximum(diff, 0.0)[None] * log_gamma[:, None, None]), 0.0).astype(f32)
    q_decay = np.exp((idx[None, :] + 1.0) * log_gamma[:, None]).astype(f32)
    k_decay = np.exp((BLK - 1.0 - idx[None, :]) * log_gamma[:, None]).astype(f32)
    chunk_decay = np.exp(BLK * log_gamma).astype(f32)
    full = (RET_HEADS, BLK, HEAD_DIM)
    return tuple(jnp.asarray(np.ascontiguousarray(t)) for t in (
        cos2, sin2, intra,
        np.broadcast_to(q_decay[:, :, None], full),
        np.broadcast_to(k_decay[:, :, None], full),
        np.broadcast_to(chunk_decay[:, None, None], full)))


def _retention(p, gn_g, tables):
    s = p.shape[0]
    nc = s // BLK
    cos2, sin2, intra, qd, kd, cd = tables
    col = lambda off: pl.BlockSpec((BLK, D_RET), lambda n: (n, off // D_RET))
    per_head = pl.BlockSpec((RET_HEADS, BLK, HEAD_DIM), lambda n: (0, 0, 0))
    pos_spec = pl.BlockSpec((BLK, HEAD_DIM), lambda n: (n, 0))
    return pl.pallas_call(
        _ret_kernel,
        grid=(nc,),
        in_specs=[col(OFF_R_Q), col(OFF_R_K), col(OFF_R_V), col(OFF_R_G),
                  pos_spec, pos_spec, per_head, per_head, per_head, per_head,
                  pl.BlockSpec((1, D_RET), lambda n: (0, 0))],
        out_specs=pl.BlockSpec((BLK, D_RET), lambda n: (n, 0)),
        out_shape=jax.ShapeDtypeStruct((s, D_RET), BF16),
        scratch_shapes=[pltpu.VMEM((RET_HEADS, HEAD_DIM, HEAD_DIM), F32)],
        compiler_params=_params("arbitrary"),
        name="retention",
    )(p, p, p, p, cos2, sin2, intra, qd, kd, cd, gn_g)


def _kv_up_kernel(c_ref, g_ref, w_ref, k_ref, vt_ref):
    c = c_ref[...].astype(F32)
    y = c * lax.rsqrt(jnp.mean(jnp.square(c), axis=1, keepdims=True) + RMS_EPS) * g_ref[...]
    kv = _dot(y.astype(BF16), w_ref[...])
    k_ref[...] = (kv[:, :D_DSA] * (HEAD_DIM ** -0.5 * LOG2_E)).astype(k_ref.dtype)
    v_t = kv[:, D_DSA:].T
    ones = jnp.ones((V_ROWS - HEAD_DIM, v_t.shape[1]), vt_ref.dtype)
    for h in range(DSA_HEADS):
        vt_ref[0, h * V_ROWS:h * V_ROWS + HEAD_DIM, :] = v_t[h * HEAD_DIM:(h + 1) * HEAD_DIM, :].astype(vt_ref.dtype)
        vt_ref[0, h * V_ROWS + HEAD_DIM:(h + 1) * V_ROWS, :] = ones


def _kv_up(p, g, w):
    s = p.shape[0]
    n = w.shape[1]
    return pl.pallas_call(
        _kv_up_kernel,
        grid=(s // KEY_CHUNK,),
        in_specs=[pl.BlockSpec((KEY_CHUNK, KV_RANK), lambda i: (i, OFF_D_CKV // KV_RANK)),
                  pl.BlockSpec((1, KV_RANK), lambda i: (0, 0)),
                  pl.BlockSpec((KV_RANK, n), lambda i: (0, 0))],
        out_specs=[pl.BlockSpec((KEY_CHUNK, D_DSA), lambda i: (i, 0)),
                   pl.BlockSpec((1, DSA_HEADS * V_ROWS, KEY_CHUNK), lambda i: (i, 0, 0))],
        out_shape=[jax.ShapeDtypeStruct((s, D_DSA), BF16),
                   jax.ShapeDtypeStruct((s // KEY_CHUNK, DSA_HEADS * V_ROWS, KEY_CHUNK), BF16)],
        compiler_params=_params("parallel"),
        name="kv_up",
    )(p, g, w)


def _ordered_bits_to_float(u):
    return pltpu.bitcast(u ^ ((u >> 31) & 0x7FFFFFFF), F32)


V_ROWS = HEAD_DIM + 16
GROUPS_PER_CHUNK = KEY_CHUNK // (32 * 8)


def _bit_transpose32(words):
    a = list(words)
    j, mask = 16, 0x0000FFFF
    while j:
        k = 0
        while k < 32:
            t = (a[k] ^ (a[k + j] >> j)) & mask
            a[k] = a[k] ^ t
            a[k + j] = a[k + j] ^ (t << j)
            k = (k + j + 1) & ~j
        j >>= 1
        mask = (mask ^ (mask << j)) & 0xFFFFFFFF
    return a


def _chunks_of_block(blk):
    return ((blk + 1) * BLK + KEY_CHUNK - 1) // KEY_CHUNK


def _dsa_kernel(q_ref, qi_ref, tq_ref, qi_next_ref, tq_next_ref, tail_ref, k_ref, vt_ref, o_ref,
                score_ref, planes_ref, alive_ref, bias_ref, acc_ref, lg_a, lg_b, *, topk):
    i = pl.program_id(0)
    nxt = jnp.minimum(i + 1, pl.num_programs(0) - 1)
    n_chunks = _chunks_of_block(i)
    n_kc = score_ref.shape[0]

    def indexer(qi_blk_ref, tq_blk_ref, blk):
        w_t = tq_blk_ref[...].astype(F32).T[IDX_DIM:IDX_DIM + IDX_HEADS, :] * IDX_SCALE
        qi = qi_blk_ref[...]
        qi_rows = jnp.concatenate([qi[:, h * IDX_DIM:(h + 1) * IDX_DIM] for h in range(IDX_HEADS)], axis=0)

        def chunk_scores(c):
            off = pl.multiple_of(c * KEY_CHUNK, KEY_CHUNK)
            rel = jnp.maximum(_dot_nt(tail_ref[pl.ds(off, KEY_CHUNK), 0:IDX_DIM], qi_rows), 0.0)
            score = rel[:, 0:BLK] * w_t[0:1, :]
            for h in range(1, IDX_HEADS):
                score = score + rel[:, h * BLK:(h + 1) * BLK] * w_t[h:h + 1, :]
            return score

        def store_chunk(c, score):
            score = jnp.where(score == 0.0, 0.0, score)
            score_ref[c] = score
            bits = pltpu.bitcast(score, I32)
            u = bits ^ ((bits >> 31) | INT_MIN)
            for g in range(GROUPS_PER_CHUNK):
                words = [u[(g * 32 + j) * 8:(g * 32 + j + 1) * 8, :] for j in range(32)]
                planes = _bit_transpose32(words)
                for b in range(32):
                    planes_ref[c * GROUPS_PER_CHUNK + g, b] = planes[b]

        def full_chunk(c):
            store_chunk(c, chunk_scores(c))

        def last_chunk():
            last = _chunks_of_block(blk) - 1
            key_pos = last * KEY_CHUNK + lax.broadcasted_iota(I32, (KEY_CHUNK, BLK), 0)
            t_col = blk * BLK + lax.broadcasted_iota(I32, (KEY_CHUNK, BLK), 1)
            store_chunk(last, jnp.where(key_pos <= t_col, chunk_scores(last), -jnp.inf))

        return full_chunk, last_chunk

    @pl.when(i == 0)
    def _():
        indexer(qi_ref, tq_ref, 0)[1]()

    def init_alive(c, _):
        for g in range(GROUPS_PER_CHUNK):
            alive_ref[c * GROUPS_PER_CHUNK + g] = jnp.full((8, BLK), -1, I32)
        return 0

    lax.fori_loop(0, n_chunks, init_alive, 0)

    @pl.when(n_chunks % 2 == 1)
    def _():
        for g in range(GROUPS_PER_CHUNK):
            alive_ref[n_chunks * GROUPS_PER_CHUNK + g] = jnp.zeros((8, BLK), I32)
            planes_ref[n_chunks * GROUPS_PER_CHUNK + g] = jnp.zeros((32, 8, BLK), I32)

    def decide(state, counts, pair):
        above, t_bits = state[0], state[1]
        with_hi, with_both, only_lo = counts
        bit_hi = jnp.int32(1) << (31 - 2 * pair)
        bit_lo = jnp.int32(1) << (30 - 2 * pair)
        take_hi = (above + with_hi) >= topk
        above = jnp.where(take_hi, above, above + with_hi)
        with_lo = jnp.where(take_hi, with_both, only_lo)
        take_lo = (above + with_lo) >= topk
        above = jnp.where(take_lo, above, above + with_lo)
        t_bits = t_bits | jnp.where(take_hi, bit_hi, 0) | jnp.where(take_lo, bit_lo, 0)
        return above, t_bits, jnp.where(take_hi, 0, -1), jnp.where(take_lo, 0, -1)

    groups_per_step = 2 * GROUPS_PER_CHUNK

    def sweep(prev, cur, drop_hi, drop_lo):
        def step(p, accs):
            accs = list(accs)
            for g in range(groups_per_step):
                gi = p * groups_per_step + g
                alive = alive_ref[gi]
                if prev is not None:
                    alive = (alive & (planes_ref[gi, 2 * prev] ^ drop_hi)) & (planes_ref[gi, 2 * prev + 1] ^ drop_lo)
                    alive_ref[gi] = alive
                if cur is None:
                    accs[0] = accs[0] + lax.population_count(alive)
                else:
                    lo = planes_ref[gi, 2 * cur + 1]
                    with_hi = alive & planes_ref[gi, 2 * cur]
                    accs[0] = accs[0] + lax.population_count(with_hi)
                    accs[1] = accs[1] + lax.population_count(with_hi & lo)
                    accs[2] = accs[2] + lax.population_count((alive ^ with_hi) & lo)
            return tuple(accs)

        n_acc = 1 if cur is None else 3
        accs = lax.fori_loop(0, (n_chunks + 1) // 2, step, tuple(jnp.zeros((8, BLK), I32) for _ in range(n_acc)))
        return tuple(jnp.sum(a, axis=0, keepdims=True) for a in accs)

    zero = jnp.zeros((1, BLK), I32)
    state = decide((zero, zero), sweep(None, 0, None, None), 0)

    def pair_step(pair, state):
        return decide(state, sweep(pair - 1, pair, state[2], state[3]), pair)

    above, t_bits, drop_hi, drop_lo = lax.fori_loop(1, 16, pair_step, state)
    n_eq, = sweep(15, None, drop_hi, drop_lo)
    need = topk - above
    thr = t_bits ^ INT_MIN
    real = thr > KEY_OF_NEG_INF
    thr_f = jnp.where(real, _ordered_bits_to_float(thr), jnp.finfo(F32).min)

    def bias_chunk(c, _):
        bias_ref[c] = jnp.where(score_ref[c] >= thr_f, 0.0, NEG_BIG)
        return 0

    lax.fori_loop(0, n_chunks, bias_chunk, 0)

    has_tie = jnp.max(jnp.where(jnp.logical_and(real, n_eq > need), 1, 0)) > 0

    @pl.when(has_tie)
    def _():
        need_f = need.astype(F32)
        r = lax.broadcasted_iota(I32, (BLK, BLK), 0)
        cc = lax.broadcasted_iota(I32, (BLK, BLK), 1)
        upto = (cc <= r).astype(BF16)

        def tie_chunk(c, run):
            sc = score_ref[c]
            for u in range(KEY_CHUNK // BLK):
                st = sc[u * BLK:(u + 1) * BLK, :]
                eq = jnp.logical_and(st == thr_f, real)
                eqf = jnp.where(eq, 1.0, 0.0)
                rank = _dot(upto, eqf.astype(BF16)) + run
                sel = jnp.logical_or(st > thr_f, jnp.logical_and(eq, rank <= need_f))
                bias_ref[c, u * BLK:(u + 1) * BLK, :] = jnp.where(
                    real, jnp.where(sel, 0.0, NEG_BIG), jnp.where(st >= thr_f, 0.0, NEG_BIG))
                run = run + jnp.sum(eqf, axis=0, keepdims=True)
            return run

        lax.fori_loop(0, n_chunks, tie_chunk, jnp.zeros((1, BLK), F32))

    acc_ref[...] = jnp.zeros_like(acc_ref)
    bias_ref[n_kc] = jnp.full((KEY_CHUNK, BLK), NEG_BIG, F32)
    q = q_ref[...]

    def logits_into(lg, c):
        kc = jnp.minimum(c, n_chunks - 1)
        off = pl.multiple_of(kc * KEY_CHUNK, KEY_CHUNK)
        bias = bias_ref[jnp.where(c < n_chunks, c, n_kc)]
        for h in range(DSA_HEADS):
            hs = slice(h * HEAD_DIM, (h + 1) * HEAD_DIM)
            lg[h] = _dot_nt(k_ref[pl.ds(off, KEY_CHUNK), hs], q[:, hs]) + bias

    def reduce_from(lg, c, ms):
        vc = jnp.minimum(c, n_chunks - 1)
        new_m = []
        for h in range(DSA_HEADS):
            logits = lg[h]
            m_new = jnp.maximum(ms[h], jnp.max(logits, axis=0, keepdims=True))
            alpha = jnp.exp2(ms[h] - m_new)
            pr = jnp.exp2(logits - m_new).astype(BF16)
            acc_ref[h] = alpha * acc_ref[h] + _dot(vt_ref[vc, h * V_ROWS:(h + 1) * V_ROWS, :], pr)
            new_m.append(m_new)
        return tuple(new_m)

    next_full, next_last = indexer(qi_next_ref, tq_next_ref, nxt)
    last_full = _chunks_of_block(nxt) - 2
    n_pairs = (n_chunks + 1) // 2

    def attn_pair(pair, ms):
        c = 2 * pair
        logits_into(lg_b, c + 1)
        ms = reduce_from(lg_a, c, ms)
        next_full(jnp.maximum(jnp.minimum(c, last_full), 0))
        logits_into(lg_a, c + 2)
        ms = reduce_from(lg_b, c + 1, ms)
        next_full(jnp.maximum(jnp.minimum(c + 1, last_full), 0))
        return ms

    logits_into(lg_a, 0)
    lax.fori_loop(0, n_pairs, attn_pair, tuple(jnp.full((1, BLK), NEG_BIG, F32) for _ in range(DSA_HEADS)))
    for h in range(DSA_HEADS):
        weighted = acc_ref[h, 0:HEAD_DIM, :]
        denom = acc_ref[h, HEAD_DIM:HEAD_DIM + 1, :]
        o_ref[:, h * HEAD_DIM:(h + 1) * HEAD_DIM] = (weighted / denom).T.astype(o_ref.dtype)

    def rest(c, _):
        next_full(c)
        return 0

    lax.fori_loop(2 * n_pairs, last_full + 1, rest, 0)
    next_last()


def _dsa(p, k, vt):
    s = p.shape[0]
    nb = s // BLK
    topk = min(TOPK_MAX, s // 4)
    n_kc = s // KEY_CHUNK
    return pl.pallas_call(
        functools.partial(_dsa_kernel, topk=topk),
        grid=(nb,),
        in_specs=[pl.BlockSpec((BLK, D_DSA), lambda i: (i, OFF_D_Q // D_DSA)),
                  pl.BlockSpec((BLK, IDX_HEADS * IDX_DIM), lambda i: (i, OFF_D_QI // (IDX_HEADS * IDX_DIM))),
                  pl.BlockSpec((BLK, BLK), lambda i: (i, OFF_TAIL // BLK)),
                  pl.BlockSpec((BLK, IDX_HEADS * IDX_DIM),
                               lambda i: (jnp.minimum(i + 1, nb - 1), OFF_D_QI // (IDX_HEADS * IDX_DIM))),
                  pl.BlockSpec((BLK, BLK), lambda i: (jnp.minimum(i + 1, nb - 1), OFF_TAIL // BLK)),
                  _resident((s, BLK), lambda i: (0, OFF_TAIL // BLK)),
                  _resident((s, D_DSA), lambda i: (0, 0)),
                  _resident((n_kc, DSA_HEADS * V_ROWS, KEY_CHUNK), lambda i: (0, 0, 0))],
        out_specs=pl.BlockSpec((BLK, D_DSA), lambda i: (i, 0)),
        out_shape=jax.ShapeDtypeStruct((s, D_DSA), BF16),
        scratch_shapes=[pltpu.VMEM((n_kc, KEY_CHUNK, BLK), F32),
                        pltpu.VMEM((n_kc * GROUPS_PER_CHUNK, 32, 8, BLK), I32),
                        pltpu.VMEM((n_kc * GROUPS_PER_CHUNK, 8, BLK), I32),
                        pltpu.VMEM((n_kc + 1, KEY_CHUNK, BLK), F32),
                        pltpu.VMEM((DSA_HEADS, V_ROWS, BLK), F32),
                        pltpu.VMEM((DSA_HEADS, KEY_CHUNK, BLK), F32),
                        pltpu.VMEM((DSA_HEADS, KEY_CHUNK, BLK), F32)],
        compiler_params=_params("arbitrary"),
        name="dsa",
    )(p, p, p, p, p, p, k, vt)


def _layer_norm(r, g, b):
    mu = jnp.mean(r, axis=1, keepdims=True)
    var = jnp.mean(jnp.square(r - mu), axis=1, keepdims=True)
    return (r - mu) * lax.rsqrt(var + LN_EPS) * g + b


def _first_max_of4(vals):
    a, b, c, d = vals
    m = jnp.maximum(jnp.maximum(a, b), jnp.maximum(c, d))
    idx = jnp.where(a == m, 0, jnp.where(b == m, 1, jnp.where(c == m, 2, 3)))
    return m, idx


def _router_gates(logits_t):
    mx = jnp.max(logits_t, axis=0, keepdims=True)
    e = jnp.exp(logits_t - mx)
    probs = e / jnp.sum(e, axis=0, keepdims=True)
    rows = [probs[j:j + 1, :] for j in range(N_EXPERTS)]
    m1s, m2s, i1s, i2s, scores = [], [], [], [], []
    for g in range(N_GROUPS):
        vals = rows[g * EXPERTS_PER_GROUP:(g + 1) * EXPERTS_PER_GROUP]
        m1, i1 = _first_max_of4(vals)
        rest = [jnp.where(i1 == j, -1.0, vals[j]) for j in range(EXPERTS_PER_GROUP)]
        m2, i2 = _first_max_of4(rest)
        m1s.append(m1); m2s.append(m2); i1s.append(i1); i2s.append(i2); scores.append(m1 + m2)
    best, g_sel = _first_max_of4(scores)
    pick = lambda xs: jnp.where(g_sel == 0, xs[0], jnp.where(g_sel == 1, xs[1], jnp.where(g_sel == 2, xs[2], xs[3])))
    m1, m2, i1, i2 = pick(m1s), pick(m2s), pick(i1s), pick(i2s)
    den = m1 + m2
    w1, w2 = m1 / den, m2 / den
    e1 = g_sel * EXPERTS_PER_GROUP + i1
    e2 = g_sel * EXPERTS_PER_GROUP + i2
    gates = [jnp.where(e1 == j, w1, 0.0) + jnp.where(e2 == j, w2, 0.0) for j in range(N_EXPERTS)]
    return jnp.concatenate(gates, axis=0)


def _out_kernel(ysb_ref, yret_ref, ydsa_ref, wo_f32_ref, x_ref, g_ref, b_ref, wr_ref,
                x1_ref, gates_ref, gates_t_ref, cnt_ref, wo_ref):
    @pl.when(pl.program_id(0) == 0)
    def _():
        wo_ref[...] = wo_f32_ref[0].astype(BF16)

    mix = (_dot(ysb_ref[...], wo_ref[0:D_SB, :]) + _dot(yret_ref[...], wo_ref[D_SB:D_SB + D_RET, :])
           + _dot(ydsa_ref[...], wo_ref[D_SB + D_RET:, :]))
    x1 = _layer_norm(DEEPNORM_ALPHA * x_ref[...] + mix, g_ref[...], b_ref[...])
    x1_ref[...] = x1
    logits_t = lax.dot_general(wr_ref[...], x1, (((1,), (1,)), ((), ())),
                               preferred_element_type=F32, precision=lax.Precision.HIGHEST)
    gates_t = _router_gates(logits_t)
    tm = x1.shape[0]
    gates_t_ref[...] = gates_t
    padded = jnp.concatenate([gates_t, jnp.zeros((BLK - N_EXPERTS, tm), F32)], axis=0)
    gates_ref[...] = padded.T
    chosen = jnp.sum(jnp.where(gates_t > 0.0, 1, 0), axis=1, keepdims=True)
    cnt_ref[0] = jnp.broadcast_to(chosen, (N_EXPERTS, BLK))


def _out_proj(ysb, yret, ydsa, w_o, layer, x, g, b, wr_t):
    s = x.shape[0]
    tm = MOE_TILE
    row = lambda n: pl.BlockSpec((tm, n), lambda i: (i, 0))
    whole = lambda a: _resident(a.shape, lambda i: (0, 0))
    return pl.pallas_call(
        _out_kernel,
        grid=(s // tm,),
        in_specs=[row(D_SB), row(D_RET), row(D_DSA),
                  _resident((1,) + w_o.shape[1:], lambda i: (layer, 0, 0)),
                  row(D_MODEL), whole(g), whole(b), whole(wr_t)],
        out_specs=[row(D_MODEL), row(BLK), pl.BlockSpec((N_EXPERTS, tm), lambda i: (0, i)),
                   pl.BlockSpec((1, N_EXPERTS, BLK), lambda i: (i, 0, 0))],
        out_shape=[jax.ShapeDtypeStruct((s, D_MODEL), F32), jax.ShapeDtypeStruct((s, BLK), F32),
                   jax.ShapeDtypeStruct((N_EXPERTS, s), F32),
                   jax.ShapeDtypeStruct((s // tm, N_EXPERTS, BLK), I32)],
        scratch_shapes=[pltpu.VMEM(w_o.shape[1:], BF16)],
        compiler_params=_params("arbitrary"),
        name="out_proj",
    )(ysb, yret, ydsa, w_o, x, g, b, wr_t)


LOCAL_ROWS = 2 * MOE_TILE + N_EXPERTS * WIN


def _round_up(x, m):
    return (x + m - 1) // m * m


def _sorted_rows(s):
    n_tiles = s // MOE_TILE
    return _round_up(2 * s + n_tiles * N_EXPERTS * (WIN - 1) + N_EXPERTS * (ROW_TILE - 1), ROW_TILE)


def _moe_plan(cnt, s):
    n_row_tiles = _sorted_rows(s) // ROW_TILE
    seg = _round_up(cnt, WIN)
    rows_e = jnp.sum(seg, axis=0)
    region = _round_up(rows_e, ROW_TILE)
    region_off = jnp.cumsum(region) - region
    dest = region_off[None, :] + jnp.cumsum(seg, axis=0) - seg
    tiles_e = region // ROW_TILE
    tile_end = jnp.cumsum(tiles_e)
    k = jnp.arange(n_row_tiles, dtype=I32)
    tile_expert = jnp.minimum(jnp.sum((k[:, None] >= tile_end[None, :]).astype(I32), axis=1), N_EXPERTS - 1)
    first = (tile_end - tiles_e)[tile_expert]
    valid = jnp.clip(rows_e[tile_expert] - (k - first) * ROW_TILE, 0, ROW_TILE)
    tile_valid = jnp.where(k < tile_end[-1], valid, 0).astype(I32)
    fill = jnp.concatenate([region_off + rows_e, (region - rows_e) // WIN,
                            tile_end[-1:], n_row_tiles - tile_end[-1:]]).astype(I32)
    used = rows_e > 0
    e_id = jnp.arange(N_EXPERTS, dtype=I32)
    buffer_e = (jnp.cumsum(used) - used) % 2
    later_used = jnp.where(jnp.logical_and(used[None, :], e_id[None, :] > e_id[:, None]), e_id[None, :], N_EXPERTS)
    next_e = jnp.min(later_used, axis=1)
    next_e = jnp.where(next_e == N_EXPERTS, -1, next_e)
    tile_info = jnp.stack([tile_expert, tile_valid, buffer_e[tile_expert], next_e[tile_expert]]).astype(I32)
    return cnt.reshape(-1).astype(I32), dest.reshape(-1).astype(I32), fill, tile_info.reshape(-1)


def _slot_offsets(cnt_sm, tile):
    offs, o = [], 0
    for e in range(N_EXPERTS):
        offs.append(o)
        o = o + _round_up(cnt_sm[tile * N_EXPERTS + e], WIN)
    return offs


def _window_copies(cnt_sm, dest_sm, tile, offs, local_ref, sorted_hbm, sem, to_sorted):
    total = 0
    for e in range(N_EXPERTS):
        n_win = (cnt_sm[tile * N_EXPERTS + e] + WIN - 1) // WIN
        base_local, base_sorted = offs[e], dest_sm[tile * N_EXPERTS + e]

        def issue(j, _, base_local=base_local, base_sorted=base_sorted):
            loc = local_ref.at[pl.ds(pl.multiple_of(base_local + j * WIN, WIN), WIN)]
            srt = sorted_hbm.at[pl.ds(pl.multiple_of(base_sorted + j * WIN, WIN), WIN)]
            if to_sorted:
                pltpu.make_async_copy(loc, srt, sem).start()
            else:
                pltpu.make_async_copy(srt, loc, sem).start()
            return 0

        lax.fori_loop(0, n_win, issue, 0)
        total = total + n_win
    return total


def _n_windows(cnt_sm, tile):
    total = 0
    for e in range(N_EXPERTS):
        total = total + (cnt_sm[tile * N_EXPERTS + e] + WIN - 1) // WIN
    return total


def _wait_windows(total, local_ref, sorted_hbm, sem):
    def wait(j, _):
        pltpu.make_async_copy(local_ref.at[pl.ds(0, WIN)], sorted_hbm.at[pl.ds(0, WIN)], sem).wait()
        return 0

    lax.fori_loop(0, total, wait, 0)


def _zero_fill(fill_sm, xs_hbm, zero_ref, sem_win, sem_tile):
    zero_ref[...] = jnp.zeros_like(zero_ref)
    win_copy = lambda row: pltpu.make_async_copy(
        zero_ref.at[pl.ds(0, WIN)], xs_hbm.at[pl.ds(pl.multiple_of(row, WIN), WIN)], sem_win)
    tile_copy = lambda row: pltpu.make_async_copy(
        zero_ref, xs_hbm.at[pl.ds(pl.multiple_of(row, ROW_TILE), ROW_TILE)], sem_tile)
    n_pad = 0
    for e in range(N_EXPERTS):
        first, n_win = fill_sm[e], fill_sm[N_EXPERTS + e]

        def issue(j, _, first=first):
            win_copy(first + j * WIN).start()
            return 0

        lax.fori_loop(0, n_win, issue, 0)
        n_pad = n_pad + n_win
    first_tile, n_tail = fill_sm[2 * N_EXPERTS], fill_sm[2 * N_EXPERTS + 1]

    def issue_tile(j, _):
        tile_copy((first_tile + j) * ROW_TILE).start()
        return 0

    def wait_win(j, _):
        win_copy(0).wait()
        return 0

    def wait_tile(j, _):
        tile_copy(0).wait()
        return 0

    lax.fori_loop(0, n_tail, issue_tile, 0)
    lax.fori_loop(0, n_pad, wait_win, 0)
    lax.fori_loop(0, n_tail, wait_tile, 0)


def _dispatch_kernel(cnt_sm, dest_sm, fill_sm, x_ref, gt_ref, xs_hbm, local_ref, zero_ref, sem, sem_win, sem_tile):
    tile = pl.program_id(0)

    @pl.when(tile == 0)
    def _():
        _zero_fill(fill_sm, xs_hbm, zero_ref, sem_win, sem_tile)

    offs = _slot_offsets(cnt_sm, tile)
    chosen = gt_ref[...] > 0.0
    t_r = lax.broadcasted_iota(I32, (MOE_TILE, MOE_TILE), 0)
    t_c = lax.broadcasted_iota(I32, (MOE_TILE, MOE_TILE), 1)
    earlier = _dot(jnp.where(chosen, 1.0, 0.0).astype(BF16), (t_r < t_c).astype(BF16))
    e_id = lax.broadcasted_iota(I32, (N_EXPERTS, 1), 0)
    slot = jnp.zeros((N_EXPERTS, 1), I32)
    for e in range(N_EXPERTS):
        slot = jnp.where(e_id == e, offs[e], slot)
    pos = earlier + slot.astype(F32)
    p_lo = jnp.min(jnp.where(chosen, pos, float(LOCAL_ROWS)), axis=0, keepdims=True).astype(I32)
    p_hi = jnp.max(jnp.where(chosen, pos, -1.0), axis=0, keepdims=True).astype(I32)
    row = lax.broadcasted_iota(I32, (LOCAL_ROWS, MOE_TILE), 0)
    onehot = jnp.where(row == p_lo, 1.0, jnp.where(row == p_hi, 1.0, 0.0)).astype(BF16)
    buf = tile % 2
    mine, mine_sem = local_ref.at[buf], sem.at[buf]
    other, other_sem = local_ref.at[1 - buf], sem.at[1 - buf]
    mine[...] = _dot(onehot, x_ref[...].astype(BF16)).astype(BF16)
    total = _window_copies(cnt_sm, dest_sm, tile, offs, mine, xs_hbm, mine_sem, True)

    @pl.when(tile >= 1)
    def _():
        _wait_windows(_n_windows(cnt_sm, tile - 1), other, xs_hbm, other_sem)

    @pl.when(tile == pl.num_programs(0) - 1)
    def _():
        _wait_windows(total, mine, xs_hbm, mine_sem)


def _dispatch(x1, gates_t, cnt_flat, dest_flat, fill):
    s = x1.shape[0]
    return pl.pallas_call(
        _dispatch_kernel,
        grid_spec=pltpu.PrefetchScalarGridSpec(
            num_scalar_prefetch=3,
            grid=(s // MOE_TILE,),
            in_specs=[pl.BlockSpec((MOE_TILE, D_MODEL), lambda i, *_: (i, 0)),
                      pl.BlockSpec((N_EXPERTS, MOE_TILE), lambda i, *_: (0, i))],
            out_specs=pl.BlockSpec(memory_space=pl.ANY),
            scratch_shapes=[pltpu.VMEM((2, LOCAL_ROWS, D_MODEL), BF16), pltpu.VMEM((ROW_TILE, D_MODEL), BF16),
                            pltpu.SemaphoreType.DMA((2,)), pltpu.SemaphoreType.DMA(()),
                            pltpu.SemaphoreType.DMA(())]),
        out_shape=jax.ShapeDtypeStruct((_sorted_rows(s), D_MODEL), BF16),
        compiler_params=_params("arbitrary"),
        name="moe_dispatch",
    )(cnt_flat, dest_flat, fill, x1, gates_t)


def _expert_kernel(info_sm, xs_ref, wg_hbm, wu_hbm, wd_hbm, y_ref, wg_f, wu_f, wd_f, wg_b, wu_b, wd_b, sem, *,
                   layer):
    k = pl.program_id(0)
    n = pl.num_programs(0)
    expert, valid, buf, next_expert = info_sm[k], info_sm[n + k], info_sm[2 * n + k], info_sm[3 * n + k]
    new_expert = jnp.logical_or(k == 0, expert != info_sm[jnp.maximum(k - 1, 0)])

    def weight_copies(e, b):
        return (pltpu.make_async_copy(wg_hbm.at[layer, e], wg_f.at[b], sem.at[b, 0]),
                pltpu.make_async_copy(wu_hbm.at[layer, e], wu_f.at[b], sem.at[b, 1]),
                pltpu.make_async_copy(wd_hbm.at[layer, e], wd_f.at[b], sem.at[b, 2]))

    @pl.when(jnp.logical_and(k == 0, valid > 0))
    def _():
        for c in weight_copies(expert, buf):
            c.start()

    @pl.when(jnp.logical_and(valid > 0, new_expert))
    def _():
        for c in weight_copies(expert, buf):
            c.wait()
        wg_b[...] = wg_f[buf].astype(BF16)
        wu_b[...] = wu_f[buf].astype(BF16)
        wd_b[...] = wd_f[buf].astype(BF16)

        @pl.when(next_expert >= 0)
        def _():
            for c in weight_copies(next_expert, 1 - buf):
                c.start()

    @pl.when(valid > 0)
    def _():
        x = xs_ref[...]
        hg = _dot(x, wg_b[...])
        hu = _dot(x, wu_b[...])
        act = hg * jax.nn.sigmoid(hg) * hu
        y_ref[...] = _dot(act.astype(BF16), wd_b[...]).astype(y_ref.dtype)

    @pl.when(valid == 0)
    def _():
        y_ref[...] = jnp.zeros_like(y_ref)


def _experts(xs, tile_info, w_gate, w_up, w_down, layer):
    rows = xs.shape[0]
    hbm = pl.BlockSpec(memory_space=pl.ANY)
    up_shape, down_shape = (D_MODEL, D_FF_EXPERT), (D_FF_EXPERT, D_MODEL)
    return pl.pallas_call(
        functools.partial(_expert_kernel, layer=layer),
        grid_spec=pltpu.PrefetchScalarGridSpec(
            num_scalar_prefetch=1,
            grid=(rows // ROW_TILE,),
            in_specs=[pl.BlockSpec((ROW_TILE, D_MODEL), lambda k, info: (k, 0)), hbm, hbm, hbm],
            out_specs=pl.BlockSpec((ROW_TILE, D_MODEL), lambda k, info: (k, 0)),
            scratch_shapes=[pltpu.VMEM((2,) + up_shape, F32), pltpu.VMEM((2,) + up_shape, F32),
                            pltpu.VMEM((2,) + down_shape, F32),
                            pltpu.VMEM(up_shape, BF16), pltpu.VMEM(up_shape, BF16), pltpu.VMEM(down_shape, BF16),
                            pltpu.SemaphoreType.DMA((2, 3))]),
        out_shape=jax.ShapeDtypeStruct((rows, D_MODEL), BF16),
        compiler_params=_params("arbitrary"),
        name="moe_experts",
    )(tile_info, xs, w_gate, w_up, w_down)


def _combine_kernel(cnt_sm, dest_sm, x_ref, gates_ref, y_hbm, g_ref, b_ref, o_ref, local_ref, sem):
    tile = pl.program_id(0)

    buf = tile % 2
    mine, mine_sem = local_ref.at[buf], sem.at[buf]
    offs = _slot_offsets(cnt_sm, tile)

    @pl.when(tile == 0)
    def _():
        local_ref[...] = jnp.zeros_like(local_ref)
        _window_copies(cnt_sm, dest_sm, tile, offs, mine, y_hbm, mine_sem, False)

    @pl.when(tile + 1 < pl.num_programs(0))
    def _():
        _window_copies(cnt_sm, dest_sm, tile + 1, _slot_offsets(cnt_sm, tile + 1),
                       local_ref.at[1 - buf], y_hbm, sem.at[1 - buf], False)

    gates = gates_ref[...]
    chosen = gates > 0.0
    t_r = lax.broadcasted_iota(I32, (MOE_TILE, MOE_TILE), 0)
    t_c = lax.broadcasted_iota(I32, (MOE_TILE, MOE_TILE), 1)
    earlier = _dot((t_c < t_r).astype(BF16), jnp.where(chosen, 1.0, 0.0).astype(BF16))
    e_id = lax.broadcasted_iota(I32, (1, BLK), 1)
    slot = jnp.zeros((1, BLK), I32)
    for e in range(N_EXPERTS):
        slot = jnp.where(e_id == e, offs[e], slot)
    pos = jnp.where(chosen, earlier + slot.astype(F32), -1.0)
    p_lo = jnp.min(jnp.where(chosen, pos, float(LOCAL_ROWS)), axis=1, keepdims=True)
    p_hi = jnp.max(pos, axis=1, keepdims=True)
    w_lo = jnp.sum(jnp.where(pos == p_lo, gates, 0.0), axis=1, keepdims=True)
    w_hi = jnp.sum(jnp.where(pos == p_hi, gates, 0.0), axis=1, keepdims=True)
    col = lax.broadcasted_iota(I32, (MOE_TILE, LOCAL_ROWS), 1)
    weights = jnp.where(col == p_lo.astype(I32), w_lo, jnp.where(col == p_hi.astype(I32), w_hi, 0.0))

    _wait_windows(_n_windows(cnt_sm, tile), mine, y_hbm, mine_sem)
    ffn = _dot(weights.astype(BF16), mine[...])
    o_ref[...] = _layer_norm(DEEPNORM_ALPHA * x_ref[...] + ffn, g_ref[...], b_ref[...])


def _combine(x1, gates, y, cnt_flat, dest_flat, g, b):
    s = x1.shape[0]
    row = lambda n: pl.BlockSpec((MOE_TILE, n), lambda i, *_: (i, 0))
    vec = pl.BlockSpec((1, D_MODEL), lambda i, *_: (0, 0))
    return pl.pallas_call(
        _combine_kernel,
        grid_spec=pltpu.PrefetchScalarGridSpec(
            num_scalar_prefetch=2,
            grid=(s // MOE_TILE,),
            in_specs=[row(D_MODEL), row(BLK), pl.BlockSpec(memory_space=pl.ANY), vec, vec],
            out_specs=row(D_MODEL),
            scratch_shapes=[pltpu.VMEM((2, LOCAL_ROWS, D_MODEL), BF16), pltpu.SemaphoreType.DMA((2,))]),
        out_shape=jax.ShapeDtypeStruct((s, D_MODEL), F32),
        compiler_params=_params("arbitrary"),
        name="moe_combine",
    )(cnt_flat, dest_flat, x1, gates, y, g, b)


def _moe(x1, gates, gates_t, cnt, w_gate, w_up, w_down, g, b, layer):
    s = x1.shape[0]
    cnt_flat, dest_flat, fill, tile_info = _moe_plan(cnt[:, :, 0], s)
    xs = _dispatch(x1, gates_t, cnt_flat, dest_flat, fill)
    y = _experts(xs, tile_info, w_gate, w_up, w_down, layer)
    return _combine(x1, gates, y, cnt_flat, dest_flat, g, b)


def _reorder_w_in(w_in):
    pieces, o = {}, 0
    for name, width in (("sb", 3 * D_SB), ("ret", 4 * D_RET), ("d_q", D_DSA), ("d_ckv", KV_RANK),
                        ("d_qi", IDX_HEADS * IDX_DIM)):
        pieces[name] = w_in[:, :, o:o + width]
        o += width
    parts = [pieces["sb"], pieces["d_q"], pieces["ret"], pieces["d_qi"], pieces["d_ckv"], w_in[:, :, o:]]
    width = sum(a.shape[2] for a in parts)
    parts.append(jnp.zeros(w_in.shape[:2] + (D_PROJ - width,), w_in.dtype))
    return jnp.concatenate(parts, axis=2).astype(BF16)


def kernel(x, w_in, w_kv_up, kv_norm_g, ret_gn_g, w_o, ln1_g, ln1_b, w_router, w_gate, w_up, w_down,
           ln2_g, ln2_b):
    b, s, _ = x.shape
    assert b == 1 and s % KEY_CHUNK == 0
    h = x[0]
    tables = _retention_tables(s)
    wr_t = w_router.T
    w_proj = _reorder_w_in(w_in)
    for l in range(DEPTH):
        p = _proj(h, w_proj, l)
        y_sb = _stick_breaking(p)
        y_ret = _retention(p, ret_gn_g[l][None, :], tables)
        w_kv = w_kv_up[l].reshape(KV_RANK, DSA_HEADS, 2, HEAD_DIM)
        w_kv = jnp.concatenate([w_kv[:, :, 0, :].reshape(KV_RANK, D_DSA),
                                w_kv[:, :, 1, :].reshape(KV_RANK, D_DSA)], axis=1).astype(BF16)
        k_dsa, vt_dsa = _kv_up(p, kv_norm_g[l][None, :], w_kv)
        y_dsa = _dsa(p, k_dsa, vt_dsa)
        x1, gates, gates_t, cnt = _out_proj(y_sb, y_ret, y_dsa, w_o, l, h, ln1_g[l][None, :], ln1_b[l][None, :],
                                            wr_t)
        h = _moe(x1, gates, gates_t, cnt, w_gate, w_up, w_down, ln2_g[l][None, :], ln2_b[l][None, :], l)
    return h[None]
```

```python
import functools

import numpy as np
import jax
import jax.numpy as jnp
from jax import lax
from jax.experimental import pallas as pl
from jax.experimental.pallas import tpu as pltpu

F32 = jnp.float32
BF16 = jnp.bfloat16
I32 = jnp.int32

D_MODEL = 2048
HEAD_DIM = 128
SB_HEADS = 6
RET_HEADS = 4
DSA_HEADS = 6
D_SB = SB_HEADS * HEAD_DIM
D_RET = RET_HEADS * HEAD_DIM
D_DSA = DSA_HEADS * HEAD_DIM
KV_RANK = 256
IDX_HEADS = 8
IDX_DIM = 64
IDX_SCALE = IDX_DIM ** -0.5 * IDX_HEADS ** -0.5
TOPK_MAX = 256
BLK = 128
N_EXPERTS = 16
N_GROUPS = 4
EXPERTS_PER_GROUP = N_EXPERTS // N_GROUPS
D_FF_EXPERT = 512
LN_EPS = 1e-5
RMS_EPS = 1e-6
GN_EPS = 1e-6
DEPTH = 2
DEEPNORM_ALPHA = (2 * DEPTH) ** 0.25

OFF_SB_Q = 0
OFF_SB_K = OFF_SB_Q + D_SB
OFF_SB_V = OFF_SB_K + D_SB
OFF_D_Q = OFF_SB_V + D_SB
OFF_R_Q = OFF_D_Q + D_DSA
OFF_R_K = OFF_R_Q + D_RET
OFF_R_V = OFF_R_K + D_RET
OFF_R_G = OFF_R_V + D_RET
OFF_D_QI = OFF_R_G + D_RET
OFF_D_CKV = OFF_D_QI + IDX_HEADS * IDX_DIM
OFF_TAIL = OFF_D_CKV + KV_RANK
D_PROJ = 6144

KEY_CHUNK = 512
MOE_TILE = 512
WIN = 16
ROW_TILE = 256
VMEM_LIMIT = 56 * 1024 * 1024
LOG2_E = 1.4426950408889634
NEG_BIG = -1e30
EXP_UNDERFLOW = -87.4
KEY_OF_NEG_INF = -2139095041
INT_MIN = -2147483648


def _dot(a, b):
    return jnp.dot(a, b, preferred_element_type=F32)


def _dot_nt(a, b):
    return lax.dot_general(a, b, (((1,), (1,)), ((), ())), preferred_element_type=F32)


def _dot_tn(a, b):
    return lax.dot_general(a, b, (((0,), (0,)), ((), ())), preferred_element_type=F32)


def _params(*sem):
    return pltpu.CompilerParams(dimension_semantics=sem, vmem_limit_bytes=VMEM_LIMIT)


def _resident(shape, index_map):
    return pl.BlockSpec(shape, index_map, pipeline_mode=pl.Buffered(1))


def _proj_kernel(x_ref, w_ref, o_ref):
    o_ref[...] = _dot(x_ref[...].astype(BF16), w_ref[0]).astype(o_ref.dtype)


def _proj(x, w, layer):
    s, d = x.shape
    n = w.shape[2]
    tm = min(1024, s)
    tn = 1536
    return pl.pallas_call(
        _proj_kernel,
        grid=(s // tm, n // tn),
        in_specs=[pl.BlockSpec((tm, d), lambda i, j: (i, 0)),
                  pl.BlockSpec((1, d, tn), lambda i, j: (layer, 0, j))],
        out_specs=pl.BlockSpec((tm, tn), lambda i, j: (i, j)),
        out_shape=jax.ShapeDtypeStruct((s, n), BF16),
        compiler_params=_params("parallel", "arbitrary"),
        name="proj",
    )(x, w)


def _sb_kernel(q_ref, k_ref, v_ref, o_ref, acc_ref):
    i = pl.program_id(0)
    q = q_ref[...]
    scale = HEAD_DIM ** -0.5
    key_pos = lax.broadcasted_iota(I32, (BLK, BLK), 0)
    qry_pos = lax.broadcasted_iota(I32, (BLK, BLK), 1)
    later = (qry_pos > key_pos).astype(BF16)

    def key_tile(j, first_query, cs):
        off = pl.multiple_of(j * BLK, BLK)
        strict = (off + key_pos) < (first_query + qry_pos)
        pvs, new_cs = [], []
        for h in range(SB_HEADS):
            hs = slice(h * HEAD_DIM, (h + 1) * HEAD_DIM)
            z = _dot_nt(k_ref[pl.ds(off, BLK), hs], q[:, hs]) * scale
            sp = jnp.maximum(z, 0.0) + jnp.log1p(jnp.exp(-jnp.abs(z)))
            log_rem = jnp.where(strict, -sp, 0.0)
            hi = log_rem.astype(BF16)
            lo = (log_rem - hi.astype(F32)).astype(BF16)
            after = _dot(later, hi) + _dot(later, lo)
            a = jnp.where(strict, jnp.exp(z - sp + after + cs[h]), 0.0)
            pvs.append(_dot_tn(a.astype(BF16), v_ref[pl.ds(off, BLK), hs]))
            new_cs.append(cs[h] + jnp.sum(log_rem, axis=0, keepdims=True))
        return pvs, tuple(new_cs)

    cs = tuple(jnp.zeros((1, BLK), F32) for _ in range(SB_HEADS))
    pv_a, cs = key_tile(i, i * BLK, cs)
    pv_b, cs = key_tile(jnp.maximum(i - 1, 0), jnp.where(i >= 1, i * BLK, -BLK), cs)
    for h in range(SB_HEADS):
        acc_ref[h] = pv_a[h] + pv_b[h]

    def cond(carry):
        j, cs = carry
        c_max = functools.reduce(jnp.maximum, cs)
        return jnp.logical_and(j >= 0, jnp.max(c_max) > EXP_UNDERFLOW)

    def body(carry):
        j, cs = carry
        pvs, cs = key_tile(j, i * BLK, cs)
        for h in range(SB_HEADS):
            acc_ref[h] += pvs[h]
        return j - 1, cs

    lax.while_loop(cond, body, (i - 2, cs))
    for h in range(SB_HEADS):
        o_ref[:, h * HEAD_DIM:(h + 1) * HEAD_DIM] = acc_ref[h].astype(o_ref.dtype)


def _stick_breaking(p):
    s = p.shape[0]
    nb = s // BLK
    return pl.pallas_call(
        _sb_kernel,
        grid=(nb,),
        in_specs=[pl.BlockSpec((BLK, D_SB), lambda i: (i, OFF_SB_Q // D_SB)),
                  _resident((s, D_SB), lambda i: (0, OFF_SB_K // D_SB)),
                  _resident((s, D_SB), lambda i: (0, OFF_SB_V // D_SB))],
        out_specs=pl.BlockSpec((BLK, D_SB), lambda i: (i, 0)),
        out_shape=jax.ShapeDtypeStruct((s, D_SB), BF16),
        scratch_shapes=[pltpu.VMEM((SB_HEADS, BLK, HEAD_DIM), F32)],
        compiler_params=_params("arbitrary"),
        name="stick_breaking",
    )(p, p, p)


def _ret_kernel(q_ref, k_ref, v_ref, g_ref, cos_ref, sin_ref, intra_ref, qd_ref, kd_ref, cd_ref,
                gn_ref, o_ref, state_ref):
    n = pl.program_id(0)

    @pl.when(n == 0)
    def _():
        state_ref[...] = jnp.zeros_like(state_ref)

    cos = cos_ref[...]
    sin = sin_ref[...]
    for h in range(RET_HEADS):
        hs = slice(h * HEAD_DIM, (h + 1) * HEAD_DIM)
        q = q_ref[:, hs].astype(F32)
        k = k_ref[:, hs].astype(F32)
        v = v_ref[:, hs]
        half = HEAD_DIM // 2
        swap = lambda t: jnp.concatenate([t[:, half:], t[:, :half]], axis=1)
        qr = q * cos + swap(q) * sin
        kr = (k * cos + swap(k) * sin) * (HEAD_DIM ** -0.5)
        scores = _dot_nt(qr.astype(BF16), kr.astype(BF16)) * intra_ref[h]
        state = state_ref[h]
        o = (_dot(scores.astype(BF16), v)
             + _dot((qr * qd_ref[h]).astype(BF16), state.astype(BF16)))
        state_ref[h] = cd_ref[h] * state + _dot_tn((kr * kd_ref[h]).astype(BF16), v)
        mu = jnp.mean(o, axis=1, keepdims=True)
        var = jnp.mean(jnp.square(o - mu), axis=1, keepdims=True)
        on = (o - mu) * lax.rsqrt(var + GN_EPS) * gn_ref[:, hs]
        g = g_ref[:, hs].astype(F32)
        o_ref[:, hs] = (g * jax.nn.sigmoid(g) * on).astype(o_ref.dtype)


def _retention_tables(s):
    f32 = np.float32
    half = HEAD_DIM // 2
    pos = np.arange(s, dtype=np.float64)
    theta = 10000.0 ** (-np.linspace(0.0, 1.0, half))
    ang = pos[:, None] * theta[None, :]
    cos, sin = np.cos(ang).astype(f32), np.sin(ang).astype(f32)
    cos2 = np.concatenate([cos, cos], axis=1)
    sin2 = np.concatenate([-sin, sin], axis=1)
    log_gamma = np.log1p(-(2.0 ** (-5.0 - np.arange(RET_HEADS, dtype=np.float64))))
    idx = np.arange(BLK, dtype=np.float64)
    diff = idx[:, None] - idx[None, :]
    intra = np.where(diff >= 0, np.exp(np.maximum(diff, 0.0)[None] * log_gamma[:, None, None]), 0.0).astype(f32)
    q_decay = np.exp((idx[None, :] + 1.0) * log_gamma[:, None]).astype(f32)
    k_decay = np.exp((BLK - 1.0 - idx[None, :]) * log_gamma[:, None]).astype(f32)
    chunk_decay = np.exp(BLK * log_gamma).astype(f32)
    full = (RET_HEADS, BLK, HEAD_DIM)
    return tuple(jnp.asarray(np.ascontiguousarray(t)) for t in (
        cos2, sin2, intra,
        np.broadcast_to(q_decay[:, :, None], full),
        np.broadcast_to(k_decay[:, :, None], full),
        np.broadcast_to(chunk_decay[:, None, None], full)))


def _retention(p, gn_g, tables):
    s = p.shape[0]
    nc = s // BLK
    cos2, sin2, intra, qd, kd, cd = tables
    col = lambda off: pl.BlockSpec((BLK, D_RET), lambda n: (n, off // D_RET))
    per_head = pl.BlockSpec((RET_HEADS, BLK, HEAD_DIM), lambda n: (0, 0, 0))
    pos_spec = pl.BlockSpec((BLK, HEAD_DIM), lambda n: (n, 0))
    return pl.pallas_call(
        _ret_kernel,
        grid=(nc,),
        in_specs=[col(OFF_R_Q), col(OFF_R_K), col(OFF_R_V), col(OFF_R_G),
                  pos_spec, pos_spec, per_head, per_head, per_head, per_head,
                  pl.BlockSpec((1, D_RET), lambda n: (0, 0))],
        out_specs=pl.BlockSpec((BLK, D_RET), lambda n: (n, 0)),
        out_shape=jax.ShapeDtypeStruct((s, D_RET), BF16),
        scratch_shapes=[pltpu.VMEM((RET_HEADS, HEAD_DIM, HEAD_DIM), F32)],
        compiler_params=_params("arbitrary"),
        name="retention",
    )(p, p, p, p, cos2, sin2, intra, qd, kd, cd, gn_g)


def _kv_up_kernel(c_ref, g_ref, w_ref, k_ref, vt_ref):
    c = c_ref[...].astype(F32)
    y = c * lax.rsqrt(jnp.mean(jnp.square(c), axis=1, keepdims=True) + RMS_EPS) * g_ref[...]
    kv = _dot(y.astype(BF16), w_ref[...])
    k_ref[...] = (kv[:, :D_DSA] * (HEAD_DIM ** -0.5 * LOG2_E)).astype(k_ref.dtype)
    v_t = kv[:, D_DSA:].T
    ones = jnp.ones((V_ROWS - HEAD_DIM, v_t.shape[1]), vt_ref.dtype)
    for h in range(DSA_HEADS):
        vt_ref[0, h * V_ROWS:h * V_ROWS + HEAD_DIM, :] = v_t[h * HEAD_DIM:(h + 1) * HEAD_DIM, :].astype(vt_ref.dtype)
        vt_ref[0, h * V_ROWS + HEAD_DIM:(h + 1) * V_ROWS, :] = ones


def _kv_up(p, g, w):
    s = p.shape[0]
    n = w.shape[1]
    return pl.pallas_call(
        _kv_up_kernel,
        grid=(s // KEY_CHUNK,),
        in_specs=[pl.BlockSpec((KEY_CHUNK, KV_RANK), lambda i: (i, OFF_D_CKV // KV_RANK)),
                  pl.BlockSpec((1, KV_RANK), lambda i: (0, 0)),
                  pl.BlockSpec((KV_RANK, n), lambda i: (0, 0))],
        out_specs=[pl.BlockSpec((KEY_CHUNK, D_DSA), lambda i: (i, 0)),
                   pl.BlockSpec((1, DSA_HEADS * V_ROWS, KEY_CHUNK), lambda i: (i, 0, 0))],
        out_shape=[jax.ShapeDtypeStruct((s, D_DSA), BF16),
                   jax.ShapeDtypeStruct((s // KEY_CHUNK, DSA_HEADS * V_ROWS, KEY_CHUNK), BF16)],
        compiler_params=_params("parallel"),
        name="kv_up",
    )(p, g, w)


def _ordered_bits_to_float(u):
    return pltpu.bitcast(u ^ ((u >> 31) & 0x7FFFFFFF), F32)


V_ROWS = HEAD_DIM + 16
GROUPS_PER_CHUNK = KEY_CHUNK // (32 * 8)


def _bit_transpose32(words):
    a = list(words)
    j, mask = 16, 0x0000FFFF
    while j:
        k = 0
        while k < 32:
            t = (a[k] ^ (a[k + j] >> j)) & mask
            a[k] = a[k] ^ t
            a[k + j] = a[k + j] ^ (t << j)
            k = (k + j + 1) & ~j
        j >>= 1
        mask = (mask ^ (mask << j)) & 0xFFFFFFFF
    return a


def _chunks_of_block(blk):
    return ((blk + 1) * BLK + KEY_CHUNK - 1) // KEY_CHUNK


def _dsa_kernel(q_ref, qi_ref, tq_ref, qi_next_ref, tq_next_ref, tail_ref, k_ref, vt_ref, o_ref,
                score_ref, planes_ref, alive_ref, bias_ref, acc_ref, lg_a, lg_b, *, topk):
    i = pl.program_id(0)
    nxt = jnp.minimum(i + 1, pl.num_programs(0) - 1)
    n_chunks = _chunks_of_block(i)
    n_kc = score_ref.shape[0]

    def indexer(qi_blk_ref, tq_blk_ref, blk):
        w_t = tq_blk_ref[...].astype(F32).T[IDX_DIM:IDX_DIM + IDX_HEADS, :] * IDX_SCALE
        qi = qi_blk_ref[...]
        qi_rows = jnp.concatenate([qi[:, h * IDX_DIM:(h + 1) * IDX_DIM] for h in range(IDX_HEADS)], axis=0)

        def chunk_scores(c):
            off = pl.multiple_of(c * KEY_CHUNK, KEY_CHUNK)
            rel = jnp.maximum(_dot_nt(tail_ref[pl.ds(off, KEY_CHUNK), 0:IDX_DIM], qi_rows), 0.0)
            score = rel[:, 0:BLK] * w_t[0:1, :]
            for h in range(1, IDX_HEADS):
                score = score + rel[:, h * BLK:(h + 1) * BLK] * w_t[h:h + 1, :]
            return score

        def store_chunk(c, score):
            score = jnp.where(score == 0.0, 0.0, score)
            score_ref[c] = score
            bits = pltpu.bitcast(score, I32)
            u = bits ^ ((bits >> 31) | INT_MIN)
            for g in range(GROUPS_PER_CHUNK):
                words = [u[(g * 32 + j) * 8:(g * 32 + j + 1) * 8, :] for j in range(32)]
                planes = _bit_transpose32(words)
                for b in range(32):
                    planes_ref[c * GROUPS_PER_CHUNK + g, b] = planes[b]

        def full_chunk(c):
            store_chunk(c, chunk_scores(c))

        def last_chunk():
            last = _chunks_of_block(blk) - 1
            key_pos = last * KEY_CHUNK + lax.broadcasted_iota(I32, (KEY_CHUNK, BLK), 0)
            t_col = blk * BLK + lax.broadcasted_iota(I32, (KEY_CHUNK, BLK), 1)
            store_chunk(last, jnp.where(key_pos <= t_col, chunk_scores(last), -jnp.inf))

        return full_chunk, last_chunk

    @pl.when(i == 0)
    def _():
        indexer(qi_ref, tq_ref, 0)[1]()

    def init_alive(c, _):
        for g in range(GROUPS_PER_CHUNK):
            alive_ref[c * GROUPS_PER_CHUNK + g] = jnp.full((8, BLK), -1, I32)
        return 0

    lax.fori_loop(0, n_chunks, init_alive, 0)

    @pl.when(n_chunks % 2 == 1)
    def _():
        for g in range(GROUPS_PER_CHUNK):
            alive_ref[n_chunks * GROUPS_PER_CHUNK + g] = jnp.zeros((8, BLK), I32)
            planes_ref[n_chunks * GROUPS_PER_CHUNK + g] = jnp.zeros((32, 8, BLK), I32)

    def decide(state, counts, pair):
        above, t_bits = state[0], state[1]
        with_hi, with_both, only_lo = counts
        bit_hi = jnp.int32(1) << (31 - 2 * pair)
        bit_lo = jnp.int32(1) << (30 - 2 * pair)
        take_hi = (above + with_hi) >= topk
        above = jnp.where(take_hi, above, above + with_hi)
        with_lo = jnp.where(take_hi, with_both, only_lo)
        take_lo = (above + with_lo) >= topk
        above = jnp.where(take_lo, above, above + with_lo)
        t_bits = t_bits | jnp.where(take_hi, bit_hi, 0) | jnp.where(take_lo, bit_lo, 0)
        return above, t_bits, jnp.where(take_hi, 0, -1), jnp.where(take_lo, 0, -1)

    groups_per_step = 2 * GROUPS_PER_CHUNK

    def sweep(prev, cur, drop_hi, drop_lo):
        def step(p, accs):
            accs = list(accs)
            for g in range(groups_per_step):
                gi = p * groups_per_step + g
                alive = alive_ref[gi]
                if prev is not None:
                    alive = (alive & (planes_ref[gi, 2 * prev] ^ drop_hi)) & (planes_ref[gi, 2 * prev + 1] ^ drop_lo)
                    alive_ref[gi] = alive
                if cur is None:
                    accs[0] = accs[0] + lax.population_count(alive)
                else:
                    lo = planes_ref[gi, 2 * cur + 1]
                    with_hi = alive & planes_ref[gi, 2 * cur]
                    accs[0] = accs[0] + lax.population_count(with_hi)
                    accs[1] = accs[1] + lax.population_count(with_hi & lo)
                    accs[2] = accs[2] + lax.population_count((alive ^ with_hi) & lo)
            return tuple(accs)

        n_acc = 1 if cur is None else 3
        accs = lax.fori_loop(0, (n_chunks + 1) // 2, step, tuple(jnp.zeros((8, BLK), I32) for _ in range(n_acc)))
        return tuple(jnp.sum(a, axis=0, keepdims=True) for a in accs)

    zero = jnp.zeros((1, BLK), I32)
    state = decide((zero, zero), sweep(None, 0, None, None), 0)

    def pair_step(pair, state):
        return decide(state, sweep(pair - 1, pair, state[2], state[3]), pair)

    above, t_bits, drop_hi, drop_lo = lax.fori_loop(1, 16, pair_step, state)
    n_eq, = sweep(15, None, drop_hi, drop_lo)
    need = topk - above
    thr = t_bits ^ INT_MIN
    real = thr > KEY_OF_NEG_INF
    thr_f = jnp.where(real, _ordered_bits_to_float(thr), jnp.finfo(F32).min)

    def bias_chunk(c, _):
        bias_ref[c] = jnp.where(score_ref[c] >= thr_f, 0.0, NEG_BIG)
        return 0

    lax.fori_loop(0, n_chunks, bias_chunk, 0)

    has_tie = jnp.max(jnp.where(jnp.logical_and(real, n_eq > need), 1, 0)) > 0

    @pl.when(has_tie)
    def _():
        need_f = need.astype(F32)
        r = lax.broadcasted_iota(I32, (BLK, BLK), 0)
        cc = lax.broadcasted_iota(I32, (BLK, BLK), 1)
        upto = (cc <= r).astype(BF16)

        def tie_chunk(c, run):
            sc = score_ref[c]
            for u in range(KEY_CHUNK // BLK):
                st = sc[u * BLK:(u + 1) * BLK, :]
                eq = jnp.logical_and(st == thr_f, real)
                eqf = jnp.where(eq, 1.0, 0.0)
                rank = _dot(upto, eqf.astype(BF16)) + run
                sel = jnp.logical_or(st > thr_f, jnp.logical_and(eq, rank <= need_f))
                bias_ref[c, u * BLK:(u + 1) * BLK, :] = jnp.where(
                    real, jnp.where(sel, 0.0, NEG_BIG), jnp.where(st >= thr_f, 0.0, NEG_BIG))
                run = run + jnp.sum(eqf, axis=0, keepdims=True)
            return run

        lax.fori_loop(0, n_chunks, tie_chunk, jnp.zeros((1, BLK), F32))

    acc_ref[...] = jnp.zeros_like(acc_ref)
    bias_ref[n_kc] = jnp.full((KEY_CHUNK, BLK), NEG_BIG, F32)
    q = q_ref[...]

    def logits_into(lg, c):
        kc = jnp.minimum(c, n_chunks - 1)
        off = pl.multiple_of(kc * KEY_CHUNK, KEY_CHUNK)
        bias = bias_ref[jnp.where(c < n_chunks, c, n_kc)]
        for h in range(DSA_HEADS):
            hs = slice(h * HEAD_DIM, (h + 1) * HEAD_DIM)
            lg[h] = _dot_nt(k_ref[pl.ds(off, KEY_CHUNK), hs], q[:, hs]) + bias

    def reduce_from(lg, c, ms):
        vc = jnp.minimum(c, n_chunks - 1)
        new_m = []
        for h in range(DSA_HEADS):
            logits = lg[h]
            m_new = jnp.maximum(ms[h], jnp.max(logits, axis=0, keepdims=True))
            alpha = jnp.exp2(ms[h] - m_new)
            pr = jnp.exp2(logits - m_new).astype(BF16)
            acc_ref[h] = alpha * acc_ref[h] + _dot(vt_ref[vc, h * V_ROWS:(h + 1) * V_ROWS, :], pr)
            new_m.append(m_new)
        return tuple(new_m)

    next_full, next_last = indexer(qi_next_ref, tq_next_ref, nxt)
    last_full = _chunks_of_block(nxt) - 2
    n_pairs = (n_chunks + 1) // 2

    def attn_pair(pair, ms):
        c = 2 * pair
        logits_into(lg_b, c + 1)
        ms = reduce_from(lg_a, c, ms)
        next_full(jnp.maximum(jnp.minimum(c, last_full), 0))
        logits_into(lg_a, c + 2)
        ms = reduce_from(lg_b, c + 1, ms)
        next_full(jnp.maximum(jnp.minimum(c + 1, last_full), 0))
        return ms

    logits_into(lg_a, 0)
    lax.fori_loop(0, n_pairs, attn_pair, tuple(jnp.full((1, BLK), NEG_BIG, F32) for _ in range(DSA_HEADS)))
    for h in range(DSA_HEADS):
        weighted = acc_ref[h, 0:HEAD_DIM, :]
        denom = acc_ref[h, HEAD_DIM:HEAD_DIM + 1, :]
        o_ref[:, h * HEAD_DIM:(h + 1) * HEAD_DIM] = (weighted / denom).T.astype(o_ref.dtype)

    def rest(c, _):
        next_full(c)
        return 0

    lax.fori_loop(2 * n_pairs, last_full + 1, rest, 0)
    next_last()


def _dsa(p, k, vt):
    s = p.shape[0]
    nb = s // BLK
    topk = min(TOPK_MAX, s // 4)
    n_kc = s // KEY_CHUNK
    return pl.pallas_call(
        functools.partial(_dsa_kernel, topk=topk),
        grid=(nb,),
        in_specs=[pl.BlockSpec((BLK, D_DSA), lambda i: (i, OFF_D_Q // D_DSA)),
                  pl.BlockSpec((BLK, IDX_HEADS * IDX_DIM), lambda i: (i, OFF_D_QI // (IDX_HEADS * IDX_DIM))),
                  pl.BlockSpec((BLK, BLK), lambda i: (i, OFF_TAIL // BLK)),
                  pl.BlockSpec((BLK, IDX_HEADS * IDX_DIM),
                               lambda i: (jnp.minimum(i + 1, nb - 1), OFF_D_QI // (IDX_HEADS * IDX_DIM))),
                  pl.BlockSpec((BLK, BLK), lambda i: (jnp.minimum(i + 1, nb - 1), OFF_TAIL // BLK)),
                  _resident((s, BLK), lambda i: (0, OFF_TAIL // BLK)),
                  _resident((s, D_DSA), lambda i: (0, 0)),
                  _resident((n_kc, DSA_HEADS * V_ROWS, KEY_CHUNK), lambda i: (0, 0, 0))],
        out_specs=pl.BlockSpec((BLK, D_DSA), lambda i: (i, 0)),
        out_shape=jax.ShapeDtypeStruct((s, D_DSA), BF16),
        scratch_shapes=[pltpu.VMEM((n_kc, KEY_CHUNK, BLK), F32),
                        pltpu.VMEM((n_kc * GROUPS_PER_CHUNK, 32, 8, BLK), I32),
                        pltpu.VMEM((n_kc * GROUPS_PER_CHUNK, 8, BLK), I32),
                        pltpu.VMEM((n_kc + 1, KEY_CHUNK, BLK), F32),
                        pltpu.VMEM((DSA_HEADS, V_ROWS, BLK), F32),
                        pltpu.VMEM((DSA_HEADS, KEY_CHUNK, BLK), F32),
                        pltpu.VMEM((DSA_HEADS, KEY_CHUNK, BLK), F32)],
        compiler_params=_params("arbitrary"),
        name="dsa",
    )(p, p, p, p, p, p, k, vt)


def _layer_norm(r, g, b):
    mu = jnp.mean(r, axis=1, keepdims=True)
    var = jnp.mean(jnp.square(r - mu), axis=1, keepdims=True)
    return (r - mu) * lax.rsqrt(var + LN_EPS) * g + b


def _first_max_of4(vals):
    a, b, c, d = vals
    m = jnp.maximum(jnp.maximum(a, b), jnp.maximum(c, d))
    idx = jnp.where(a == m, 0, jnp.where(b == m, 1, jnp.where(c == m, 2, 3)))
    return m, idx


def _router_gates(logits_t):
    mx = jnp.max(logits_t, axis=0, keepdims=True)
    e = jnp.exp(logits_t - mx)
    probs = e / jnp.sum(e, axis=0, keepdims=True)
    rows = [probs[j:j + 1, :] for j in range(N_EXPERTS)]
    m1s, m2s, i1s, i2s, scores = [], [], [], [], []
    for g in range(N_GROUPS):
        vals = rows[g * EXPERTS_PER_GROUP:(g + 1) * EXPERTS_PER_GROUP]
        m1, i1 = _first_max_of4(vals)
        rest = [jnp.where(i1 == j, -1.0, vals[j]) for j in range(EXPERTS_PER_GROUP)]
        m2, i2 = _first_max_of4(rest)
        m1s.append(m1); m2s.append(m2); i1s.append(i1); i2s.append(i2); scores.append(m1 + m2)
    best, g_sel = _first_max_of4(scores)
    pick = lambda xs: jnp.where(g_sel == 0, xs[0], jnp.where(g_sel == 1, xs[1], jnp.where(g_sel == 2, xs[2], xs[3])))
    m1, m2, i1, i2 = pick(m1s), pick(m2s), pick(i1s), pick(i2s)
    den = m1 + m2
    w1, w2 = m1 / den, m2 / den
    e1 = g_sel * EXPERTS_PER_GROUP + i1
    e2 = g_sel * EXPERTS_PER_GROUP + i2
    gates = [jnp.where(e1 == j, w1, 0.0) + jnp.where(e2 == j, w2, 0.0) for j in range(N_EXPERTS)]
    return jnp.concatenate(gates, axis=0)


def _out_kernel(ysb_ref, yret_ref, ydsa_ref, wo_f32_ref, x_ref, g_ref, b_ref, wr_ref,
                x1_ref, gates_ref, gates_t_ref, cnt_ref, wo_ref):
    @pl.when(pl.program_id(0) == 0)
    def _():
        wo_ref[...] = wo_f32_ref[0].astype(BF16)

    mix = (_dot(ysb_ref[...], wo_ref[0:D_SB, :]) + _dot(yret_ref[...], wo_ref[D_SB:D_SB + D_RET, :])
           + _dot(ydsa_ref[...], wo_ref[D_SB + D_RET:, :]))
    x1 = _layer_norm(DEEPNORM_ALPHA * x_ref[...] + mix, g_ref[...], b_ref[...])
    x1_ref[...] = x1
    logits_t = lax.dot_general(wr_ref[...], x1, (((1,), (1,)), ((), ())),
                               preferred_element_type=F32, precision=lax.Precision.HIGHEST)
    gates_t = _router_gates(logits_t)
    tm = x1.shape[0]
    gates_t_ref[...] = gates_t
    padded = jnp.concatenate([gates_t, jnp.zeros((BLK - N_EXPERTS, tm), F32)], axis=0)
    gates_ref[...] = padded.T
    chosen = jnp.sum(jnp.where(gates_t > 0.0, 1, 0), axis=1, keepdims=True)
    cnt_ref[0] = jnp.broadcast_to(chosen, (N_EXPERTS, BLK))


def _out_proj(ysb, yret, ydsa, w_o, layer, x, g, b, wr_t):
    s = x.shape[0]
    tm = MOE_TILE
    row = lambda n: pl.BlockSpec((tm, n), lambda i: (i, 0))
    whole = lambda a: _resident(a.shape, lambda i: (0, 0))
    return pl.pallas_call(
        _out_kernel,
        grid=(s // tm,),
        in_specs=[row(D_SB), row(D_RET), row(D_DSA),
                  _resident((1,) + w_o.shape[1:], lambda i: (layer, 0, 0)),
                  row(D_MODEL), whole(g), whole(b), whole(wr_t)],
        out_specs=[row(D_MODEL), row(BLK), pl.BlockSpec((N_EXPERTS, tm), lambda i: (0, i)),
                   pl.BlockSpec((1, N_EXPERTS, BLK), lambda i: (i, 0, 0))],
        out_shape=[jax.ShapeDtypeStruct((s, D_MODEL), F32), jax.ShapeDtypeStruct((s, BLK), F32),
                   jax.ShapeDtypeStruct((N_EXPERTS, s), F32),
                   jax.ShapeDtypeStruct((s // tm, N_EXPERTS, BLK), I32)],
        scratch_shapes=[pltpu.VMEM(w_o.shape[1:], BF16)],
        compiler_params=_params("arbitrary"),
        name="out_proj",
    )(ysb, yret, ydsa, w_o, x, g, b, wr_t)


LOCAL_ROWS = 2 * MOE_TILE + N_EXPERTS * WIN


def _round_up(x, m):
    return (x + m - 1) // m * m


def _sorted_rows(s):
    n_tiles = s // MOE_TILE
    return _round_up(2 * s + n_tiles * N_EXPERTS * (WIN - 1) + N_EXPERTS * (ROW_TILE - 1), ROW_TILE)


def _moe_plan(cnt, s):
    n_row_tiles = _sorted_rows(s) // ROW_TILE
    seg = _round_up(cnt, WIN)
    rows_e = jnp.sum(seg, axis=0)
    region = _round_up(rows_e, ROW_TILE)
    region_off = jnp.cumsum(region) - region
    dest = region_off[None, :] + jnp.cumsum(seg, axis=0) - seg
    tiles_e = region // ROW_TILE
    tile_end = jnp.cumsum(tiles_e)
    k = jnp.arange(n_row_tiles, dtype=I32)
    tile_expert = jnp.minimum(jnp.sum((k[:, None] >= tile_end[None, :]).astype(I32), axis=1), N_EXPERTS - 1)
    first = (tile_end - tiles_e)[tile_expert]
    valid = jnp.clip(rows_e[tile_expert] - (k - first) * ROW_TILE, 0, ROW_TILE)
    tile_valid = jnp.where(k < tile_end[-1], valid, 0).astype(I32)
    fill = jnp.concatenate([region_off + rows_e, (region - rows_e) // WIN,
                            tile_end[-1:], n_row_tiles - tile_end[-1:]]).astype(I32)
    used = rows_e > 0
    e_id = jnp.arange(N_EXPERTS, dtype=I32)
    buffer_e = (jnp.cumsum(used) - used) % 2
    later_used = jnp.where(jnp.logical_and(used[None, :], e_id[None, :] > e_id[:, None]), e_id[None, :], N_EXPERTS)
    next_e = jnp.min(later_used, axis=1)
    next_e = jnp.where(next_e == N_EXPERTS, -1, next_e)
    tile_info = jnp.stack([tile_expert, tile_valid, buffer_e[tile_expert], next_e[tile_expert]]).astype(I32)
    return cnt.reshape(-1).astype(I32), dest.reshape(-1).astype(I32), fill, tile_info.reshape(-1)


def _slot_offsets(cnt_sm, tile):
    offs, o = [], 0
    for e in range(N_EXPERTS):
        offs.append(o)
        o = o + _round_up(cnt_sm[tile * N_EXPERTS + e], WIN)
    return offs


def _window_copies(cnt_sm, dest_sm, tile, offs, local_ref, sorted_hbm, sem, to_sorted):
    total = 0
    for e in range(N_EXPERTS):
        n_win = (cnt_sm[tile * N_EXPERTS + e] + WIN - 1) // WIN
        base_local, base_sorted = offs[e], dest_sm[tile * N_EXPERTS + e]

        def issue(j, _, base_local=base_local, base_sorted=base_sorted):
            loc = local_ref.at[pl.ds(pl.multiple_of(base_local + j * WIN, WIN), WIN)]
            srt = sorted_hbm.at[pl.ds(pl.multiple_of(base_sorted + j * WIN, WIN), WIN)]
            if to_sorted:
                pltpu.make_async_copy(loc, srt, sem).start()
            else:
                pltpu.make_async_copy(srt, loc, sem).start()
            return 0

        lax.fori_loop(0, n_win, issue, 0)
        total = total + n_win
    return total


def _n_windows(cnt_sm, tile):
    total = 0
    for e in range(N_EXPERTS):
        total = total + (cnt_sm[tile * N_EXPERTS + e] + WIN - 1) // WIN
    return total


def _wait_windows(total, local_ref, sorted_hbm, sem):
    def wait(j, _):
        pltpu.make_async_copy(local_ref.at[pl.ds(0, WIN)], sorted_hbm.at[pl.ds(0, WIN)], sem).wait()
        return 0

    lax.fori_loop(0, total, wait, 0)


def _zero_fill(fill_sm, xs_hbm, zero_ref, sem_win, sem_tile):
    zero_ref[...] = jnp.zeros_like(zero_ref)
    win_copy = lambda row: pltpu.make_async_copy(
        zero_ref.at[pl.ds(0, WIN)], xs_hbm.at[pl.ds(pl.multiple_of(row, WIN), WIN)], sem_win)
    tile_copy = lambda row: pltpu.make_async_copy(
        zero_ref, xs_hbm.at[pl.ds(pl.multiple_of(row, ROW_TILE), ROW_TILE)], sem_tile)
    n_pad = 0
    for e in range(N_EXPERTS):
        first, n_win = fill_sm[e], fill_sm[N_EXPERTS + e]

        def issue(j, _, first=first):
            win_copy(first + j * WIN).start()
            return 0

        lax.fori_loop(0, n_win, issue, 0)
        n_pad = n_pad + n_win
    first_tile, n_tail = fill_sm[2 * N_EXPERTS], fill_sm[2 * N_EXPERTS + 1]

    def issue_tile(j, _):
        tile_copy((first_tile + j) * ROW_TILE).start()
        return 0

    def wait_win(j, _):
        win_copy(0).wait()
        return 0

    def wait_tile(j, _):
        tile_copy(0).wait()
        return 0

    lax.fori_loop(0, n_tail, issue_tile, 0)
    lax.fori_loop(0, n_pad, wait_win, 0)
    lax.fori_loop(0, n_tail, wait_tile, 0)


def _dispatch_kernel(cnt_sm, dest_sm, fill_sm, x_ref, gt_ref, xs_hbm, local_ref, zero_ref, sem, sem_win, sem_tile):
    tile = pl.program_id(0)

    @pl.when(tile == 0)
    def _():
        _zero_fill(fill_sm, xs_hbm, zero_ref, sem_win, sem_tile)

    offs = _slot_offsets(cnt_sm, tile)
    chosen = gt_ref[...] > 0.0
    t_r = lax.broadcasted_iota(I32, (MOE_TILE, MOE_TILE), 0)
    t_c = lax.broadcasted_iota(I32, (MOE_TILE, MOE_TILE), 1)
    earlier = _dot(jnp.where(chosen, 1.0, 0.0).astype(BF16), (t_r < t_c).astype(BF16))
    e_id = lax.broadcasted_iota(I32, (N_EXPERTS, 1), 0)
    slot = jnp.zeros((N_EXPERTS, 1), I32)
    for e in range(N_EXPERTS):
        slot = jnp.where(e_id == e, offs[e], slot)
    pos = earlier + slot.astype(F32)
    p_lo = jnp.min(jnp.where(chosen, pos, float(LOCAL_ROWS)), axis=0, keepdims=True).astype(I32)
    p_hi = jnp.max(jnp.where(chosen, pos, -1.0), axis=0, keepdims=True).astype(I32)
    row = lax.broadcasted_iota(I32, (LOCAL_ROWS, MOE_TILE), 0)
    onehot = jnp.where(row == p_lo, 1.0, jnp.where(row == p_hi, 1.0, 0.0)).astype(BF16)
    buf = tile % 2
    mine, mine_sem = local_ref.at[buf], sem.at[buf]
    other, other_sem = local_ref.at[1 - buf], sem.at[1 - buf]
    mine[...] = _dot(onehot, x_ref[...].astype(BF16)).astype(BF16)
    total = _window_copies(cnt_sm, dest_sm, tile, offs, mine, xs_hbm, mine_sem, True)

    @pl.when(tile >= 1)
    def _():
        _wait_windows(_n_windows(cnt_sm, tile - 1), other, xs_hbm, other_sem)

    @pl.when(tile == pl.num_programs(0) - 1)
    def _():
        _wait_windows(total, mine, xs_hbm, mine_sem)


def _dispatch(x1, gates_t, cnt_flat, dest_flat, fill):
    s = x1.shape[0]
    return pl.pallas_call(
        _dispatch_kernel,
        grid_spec=pltpu.PrefetchScalarGridSpec(
            num_scalar_prefetch=3,
            grid=(s // MOE_TILE,),
            in_specs=[pl.BlockSpec((MOE_TILE, D_MODEL), lambda i, *_: (i, 0)),
                      pl.BlockSpec((N_EXPERTS, MOE_TILE), lambda i, *_: (0, i))],
            out_specs=pl.BlockSpec(memory_space=pl.ANY),
            scratch_shapes=[pltpu.VMEM((2, LOCAL_ROWS, D_MODEL), BF16), pltpu.VMEM((ROW_TILE, D_MODEL), BF16),
                            pltpu.SemaphoreType.DMA((2,)), pltpu.SemaphoreType.DMA(()),
                            pltpu.SemaphoreType.DMA(())]),
        out_shape=jax.ShapeDtypeStruct((_sorted_rows(s), D_MODEL), BF16),
        compiler_params=_params("arbitrary"),
        name="moe_dispatch",
    )(cnt_flat, dest_flat, fill, x1, gates_t)


def _expert_kernel(info_sm, xs_ref, wg_hbm, wu_hbm, wd_hbm, y_ref, wg_f, wu_f, wd_f, wg_b, wu_b, wd_b, sem, *,
                   layer):
    k = pl.program_id(0)
    n = pl.num_programs(0)
    expert, valid, buf, next_expert = info_sm[k], info_sm[n + k], info_sm[2 * n + k], info_sm[3 * n + k]
    new_expert = jnp.logical_or(k == 0, expert != info_sm[jnp.maximum(k - 1, 0)])

    def weight_copies(e, b):
        return (pltpu.make_async_copy(wg_hbm.at[layer, e], wg_f.at[b], sem.at[b, 0]),
                pltpu.make_async_copy(wu_hbm.at[layer, e], wu_f.at[b], sem.at[b, 1]),
                pltpu.make_async_copy(wd_hbm.at[layer, e], wd_f.at[b], sem.at[b, 2]))

    @pl.when(jnp.logical_and(k == 0, valid > 0))
    def _():
        for c in weight_copies(expert, buf):
            c.start()

    @pl.when(jnp.logical_and(valid > 0, new_expert))
    def _():
        for c in weight_copies(expert, buf):
            c.wait()
        wg_b[...] = wg_f[buf].astype(BF16)
        wu_b[...] = wu_f[buf].astype(BF16)
        wd_b[...] = wd_f[buf].astype(BF16)

        @pl.when(next_expert >= 0)
        def _():
            for c in weight_copies(next_expert, 1 - buf):
                c.start()

    @pl.when(valid > 0)
    def _():
        x = xs_ref[...]
        hg = _dot(x, wg_b[...])
        hu = _dot(x, wu_b[...])
        act = hg * jax.nn.sigmoid(hg) * hu
        y_ref[...] = _dot(act.astype(BF16), wd_b[...]).astype(y_ref.dtype)

    @pl.when(valid == 0)
    def _():
        y_ref[...] = jnp.zeros_like(y_ref)


def _experts(xs, tile_info, w_gate, w_up, w_down, layer):
    rows = xs.shape[0]
    hbm = pl.BlockSpec(memory_space=pl.ANY)
    up_shape, down_shape = (D_MODEL, D_FF_EXPERT), (D_FF_EXPERT, D_MODEL)
    return pl.pallas_call(
        functools.partial(_expert_kernel, layer=layer),
        grid_spec=pltpu.PrefetchScalarGridSpec(
            num_scalar_prefetch=1,
            grid=(rows // ROW_TILE,),
            in_specs=[pl.BlockSpec((ROW_TILE, D_MODEL), lambda k, info: (k, 0)), hbm, hbm, hbm],
            out_specs=pl.BlockSpec((ROW_TILE, D_MODEL), lambda k, info: (k, 0)),
            scratch_shapes=[pltpu.VMEM((2,) + up_shape, F32), pltpu.VMEM((2,) + up_shape, F32),
                            pltpu.VMEM((2,) + down_shape, F32),
                            pltpu.VMEM(up_shape, BF16), pltpu.VMEM(up_shape, BF16), pltpu.VMEM(down_shape, BF16),
                            pltpu.SemaphoreType.DMA((2, 3))]),
        out_shape=jax.ShapeDtypeStruct((rows, D_MODEL), BF16),
        compiler_params=_params("arbitrary"),
        name="moe_experts",
    )(tile_info, xs, w_gate, w_up, w_down)


def _combine_kernel(cnt_sm, dest_sm, x_ref, gates_ref, y_hbm, g_ref, b_ref, o_ref, local_ref, sem):
    tile = pl.program_id(0)

    buf = tile % 2
    mine, mine_sem = local_ref.at[buf], sem.at[buf]
    offs = _slot_offsets(cnt_sm, tile)

    @pl.when(tile == 0)
    def _():
        local_ref[...] = jnp.zeros_like(local_ref)
        _window_copies(cnt_sm, dest_sm, tile, offs, mine, y_hbm, mine_sem, False)

    @pl.when(tile + 1 < pl.num_programs(0))
    def _():
        _window_copies(cnt_sm, dest_sm, tile + 1, _slot_offsets(cnt_sm, tile + 1),
                       local_ref.at[1 - buf], y_hbm, sem.at[1 - buf], False)

    gates = gates_ref[...]
    chosen = gates > 0.0
    t_r = lax.broadcasted_iota(I32, (MOE_TILE, MOE_TILE), 0)
    t_c = lax.broadcasted_iota(I32, (MOE_TILE, MOE_TILE), 1)
    earlier = _dot((t_c < t_r).astype(BF16), jnp.where(chosen, 1.0, 0.0).astype(BF16))
    e_id = lax.broadcasted_iota(I32, (1, BLK), 1)
    slot = jnp.zeros((1, BLK), I32)
    for e in range(N_EXPERTS):
        slot = jnp.where(e_id == e, offs[e], slot)
    pos = jnp.where(chosen, earlier + slot.astype(F32), -1.0)
    p_lo = jnp.min(jnp.where(chosen, pos, float(LOCAL_ROWS)), axis=1, keepdims=True)
    p_hi = jnp.max(pos, axis=1, keepdims=True)
    w_lo = jnp.sum(jnp.where(pos == p_lo, gates, 0.0), axis=1, keepdims=True)
    w_hi = jnp.sum(jnp.where(pos == p_hi, gates, 0.0), axis=1, keepdims=True)
    col = lax.broadcasted_iota(I32, (MOE_TILE, LOCAL_ROWS), 1)
    weights = jnp.where(col == p_lo.astype(I32), w_lo, jnp.where(col == p_hi.astype(I32), w_hi, 0.0))

    _wait_windows(_n_windows(cnt_sm, tile), mine, y_hbm, mine_sem)
    ffn = _dot(weights.astype(BF16), mine[...])
    o_ref[...] = _layer_norm(DEEPNORM_ALPHA * x_ref[...] + ffn, g_ref[...], b_ref[...])


def _combine(x1, gates, y, cnt_flat, dest_flat, g, b):
    s = x1.shape[0]
    row = lambda n: pl.BlockSpec((MOE_TILE, n), lambda i, *_: (i, 0))
    vec = pl.BlockSpec((1, D_MODEL), lambda i, *_: (0, 0))
    return pl.pallas_call(
        _combine_kernel,
        grid_spec=pltpu.PrefetchScalarGridSpec(
            num_scalar_prefetch=2,
            grid=(s // MOE_TILE,),
            in_specs=[row(D_MODEL), row(BLK), pl.BlockSpec(memory_space=pl.ANY), vec, vec],
            out_specs=row(D_MODEL),
            scratch_shapes=[pltpu.VMEM((2, LOCAL_ROWS, D_MODEL), BF16), pltpu.SemaphoreType.DMA((2,))]),
        out_shape=jax.ShapeDtypeStruct((s, D_MODEL), F32),
        compiler_params=_params("arbitrary"),
        name="moe_combine",
    )(cnt_flat, dest_flat, x1, gates, y, g, b)


def _moe(x1, gates, gates_t, cnt, w_gate, w_up, w_down, g, b, layer):
    s = x1.shape[0]
    cnt_flat, dest_flat, fill, tile_info = _moe_plan(cnt[:, :, 0], s)
    xs = _dispatch(x1, gates_t, cnt_flat, dest_flat, fill)
    y = _experts(xs, tile_info, w_gate, w_up, w_down, layer)
    return _combine(x1, gates, y, cnt_flat, dest_flat, g, b)


def _reorder_w_in(w_in):
    pieces, o = {}, 0
    for name, width in (("sb", 3 * D_SB), ("ret", 4 * D_RET), ("d_q", D_DSA), ("d_ckv", KV_RANK),
                        ("d_qi", IDX_HEADS * IDX_DIM)):
        pieces[name] = w_in[:, :, o:o + width]
        o += width
    parts = [pieces["sb"], pieces["d_q"], pieces["ret"], pieces["d_qi"], pieces["d_ckv"], w_in[:, :, o:]]
    width = sum(a.shape[2] for a in parts)
    parts.append(jnp.zeros(w_in.shape[:2] + (D_PROJ - width,), w_in.dtype))
    return jnp.concatenate(parts, axis=2).astype(BF16)


def kernel(x, w_in, w_kv_up, kv_norm_g, ret_gn_g, w_o, ln1_g, ln1_b, w_router, w_gate, w_up, w_down,
           ln2_g, ln2_b):
    b, s, _ = x.shape
    assert b == 1 and s % KEY_CHUNK == 0
    h = x[0]
    tables = _retention_tables(s)
    wr_t = w_router.T
    w_proj = _reorder_w_in(w_in)
    for l in range(DEPTH):
        p = _proj(h, w_proj, l)
        y_sb = _stick_breaking(p)
        y_ret = _retention(p, ret_gn_g[l][None, :], tables)
        w_kv = w_kv_up[l].reshape(KV_RANK, DSA_HEADS, 2, HEAD_DIM)
        w_kv = jnp.concatenate([w_kv[:, :, 0, :].reshape(KV_RANK, D_DSA),
                                w_kv[:, :, 1, :].reshape(KV_RANK, D_DSA)], axis=1).astype(BF16)
        k_dsa, vt_dsa = _kv_up(p, kv_norm_g[l][None, :], w_kv)
        y_dsa = _dsa(p, k_dsa, vt_dsa)
        x1, gates, gates_t, cnt = _out_proj(y_sb, y_ret, y_dsa, w_o, l, h, ln1_g[l][None, :], ln1_b[l][None, :],
                                            wr_t)
        h = _moe(x1, gates, gates_t, cnt, w_gate, w_up, w_down, ln2_g[l][None, :], ln2_b[l][None, :], l)
    return h[None]
```

```python
import functools

import numpy as np
import jax
import jax.numpy as jnp
from jax import lax
from jax.experimental import pallas as pl
from jax.experimental.pallas import tpu as pltpu

F32 = jnp.float32
BF16 = jnp.bfloat16
I32 = jnp.int32

D_MODEL = 2048
HEAD_DIM = 128
SB_HEADS = 6
RET_HEADS = 4
DSA_HEADS = 6
D_SB = SB_HEADS * HEAD_DIM
D_RET = RET_HEADS * HEAD_DIM
D_DSA = DSA_HEADS * HEAD_DIM
KV_RANK = 256
IDX_HEADS = 8
IDX_DIM = 64
IDX_SCALE = IDX_DIM ** -0.5 * IDX_HEADS ** -0.5
TOPK_MAX = 256
BLK = 128
N_EXPERTS = 16
N_GROUPS = 4
EXPERTS_PER_GROUP = N_EXPERTS // N_GROUPS
D_FF_EXPERT = 512
LN_EPS = 1e-5
RMS_EPS = 1e-6
GN_EPS = 1e-6
DEPTH = 2
DEEPNORM_ALPHA = (2 * DEPTH) ** 0.25

OFF_SB_Q = 0
OFF_SB_K = OFF_SB_Q + D_SB
OFF_SB_V = OFF_SB_K + D_SB
OFF_D_Q = OFF_SB_V + D_SB
OFF_R_Q = OFF_D_Q + D_DSA
OFF_R_K = OFF_R_Q + D_RET
OFF_R_V = OFF_R_K + D_RET
OFF_R_G = OFF_R_V + D_RET
OFF_D_QI = OFF_R_G + D_RET
OFF_D_CKV = OFF_D_QI + IDX_HEADS * IDX_DIM
OFF_TAIL = OFF_D_CKV + KV_RANK
D_PROJ = 6144

KEY_CHUNK = 512
MOE_TILE = 512
WIN = 16
ROW_TILE = 512
RET_CHUNKS_PER_STEP = 4
VMEM_LIMIT = 56 * 1024 * 1024
LOG2_E = 1.4426950408889634
NEG_BIG = -1e30
EXP_UNDERFLOW = -87.4
KEY_OF_NEG_INF = -2139095041
INT_MIN = -2147483648


def _dot(a, b):
    return jnp.dot(a, b, preferred_element_type=F32)


def _dot_nt(a, b):
    return lax.dot_general(a, b, (((1,), (1,)), ((), ())), preferred_element_type=F32)


def _dot_tn(a, b):
    return lax.dot_general(a, b, (((0,), (0,)), ((), ())), preferred_element_type=F32)


def _params(*sem):
    return pltpu.CompilerParams(dimension_semantics=sem, vmem_limit_bytes=VMEM_LIMIT)


def _resident(shape, index_map):
    return pl.BlockSpec(shape, index_map, pipeline_mode=pl.Buffered(1))


def _proj_kernel(x_ref, w_ref, o_ref):
    o_ref[...] = _dot(x_ref[...].astype(BF16), w_ref[0]).astype(o_ref.dtype)


def _proj(x, w, layer):
    s, d = x.shape
    n = w.shape[2]
    tm = min(1024, s)
    tn = 1536
    return pl.pallas_call(
        _proj_kernel,
        grid=(s // tm, n // tn),
        in_specs=[pl.BlockSpec((tm, d), lambda i, j: (i, 0)),
                  pl.BlockSpec((1, d, tn), lambda i, j: (layer, 0, j))],
        out_specs=pl.BlockSpec((tm, tn), lambda i, j: (i, j)),
        out_shape=jax.ShapeDtypeStruct((s, n), BF16),
        compiler_params=_params("parallel", "arbitrary"),
        name="proj",
    )(x, w)


def _sb_kernel(q_ref, k_ref, v_ref, o_ref, acc_ref):
    i = pl.program_id(0)
    q = q_ref[...]
    scale = HEAD_DIM ** -0.5
    key_pos = lax.broadcasted_iota(I32, (BLK, BLK), 0)
    qry_pos = lax.broadcasted_iota(I32, (BLK, BLK), 1)
    later = (qry_pos > key_pos).astype(BF16)

    def key_tile(j, first_query, cs):
        off = pl.multiple_of(j * BLK, BLK)
        strict = (off + key_pos) < (first_query + qry_pos)
        pvs, new_cs = [], []
        for h in range(SB_HEADS):
            hs = slice(h * HEAD_DIM, (h + 1) * HEAD_DIM)
            z = _dot_nt(k_ref[pl.ds(off, BLK), hs], q[:, hs]) * scale
            sp = jnp.maximum(z, 0.0) + jnp.log1p(jnp.exp(-jnp.abs(z)))
            log_rem = jnp.where(strict, -sp, 0.0)
            hi = log_rem.astype(BF16)
            lo = (log_rem - hi.astype(F32)).astype(BF16)
            after = _dot(later, hi) + _dot(later, lo)
            a = jnp.where(strict, jnp.exp(z - sp + after + cs[h]), 0.0)
            pvs.append(_dot_tn(a.astype(BF16), v_ref[pl.ds(off, BLK), hs]))
            new_cs.append(cs[h] + jnp.sum(log_rem, axis=0, keepdims=True))
        return pvs, tuple(new_cs)

    cs = tuple(jnp.zeros((1, BLK), F32) for _ in range(SB_HEADS))
    pv_a, cs = key_tile(i, i * BLK, cs)
    pv_b, cs = key_tile(jnp.maximum(i - 1, 0), jnp.where(i >= 1, i * BLK, -BLK), cs)
    for h in range(SB_HEADS):
        acc_ref[h] = pv_a[h] + pv_b[h]

    def cond(carry):
        j, cs = carry
        c_max = functools.reduce(jnp.maximum, cs)
        return jnp.logical_and(j >= 0, jnp.max(c_max) > EXP_UNDERFLOW)

    def body(carry):
        j, cs = carry
        pvs, cs = key_tile(j, i * BLK, cs)
        for h in range(SB_HEADS):
            acc_ref[h] += pvs[h]
        return j - 1, cs

    lax.while_loop(cond, body, (i - 2, cs))
    for h in range(SB_HEADS):
        o_ref[:, h * HEAD_DIM:(h + 1) * HEAD_DIM] = acc_ref[h].astype(o_ref.dtype)


def _stick_breaking(p):
    s = p.shape[0]
    nb = s // BLK
    return pl.pallas_call(
        _sb_kernel,
        grid=(nb,),
        in_specs=[pl.BlockSpec((BLK, D_SB), lambda i: (i, OFF_SB_Q // D_SB)),
                  _resident((s, D_SB), lambda i: (0, OFF_SB_K // D_SB)),
                  _resident((s, D_SB), lambda i: (0, OFF_SB_V // D_SB))],
        out_specs=pl.BlockSpec((BLK, D_SB), lambda i: (i, 0)),
        out_shape=jax.ShapeDtypeStruct((s, D_SB), BF16),
        scratch_shapes=[pltpu.VMEM((SB_HEADS, BLK, HEAD_DIM), F32)],
        compiler_params=_params("arbitrary"),
        name="stick_breaking",
    )(p, p, p)


def _ret_kernel(q_ref, k_ref, v_ref, g_ref, cos_ref, sin_ref, intra_ref, qd_ref, kd_ref, cd_ref,
                gn_ref, o_ref, state_ref):
    n = pl.program_id(0)

    @pl.when(n == 0)
    def _():
        state_ref[...] = jnp.zeros_like(state_ref)

    half = HEAD_DIM // 2
    swap = lambda t: jnp.concatenate([t[:, half:], t[:, :half]], axis=1)
    for h in range(RET_HEADS):
        hs = slice(h * HEAD_DIM, (h + 1) * HEAD_DIM)
        state = state_ref[h]
        for c in range(RET_CHUNKS_PER_STEP):
            rows = slice(c * BLK, (c + 1) * BLK)
            cos, sin = cos_ref[rows, :], sin_ref[rows, :]
            q = q_ref[rows, hs].astype(F32)
            k = k_ref[rows, hs].astype(F32)
            v = v_ref[rows, hs]
            qr = q * cos + swap(q) * sin
            kr = (k * cos + swap(k) * sin) * (HEAD_DIM ** -0.5)
            scores = _dot_nt(qr.astype(BF16), kr.astype(BF16)) * intra_ref[h]
            o = (_dot(scores.astype(BF16), v)
                 + _dot((qr * qd_ref[h]).astype(BF16), state.astype(BF16)))
            state = cd_ref[h] * state + _dot_tn((kr * kd_ref[h]).astype(BF16), v)
            mu = jnp.mean(o, axis=1, keepdims=True)
            var = jnp.mean(jnp.square(o - mu), axis=1, keepdims=True)
            on = (o - mu) * lax.rsqrt(var + GN_EPS) * gn_ref[:, hs]
            g = g_ref[rows, hs].astype(F32)
            o_ref[rows, hs] = (g * jax.nn.sigmoid(g) * on).astype(o_ref.dtype)
        state_ref[h] = state


def _retention_tables(s):
    f32 = np.float32
    half = HEAD_DIM // 2
    pos = np.arange(s, dtype=np.float64)
    theta = 10000.0 ** (-np.linspace(0.0, 1.0, half))
    ang = pos[:, None] * theta[None, :]
    cos, sin = np.cos(ang).astype(f32), np.sin(ang).astype(f32)
    cos2 = np.concatenate([cos, cos], axis=1)
    sin2 = np.concatenate([-sin, sin], axis=1)
    log_gamma = np.log1p(-(2.0 ** (-5.0 - np.arange(RET_HEADS, dtype=np.float64))))
    idx = np.arange(BLK, dtype=np.float64)
    diff = idx[:, None] - idx[None, :]
    intra = np.where(diff >= 0, np.exp(np.maximum(diff, 0.0)[None] * log_gamma[:, None, None]), 0.0).astype(f32)
    q_decay = np.exp((idx[None, :] + 1.0) * log_gamma[:, None]).astype(f32)
    k_decay = np.exp((BLK - 1.0 - idx[None, :]) * log_gamma[:, None]).astype(f32)
    chunk_decay = np.exp(BLK * log_gamma).astype(f32)
    full = (RET_HEADS, BLK, HEAD_DIM)
    return tuple(jnp.asarray(np.ascontiguousarray(t)) for t in (
        cos2, sin2, intra,
        np.broadcast_to(q_decay[:, :, None], full),
        np.broadcast_to(k_decay[:, :, None], full),
        np.broadcast_to(chunk_decay[:, None, None], full)))


def _retention(p, gn_g, tables):
    s = p.shape[0]
    rows = RET_CHUNKS_PER_STEP * BLK
    nc = s // rows
    cos2, sin2, intra, qd, kd, cd = tables
    col = lambda off: pl.BlockSpec((rows, D_RET), lambda n: (n, off // D_RET))
    per_head = pl.BlockSpec((RET_HEADS, BLK, HEAD_DIM), lambda n: (0, 0, 0))
    pos_spec = pl.BlockSpec((rows, HEAD_DIM), lambda n: (n, 0))
    return pl.pallas_call(
        _ret_kernel,
        grid=(nc,),
        in_specs=[col(OFF_R_Q), col(OFF_R_K), col(OFF_R_V), col(OFF_R_G),
                  pos_spec, pos_spec, per_head, per_head, per_head, per_head,
                  pl.BlockSpec((1, D_RET), lambda n: (0, 0))],
        out_specs=pl.BlockSpec((rows, D_RET), lambda n: (n, 0)),
        out_shape=jax.ShapeDtypeStruct((s, D_RET), BF16),
        scratch_shapes=[pltpu.VMEM((RET_HEADS, HEAD_DIM, HEAD_DIM), F32)],
        compiler_params=_params("arbitrary"),
        name="retention",
    )(p, p, p, p, cos2, sin2, intra, qd, kd, cd, gn_g)


def _kv_up_kernel(c_ref, g_ref, w_ref, k_ref, vt_ref):
    c = c_ref[...].astype(F32)
    y = c * lax.rsqrt(jnp.mean(jnp.square(c), axis=1, keepdims=True) + RMS_EPS) * g_ref[...]
    kv = _dot(y.astype(BF16), w_ref[...])
    k_ref[...] = (kv[:, :D_DSA] * (HEAD_DIM ** -0.5 * LOG2_E)).astype(k_ref.dtype)
    v_t = kv[:, D_DSA:].T
    ones = jnp.ones((V_ROWS - HEAD_DIM, v_t.shape[1]), vt_ref.dtype)
    for h in range(DSA_HEADS):
        vt_ref[0, h * V_ROWS:h * V_ROWS + HEAD_DIM, :] = v_t[h * HEAD_DIM:(h + 1) * HEAD_DIM, :].astype(vt_ref.dtype)
        vt_ref[0, h * V_ROWS + HEAD_DIM:(h + 1) * V_ROWS, :] = ones


def _kv_up(p, g, w):
    s = p.shape[0]
    n = w.shape[1]
    return pl.pallas_call(
        _kv_up_kernel,
        grid=(s // KEY_CHUNK,),
        in_specs=[pl.BlockSpec((KEY_CHUNK, KV_RANK), lambda i: (i, OFF_D_CKV // KV_RANK)),
                  pl.BlockSpec((1, KV_RANK), lambda i: (0, 0)),
                  pl.BlockSpec((KV_RANK, n), lambda i: (0, 0))],
        out_specs=[pl.BlockSpec((KEY_CHUNK, D_DSA), lambda i: (i, 0)),
                   pl.BlockSpec((1, DSA_HEADS * V_ROWS, KEY_CHUNK), lambda i: (i, 0, 0))],
        out_shape=[jax.ShapeDtypeStruct((s, D_DSA), BF16),
                   jax.ShapeDtypeStruct((s // KEY_CHUNK, DSA_HEADS * V_ROWS, KEY_CHUNK), BF16)],
        compiler_params=_params("parallel"),
        name="kv_up",
    )(p, g, w)


def _ordered_bits_to_float(u):
    return pltpu.bitcast(u ^ ((u >> 31) & 0x7FFFFFFF), F32)


V_ROWS = HEAD_DIM + 16
GROUPS_PER_CHUNK = KEY_CHUNK // (32 * 8)


def _bit_transpose32(words):
    a = list(words)
    j, mask = 16, 0x0000FFFF
    while j:
        k = 0
        while k < 32:
            t = (a[k] ^ (a[k + j] >> j)) & mask
            a[k] = a[k] ^ t
            a[k + j] = a[k + j] ^ (t << j)
            k = (k + j + 1) & ~j
        j >>= 1
        mask = (mask ^ (mask << j)) & 0xFFFFFFFF
    return a


def _chunks_of_block(blk):
    return ((blk + 1) * BLK + KEY_CHUNK - 1) // KEY_CHUNK


def _dsa_kernel(q_ref, qi_ref, tq_ref, qi_next_ref, tq_next_ref, tail_ref, k_ref, vt_ref, o_ref,
                score_ref, planes_ref, alive_ref, bias_ref, acc_ref, lg_a, lg_b, *, topk):
    i = pl.program_id(0)
    nxt = jnp.minimum(i + 1, pl.num_programs(0) - 1)
    n_chunks = _chunks_of_block(i)
    n_kc = score_ref.shape[0]

    def indexer(qi_blk_ref, tq_blk_ref, blk):
        w_t = tq_blk_ref[...].astype(F32).T[IDX_DIM:IDX_DIM + IDX_HEADS, :] * IDX_SCALE
        qi = qi_blk_ref[...]
        qi_rows = jnp.concatenate([qi[:, h * IDX_DIM:(h + 1) * IDX_DIM] for h in range(IDX_HEADS)], axis=0)

        def chunk_scores(c):
            off = pl.multiple_of(c * KEY_CHUNK, KEY_CHUNK)
            rel = jnp.maximum(_dot_nt(tail_ref[pl.ds(off, KEY_CHUNK), 0:IDX_DIM], qi_rows), 0.0)
            score = rel[:, 0:BLK] * w_t[0:1, :]
            for h in range(1, IDX_HEADS):
                score = score + rel[:, h * BLK:(h + 1) * BLK] * w_t[h:h + 1, :]
            return score

        def store_chunk(c, score):
            score = jnp.where(score == 0.0, 0.0, score)
            score_ref[c] = score
            bits = pltpu.bitcast(score, I32)
            u = bits ^ ((bits >> 31) | INT_MIN)
            for g in range(GROUPS_PER_CHUNK):
                words = [u[(g * 32 + j) * 8:(g * 32 + j + 1) * 8, :] for j in range(32)]
                planes = _bit_transpose32(words)
                for b in range(32):
                    planes_ref[c * GROUPS_PER_CHUNK + g, b] = planes[b]

        def full_chunk(c):
            store_chunk(c, chunk_scores(c))

        def last_chunk():
            last = _chunks_of_block(blk) - 1
            key_pos = last * KEY_CHUNK + lax.broadcasted_iota(I32, (KEY_CHUNK, BLK), 0)
            t_col = blk * BLK + lax.broadcasted_iota(I32, (KEY_CHUNK, BLK), 1)
            store_chunk(last, jnp.where(key_pos <= t_col, chunk_scores(last), -jnp.inf))

        return full_chunk, last_chunk

    @pl.when(i == 0)
    def _():
        indexer(qi_ref, tq_ref, 0)[1]()

    def init_alive(c, _):
        for g in range(GROUPS_PER_CHUNK):
            alive_ref[c * GROUPS_PER_CHUNK + g] = jnp.full((8, BLK), -1, I32)
        return 0

    lax.fori_loop(0, n_chunks, init_alive, 0)

    @pl.when(n_chunks % 2 == 1)
    def _():
        for g in range(GROUPS_PER_CHUNK):
            alive_ref[n_chunks * GROUPS_PER_CHUNK + g] = jnp.zeros((8, BLK), I32)
            planes_ref[n_chunks * GROUPS_PER_CHUNK + g] = jnp.zeros((32, 8, BLK), I32)

    def decide(state, counts, pair):
        above, t_bits = state[0], state[1]
        with_hi, with_both, only_lo = counts
        bit_hi = jnp.int32(1) << (31 - 2 * pair)
        bit_lo = jnp.int32(1) << (30 - 2 * pair)
        take_hi = (above + with_hi) >= topk
        above = jnp.where(take_hi, above, above + with_hi)
        with_lo = jnp.where(take_hi, with_both, only_lo)
        take_lo = (above + with_lo) >= topk
        above = jnp.where(take_lo, above, above + with_lo)
        t_bits = t_bits | jnp.where(take_hi, bit_hi, 0) | jnp.where(take_lo, bit_lo, 0)
        return above, t_bits, jnp.where(take_hi, 0, -1), jnp.where(take_lo, 0, -1)

    groups_per_step = 2 * GROUPS_PER_CHUNK

    def sweep(prev, cur, drop_hi, drop_lo):
        def step(p, accs):
            accs = list(accs)
            for g in range(groups_per_step):
                gi = p * groups_per_step + g
                alive = alive_ref[gi]
                if prev is not None:
                    alive = (alive & (planes_ref[gi, 2 * prev] ^ drop_hi)) & (planes_ref[gi, 2 * prev + 1] ^ drop_lo)
                    alive_ref[gi] = alive
                if cur is None:
                    accs[0] = accs[0] + lax.population_count(alive)
                else:
                    lo = planes_ref[gi, 2 * cur + 1]
                    with_hi = alive & planes_ref[gi, 2 * cur]
                    accs[0] = accs[0] + lax.population_count(with_hi)
                    accs[1] = accs[1] + lax.population_count(with_hi & lo)
                    accs[2] = accs[2] + lax.population_count((alive ^ with_hi) & lo)
            return tuple(accs)

        n_acc = 1 if cur is None else 3
        accs = lax.fori_loop(0, (n_chunks + 1) // 2, step, tuple(jnp.zeros((8, BLK), I32) for _ in range(n_acc)))
        return tuple(jnp.sum(a, axis=0, keepdims=True) for a in accs)

    zero = jnp.zeros((1, BLK), I32)
    state = decide((zero, zero), sweep(None, 0, None, None), 0)

    def pair_step(pair, state):
        return decide(state, sweep(pair - 1, pair, state[2], state[3]), pair)

    above, t_bits, drop_hi, drop_lo = lax.fori_loop(1, 16, pair_step, state)
    n_eq, = sweep(15, None, drop_hi, drop_lo)
    need = topk - above
    thr = t_bits ^ INT_MIN
    real = thr > KEY_OF_NEG_INF
    thr_f = jnp.where(real, _ordered_bits_to_float(thr), jnp.finfo(F32).min)

    def bias_chunk(c, _):
        bias_ref[c] = jnp.where(score_ref[c] >= thr_f, 0.0, NEG_BIG)
        return 0

    lax.fori_loop(0, n_chunks, bias_chunk, 0)

    has_tie = jnp.max(jnp.where(jnp.logical_and(real, n_eq > need), 1, 0)) > 0

    @pl.when(has_tie)
    def _():
        need_f = need.astype(F32)
        r = lax.broadcasted_iota(I32, (BLK, BLK), 0)
        cc = lax.broadcasted_iota(I32, (BLK, BLK), 1)
        upto = (cc <= r).astype(BF16)

        def tie_chunk(c, run):
            sc = score_ref[c]
            for u in range(KEY_CHUNK // BLK):
                st = sc[u * BLK:(u + 1) * BLK, :]
                eq = jnp.logical_and(st == thr_f, real)
                eqf = jnp.where(eq, 1.0, 0.0)
                rank = _dot(upto, eqf.astype(BF16)) + run
                sel = jnp.logical_or(st > thr_f, jnp.logical_and(eq, rank <= need_f))
                bias_ref[c, u * BLK:(u + 1) * BLK, :] = jnp.where(
                    real, jnp.where(sel, 0.0, NEG_BIG), jnp.where(st >= thr_f, 0.0, NEG_BIG))
                run = run + jnp.sum(eqf, axis=0, keepdims=True)
            return run

        lax.fori_loop(0, n_chunks, tie_chunk, jnp.zeros((1, BLK), F32))

    acc_ref[...] = jnp.zeros_like(acc_ref)
    bias_ref[n_kc] = jnp.full((KEY_CHUNK, BLK), NEG_BIG, F32)
    q = q_ref[...]

    def logits_into(lg, c):
        kc = jnp.minimum(c, n_chunks - 1)
        off = pl.multiple_of(kc * KEY_CHUNK, KEY_CHUNK)
        bias = bias_ref[jnp.where(c < n_chunks, c, n_kc)]
        for h in range(DSA_HEADS):
            hs = slice(h * HEAD_DIM, (h + 1) * HEAD_DIM)
            lg[h] = _dot_nt(k_ref[pl.ds(off, KEY_CHUNK), hs], q[:, hs]) + bias

    def reduce_from(lg, c, ms):
        vc = jnp.minimum(c, n_chunks - 1)
        new_m = []
        for h in range(DSA_HEADS):
            logits = lg[h]
            m_new = jnp.maximum(ms[h], jnp.max(logits, axis=0, keepdims=True))
            alpha = jnp.exp2(ms[h] - m_new)
            pr = jnp.exp2(logits - m_new).astype(BF16)
            acc_ref[h] = alpha * acc_ref[h] + _dot(vt_ref[vc, h * V_ROWS:(h + 1) * V_ROWS, :], pr)
            new_m.append(m_new)
        return tuple(new_m)

    next_full, next_last = indexer(qi_next_ref, tq_next_ref, nxt)
    last_full = _chunks_of_block(nxt) - 2
    n_pairs = (n_chunks + 1) // 2

    def attn_pair(pair, ms):
        c = 2 * pair
        logits_into(lg_b, c + 1)
        ms = reduce_from(lg_a, c, ms)
        next_full(jnp.maximum(jnp.minimum(c, last_full), 0))
        logits_into(lg_a, c + 2)
        ms = reduce_from(lg_b, c + 1, ms)
        next_full(jnp.maximum(jnp.minimum(c + 1, last_full), 0))
        return ms

    logits_into(lg_a, 0)
    lax.fori_loop(0, n_pairs, attn_pair, tuple(jnp.full((1, BLK), NEG_BIG, F32) for _ in range(DSA_HEADS)))
    for h in range(DSA_HEADS):
        weighted = acc_ref[h, 0:HEAD_DIM, :]
        denom = acc_ref[h, HEAD_DIM:HEAD_DIM + 1, :]
        o_ref[:, h * HEAD_DIM:(h + 1) * HEAD_DIM] = (weighted / denom).T.astype(o_ref.dtype)

    def rest(c, _):
        next_full(c)
        return 0

    lax.fori_loop(2 * n_pairs, last_full + 1, rest, 0)
    next_last()


def _dsa(p, k, vt):
    s = p.shape[0]
    nb = s // BLK
    topk = min(TOPK_MAX, s // 4)
    n_kc = s // KEY_CHUNK
    return pl.pallas_call(
        functools.partial(_dsa_kernel, topk=topk),
        grid=(nb,),
        in_specs=[pl.BlockSpec((BLK, D_DSA), lambda i: (i, OFF_D_Q // D_DSA)),
                  pl.BlockSpec((BLK, IDX_HEADS * IDX_DIM), lambda i: (i, OFF_D_QI // (IDX_HEADS * IDX_DIM))),
                  pl.BlockSpec((BLK, BLK), lambda i: (i, OFF_TAIL // BLK)),
                  pl.BlockSpec((BLK, IDX_HEADS * IDX_DIM),
                               lambda i: (jnp.minimum(i + 1, nb - 1), OFF_D_QI // (IDX_HEADS * IDX_DIM))),
                  pl.BlockSpec((BLK, BLK), lambda i: (jnp.minimum(i + 1, nb - 1), OFF_TAIL // BLK)),
                  _resident((s, BLK), lambda i: (0, OFF_TAIL // BLK)),
                  _resident((s, D_DSA), lambda i: (0, 0)),
                  _resident((n_kc, DSA_HEADS * V_ROWS, KEY_CHUNK), lambda i: (0, 0, 0))],
        out_specs=pl.BlockSpec((BLK, D_DSA), lambda i: (i, 0)),
        out_shape=jax.ShapeDtypeStruct((s, D_DSA), BF16),
        scratch_shapes=[pltpu.VMEM((n_kc, KEY_CHUNK, BLK), F32),
                        pltpu.VMEM((n_kc * GROUPS_PER_CHUNK, 32, 8, BLK), I32),
                        pltpu.VMEM((n_kc * GROUPS_PER_CHUNK, 8, BLK), I32),
                        pltpu.VMEM((n_kc + 1, KEY_CHUNK, BLK), F32),
                        pltpu.VMEM((DSA_HEADS, V_ROWS, BLK), F32),
                        pltpu.VMEM((DSA_HEADS, KEY_CHUNK, BLK), F32),
                        pltpu.VMEM((DSA_HEADS, KEY_CHUNK, BLK), F32)],
        compiler_params=_params("arbitrary"),
        name="dsa",
    )(p, p, p, p, p, p, k, vt)


def _layer_norm(r, g, b):
    mu = jnp.mean(r, axis=1, keepdims=True)
    var = jnp.mean(jnp.square(r - mu), axis=1, keepdims=True)
    return (r - mu) * lax.rsqrt(var + LN_EPS) * g + b


def _first_max_of4(vals):
    a, b, c, d = vals
    m = jnp.maximum(jnp.maximum(a, b), jnp.maximum(c, d))
    idx = jnp.where(a == m, 0, jnp.where(b == m, 1, jnp.where(c == m, 2, 3)))
    return m, idx


def _router_gates(logits_t):
    mx = jnp.max(logits_t, axis=0, keepdims=True)
    e = jnp.exp(logits_t - mx)
    probs = e / jnp.sum(e, axis=0, keepdims=True)
    rows = [probs[j:j + 1, :] for j in range(N_EXPERTS)]
    m1s, m2s, i1s, i2s, scores = [], [], [], [], []
    for g in range(N_GROUPS):
        vals = rows[g * EXPERTS_PER_GROUP:(g + 1) * EXPERTS_PER_GROUP]
        m1, i1 = _first_max_of4(vals)
        rest = [jnp.where(i1 == j, -1.0, vals[j]) for j in range(EXPERTS_PER_GROUP)]
        m2, i2 = _first_max_of4(rest)
        m1s.append(m1); m2s.append(m2); i1s.append(i1); i2s.append(i2); scores.append(m1 + m2)
    best, g_sel = _first_max_of4(scores)
    pick = lambda xs: jnp.where(g_sel == 0, xs[0], jnp.where(g_sel == 1, xs[1], jnp.where(g_sel == 2, xs[2], xs[3])))
    m1, m2, i1, i2 = pick(m1s), pick(m2s), pick(i1s), pick(i2s)
    den = m1 + m2
    w1, w2 = m1 / den, m2 / den
    e1 = g_sel * EXPERTS_PER_GROUP + i1
    e2 = g_sel * EXPERTS_PER_GROUP + i2
    gates = [jnp.where(e1 == j, w1, 0.0) + jnp.where(e2 == j, w2, 0.0) for j in range(N_EXPERTS)]
    return jnp.concatenate(gates, axis=0)


def _out_kernel(ysb_ref, yret_ref, ydsa_ref, wo_f32_ref, x_ref, g_ref, b_ref, wr_ref,
                x1_ref, gates_ref, gates_t_ref, cnt_ref, wo_ref):
    @pl.when(pl.program_id(0) == 0)
    def _():
        wo_ref[...] = wo_f32_ref[0].astype(BF16)

    mix = (_dot(ysb_ref[...], wo_ref[0:D_SB, :]) + _dot(yret_ref[...], wo_ref[D_SB:D_SB + D_RET, :])
           + _dot(ydsa_ref[...], wo_ref[D_SB + D_RET:, :]))
    x1 = _layer_norm(DEEPNORM_ALPHA * x_ref[...] + mix, g_ref[...], b_ref[...])
    x1_ref[...] = x1
    logits_t = lax.dot_general(wr_ref[...], x1, (((1,), (1,)), ((), ())),
                               preferred_element_type=F32, precision=lax.Precision.HIGHEST)
    gates_t = _router_gates(logits_t)
    tm = x1.shape[0]
    gates_t_ref[...] = gates_t
    padded = jnp.concatenate([gates_t, jnp.zeros((BLK - N_EXPERTS, tm), F32)], axis=0)
    gates_ref[...] = padded.T
    chosen = jnp.sum(jnp.where(gates_t > 0.0, 1, 0), axis=1, keepdims=True)
    cnt_ref[0] = jnp.broadcast_to(chosen, (N_EXPERTS, BLK))


def _out_proj(ysb, yret, ydsa, w_o, layer, x, g, b, wr_t):
    s = x.shape[0]
    tm = MOE_TILE
    row = lambda n: pl.BlockSpec((tm, n), lambda i: (i, 0))
    whole = lambda a: _resident(a.shape, lambda i: (0, 0))
    return pl.pallas_call(
        _out_kernel,
        grid=(s // tm,),
        in_specs=[row(D_SB), row(D_RET), row(D_DSA),
                  _resident((1,) + w_o.shape[1:], lambda i: (layer, 0, 0)),
                  row(D_MODEL), whole(g), whole(b), whole(wr_t)],
        out_specs=[row(D_MODEL), row(BLK), pl.BlockSpec((N_EXPERTS, tm), lambda i: (0, i)),
                   pl.BlockSpec((1, N_EXPERTS, BLK), lambda i: (i, 0, 0))],
        out_shape=[jax.ShapeDtypeStruct((s, D_MODEL), F32), jax.ShapeDtypeStruct((s, BLK), F32),
                   jax.ShapeDtypeStruct((N_EXPERTS, s), F32),
                   jax.ShapeDtypeStruct((s // tm, N_EXPERTS, BLK), I32)],
        scratch_shapes=[pltpu.VMEM(w_o.shape[1:], BF16)],
        compiler_params=_params("arbitrary"),
        name="out_proj",
    )(ysb, yret, ydsa, w_o, x, g, b, wr_t)


LOCAL_ROWS = 2 * MOE_TILE + N_EXPERTS * WIN


def _round_up(x, m):
    return (x + m - 1) // m * m


def _sorted_rows(s):
    n_tiles = s // MOE_TILE
    return _round_up(2 * s + n_tiles * N_EXPERTS * (WIN - 1) + N_EXPERTS * (ROW_TILE - 1), ROW_TILE)


def _moe_plan(cnt, s):
    n_row_tiles = _sorted_rows(s) // ROW_TILE
    seg = _round_up(cnt, WIN)
    rows_e = jnp.sum(seg, axis=0)
    region = _round_up(rows_e, ROW_TILE)
    region_off = jnp.cumsum(region) - region
    dest = region_off[None, :] + jnp.cumsum(seg, axis=0) - seg
    tiles_e = region // ROW_TILE
    tile_end = jnp.cumsum(tiles_e)
    k = jnp.arange(n_row_tiles, dtype=I32)
    tile_expert = jnp.minimum(jnp.sum((k[:, None] >= tile_end[None, :]).astype(I32), axis=1), N_EXPERTS - 1)
    first = (tile_end - tiles_e)[tile_expert]
    valid = jnp.clip(rows_e[tile_expert] - (k - first) * ROW_TILE, 0, ROW_TILE)
    tile_valid = jnp.where(k < tile_end[-1], valid, 0).astype(I32)
    fill = jnp.concatenate([region_off + rows_e, (region - rows_e) // WIN,
                            tile_end[-1:], n_row_tiles - tile_end[-1:]]).astype(I32)
    used = rows_e > 0
    e_id = jnp.arange(N_EXPERTS, dtype=I32)
    buffer_e = (jnp.cumsum(used) - used) % 2
    later_used = jnp.where(jnp.logical_and(used[None, :], e_id[None, :] > e_id[:, None]), e_id[None, :], N_EXPERTS)
    next_e = jnp.min(later_used, axis=1)
    next_e = jnp.where(next_e == N_EXPERTS, -1, next_e)
    tile_info = jnp.stack([tile_expert, tile_valid, buffer_e[tile_expert], next_e[tile_expert]]).astype(I32)
    return cnt.reshape(-1).astype(I32), dest.reshape(-1).astype(I32), fill, tile_info.reshape(-1)


def _slot_offsets(cnt_sm, tile):
    offs, o = [], 0
    for e in range(N_EXPERTS):
        offs.append(o)
        o = o + _round_up(cnt_sm[tile * N_EXPERTS + e], WIN)
    return offs


def _window_copies(cnt_sm, dest_sm, tile, offs, local_ref, sorted_hbm, sem, to_sorted):
    total = 0
    for e in range(N_EXPERTS):
        n_win = (cnt_sm[tile * N_EXPERTS + e] + WIN - 1) // WIN
        base_local, base_sorted = offs[e], dest_sm[tile * N_EXPERTS + e]

        def issue(j, _, base_local=base_local, base_sorted=base_sorted):
            loc = local_ref.at[pl.ds(pl.multiple_of(base_local + j * WIN, WIN), WIN)]
            srt = sorted_hbm.at[pl.ds(pl.multiple_of(base_sorted + j * WIN, WIN), WIN)]
            if to_sorted:
                pltpu.make_async_copy(loc, srt, sem).start()
            else:
                pltpu.make_async_copy(srt, loc, sem).start()
            return 0

        lax.fori_loop(0, n_win, issue, 0)
        total = total + n_win
    return total


def _n_windows(cnt_sm, tile):
    total = 0
    for e in range(N_EXPERTS):
        total = total + (cnt_sm[tile * N_EXPERTS + e] + WIN - 1) // WIN
    return total


def _wait_windows(total, local_ref, sorted_hbm, sem):
    def wait(j, _):
        pltpu.make_async_copy(local_ref.at[pl.ds(0, WIN)], sorted_hbm.at[pl.ds(0, WIN)], sem).wait()
        return 0

    lax.fori_loop(0, total, wait, 0)


def _zero_fill(fill_sm, xs_hbm, zero_ref, sem_win, sem_tile):
    zero_ref[...] = jnp.zeros_like(zero_ref)
    win_copy = lambda row: pltpu.make_async_copy(
        zero_ref.at[pl.ds(0, WIN)], xs_hbm.at[pl.ds(pl.multiple_of(row, WIN), WIN)], sem_win)
    tile_copy = lambda row: pltpu.make_async_copy(
        zero_ref, xs_hbm.at[pl.ds(pl.multiple_of(row, ROW_TILE), ROW_TILE)], sem_tile)
    n_pad = 0
    for e in range(N_EXPERTS):
        first, n_win = fill_sm[e], fill_sm[N_EXPERTS + e]

        def issue(j, _, first=first):
            win_copy(first + j * WIN).start()
            return 0

        lax.fori_loop(0, n_win, issue, 0)
        n_pad = n_pad + n_win
    first_tile, n_tail = fill_sm[2 * N_EXPERTS], fill_sm[2 * N_EXPERTS + 1]

    def issue_tile(j, _):
        tile_copy((first_tile + j) * ROW_TILE).start()
        return 0

    def wait_win(j, _):
        win_copy(0).wait()
        return 0

    def wait_tile(j, _):
        tile_copy(0).wait()
        return 0

    lax.fori_loop(0, n_tail, issue_tile, 0)
    lax.fori_loop(0, n_pad, wait_win, 0)
    lax.fori_loop(0, n_tail, wait_tile, 0)


def _dispatch_kernel(cnt_sm, dest_sm, fill_sm, x_ref, gt_ref, xs_hbm, local_ref, zero_ref, sem, sem_win, sem_tile):
    tile = pl.program_id(0)

    @pl.when(tile == 0)
    def _():
        _zero_fill(fill_sm, xs_hbm, zero_ref, sem_win, sem_tile)

    offs = _slot_offsets(cnt_sm, tile)
    chosen = gt_ref[...] > 0.0
    t_r = lax.broadcasted_iota(I32, (MOE_TILE, MOE_TILE), 0)
    t_c = lax.broadcasted_iota(I32, (MOE_TILE, MOE_TILE), 1)
    earlier = _dot(jnp.where(chosen, 1.0, 0.0).astype(BF16), (t_r < t_c).astype(BF16))
    e_id = lax.broadcasted_iota(I32, (N_EXPERTS, 1), 0)
    slot = jnp.zeros((N_EXPERTS, 1), I32)
    for e in range(N_EXPERTS):
        slot = jnp.where(e_id == e, offs[e], slot)
    pos = earlier + slot.astype(F32)
    p_lo = jnp.min(jnp.where(chosen, pos, float(LOCAL_ROWS)), axis=0, keepdims=True).astype(I32)
    p_hi = jnp.max(jnp.where(chosen, pos, -1.0), axis=0, keepdims=True).astype(I32)
    row = lax.broadcasted_iota(I32, (LOCAL_ROWS, MOE_TILE), 0)
    onehot = jnp.where(row == p_lo, 1.0, jnp.where(row == p_hi, 1.0, 0.0)).astype(BF16)
    buf = tile % 2
    mine, mine_sem = local_ref.at[buf], sem.at[buf]
    other, other_sem = local_ref.at[1 - buf], sem.at[1 - buf]
    mine[...] = _dot(onehot, x_ref[...].astype(BF16)).astype(BF16)
    total = _window_copies(cnt_sm, dest_sm, tile, offs, mine, xs_hbm, mine_sem, True)

    @pl.when(tile >= 1)
    def _():
        _wait_windows(_n_windows(cnt_sm, tile - 1), other, xs_hbm, other_sem)

    @pl.when(tile == pl.num_programs(0) - 1)
    def _():
        _wait_windows(total, mine, xs_hbm, mine_sem)


def _dispatch(x1, gates_t, cnt_flat, dest_flat, fill):
    s = x1.shape[0]
    return pl.pallas_call(
        _dispatch_kernel,
        grid_spec=pltpu.PrefetchScalarGridSpec(
            num_scalar_prefetch=3,
            grid=(s // MOE_TILE,),
            in_specs=[pl.BlockSpec((MOE_TILE, D_MODEL), lambda i, *_: (i, 0)),
                      pl.BlockSpec((N_EXPERTS, MOE_TILE), lambda i, *_: (0, i))],
            out_specs=pl.BlockSpec(memory_space=pl.ANY),
            scratch_shapes=[pltpu.VMEM((2, LOCAL_ROWS, D_MODEL), BF16), pltpu.VMEM((ROW_TILE, D_MODEL), BF16),
                            pltpu.SemaphoreType.DMA((2,)), pltpu.SemaphoreType.DMA(()),
                            pltpu.SemaphoreType.DMA(())]),
        out_shape=jax.ShapeDtypeStruct((_sorted_rows(s), D_MODEL), BF16),
        compiler_params=_params("arbitrary"),
        name="moe_dispatch",
    )(cnt_flat, dest_flat, fill, x1, gates_t)


def _expert_kernel(info_sm, xs_ref, wg_hbm, wu_hbm, wd_hbm, y_ref, wg_f, wu_f, wd_f, wg_b, wu_b, wd_b, sem, *,
                   layer):
    k = pl.program_id(0)
    n = pl.num_programs(0)
    expert, valid, buf, next_expert = info_sm[k], info_sm[n + k], info_sm[2 * n + k], info_sm[3 * n + k]
    new_expert = jnp.logical_or(k == 0, expert != info_sm[jnp.maximum(k - 1, 0)])

    def weight_copies(e, b):
        return (pltpu.make_async_copy(wg_hbm.at[layer, e], wg_f.at[b], sem.at[b, 0]),
                pltpu.make_async_copy(wu_hbm.at[layer, e], wu_f.at[b], sem.at[b, 1]),
                pltpu.make_async_copy(wd_hbm.at[layer, e], wd_f.at[b], sem.at[b, 2]))

    @pl.when(jnp.logical_and(k == 0, valid > 0))
    def _():
        for c in weight_copies(expert, buf):
            c.start()

    @pl.when(jnp.logical_and(valid > 0, new_expert))
    def _():
        for c in weight_copies(expert, buf):
            c.wait()
        wg_b[...] = wg_f[buf].astype(BF16)
        wu_b[...] = wu_f[buf].astype(BF16)
        wd_b[...] = wd_f[buf].astype(BF16)

        @pl.when(next_expert >= 0)
        def _():
            for c in weight_copies(next_expert, 1 - buf):
                c.start()

    @pl.when(valid > 0)
    def _():
        x = xs_ref[...]
        hg = _dot(x, wg_b[...])
        hu = _dot(x, wu_b[...])
        act = hg * jax.nn.sigmoid(hg) * hu
        y_ref[...] = _dot(act.astype(BF16), wd_b[...]).astype(y_ref.dtype)

    @pl.when(valid == 0)
    def _():
        y_ref[...] = jnp.zeros_like(y_ref)


def _experts(xs, tile_info, w_gate, w_up, w_down, layer):
    rows = xs.shape[0]
    hbm = pl.BlockSpec(memory_space=pl.ANY)
    up_shape, down_shape = (D_MODEL, D_FF_EXPERT), (D_FF_EXPERT, D_MODEL)
    return pl.pallas_call(
        functools.partial(_expert_kernel, layer=layer),
        grid_spec=pltpu.PrefetchScalarGridSpec(
            num_scalar_prefetch=1,
            grid=(rows // ROW_TILE,),
            in_specs=[pl.BlockSpec((ROW_TILE, D_MODEL), lambda k, info: (k, 0)), hbm, hbm, hbm],
            out_specs=pl.BlockSpec((ROW_TILE, D_MODEL), lambda k, info: (k, 0)),
            scratch_shapes=[pltpu.VMEM((2,) + up_shape, F32), pltpu.VMEM((2,) + up_shape, F32),
                            pltpu.VMEM((2,) + down_shape, F32),
                            pltpu.VMEM(up_shape, BF16), pltpu.VMEM(up_shape, BF16), pltpu.VMEM(down_shape, BF16),
                            pltpu.SemaphoreType.DMA((2, 3))]),
        out_shape=jax.ShapeDtypeStruct((rows, D_MODEL), BF16),
        compiler_params=_params("arbitrary"),
        name="moe_experts",
    )(tile_info, xs, w_gate, w_up, w_down)


def _combine_kernel(cnt_sm, dest_sm, x_ref, gates_ref, y_hbm, g_ref, b_ref, o_ref, local_ref, sem):
    tile = pl.program_id(0)

    buf = tile % 2
    mine, mine_sem = local_ref.at[buf], sem.at[buf]
    offs = _slot_offsets(cnt_sm, tile)

    @pl.when(tile == 0)
    def _():
        local_ref[...] = jnp.zeros_like(local_ref)
        _window_copies(cnt_sm, dest_sm, tile, offs, mine, y_hbm, mine_sem, False)

    @pl.when(tile + 1 < pl.num_programs(0))
    def _():
        _window_copies(cnt_sm, dest_sm, tile + 1, _slot_offsets(cnt_sm, tile + 1),
                       local_ref.at[1 - buf], y_hbm, sem.at[1 - buf], False)

    gates = gates_ref[...]
    chosen = gates > 0.0
    t_r = lax.broadcasted_iota(I32, (MOE_TILE, MOE_TILE), 0)
    t_c = lax.broadcasted_iota(I32, (MOE_TILE, MOE_TILE), 1)
    earlier = _dot((t_c < t_r).astype(BF16), jnp.where(chosen, 1.0, 0.0).astype(BF16))
    e_id = lax.broadcasted_iota(I32, (1, BLK), 1)
    slot = jnp.zeros((1, BLK), I32)
    for e in range(N_EXPERTS):
        slot = jnp.where(e_id == e, offs[e], slot)
    pos = jnp.where(chosen, earlier + slot.astype(F32), -1.0)
    p_lo = jnp.min(jnp.where(chosen, pos, float(LOCAL_ROWS)), axis=1, keepdims=True)
    p_hi = jnp.max(pos, axis=1, keepdims=True)
    w_lo = jnp.sum(jnp.where(pos == p_lo, gates, 0.0), axis=1, keepdims=True)
    w_hi = jnp.sum(jnp.where(pos == p_hi, gates, 0.0), axis=1, keepdims=True)
    col = lax.broadcasted_iota(I32, (MOE_TILE, LOCAL_ROWS), 1)
    weights = jnp.where(col == p_lo.astype(I32), w_lo, jnp.where(col == p_hi.astype(I32), w_hi, 0.0))

    _wait_windows(_n_windows(cnt_sm, tile), mine, y_hbm, mine_sem)
    ffn = _dot(weights.astype(BF16), mine[...])
    o_ref[...] = _layer_norm(DEEPNORM_ALPHA * x_ref[...] + ffn, g_ref[...], b_ref[...])


def _combine(x1, gates, y, cnt_flat, dest_flat, g, b):
    s = x1.shape[0]
    row = lambda n: pl.BlockSpec((MOE_TILE, n), lambda i, *_: (i, 0))
    vec = pl.BlockSpec((1, D_MODEL), lambda i, *_: (0, 0))
    return pl.pallas_call(
        _combine_kernel,
        grid_spec=pltpu.PrefetchScalarGridSpec(
            num_scalar_prefetch=2,
            grid=(s // MOE_TILE,),
            in_specs=[row(D_MODEL), row(BLK), pl.BlockSpec(memory_space=pl.ANY), vec, vec],
            out_specs=row(D_MODEL),
            scratch_shapes=[pltpu.VMEM((2, LOCAL_ROWS, D_MODEL), BF16), pltpu.SemaphoreType.DMA((2,))]),
        out_shape=jax.ShapeDtypeStruct((s, D_MODEL), F32),
        compiler_params=_params("arbitrary"),
        name="moe_combine",
    )(cnt_flat, dest_flat, x1, gates, y, g, b)


def _moe(x1, gates, gates_t, cnt, w_gate, w_up, w_down, g, b, layer):
    s = x1.shape[0]
    cnt_flat, dest_flat, fill, tile_info = _moe_plan(cnt[:, :, 0], s)
    xs = _dispatch(x1, gates_t, cnt_flat, dest_flat, fill)
    y = _experts(xs, tile_info, w_gate, w_up, w_down, layer)
    return _combine(x1, gates, y, cnt_flat, dest_flat, g, b)


def _reorder_w_in(w_in):
    pieces, o = {}, 0
    for name, width in (("sb", 3 * D_SB), ("ret", 4 * D_RET), ("d_q", D_DSA), ("d_ckv", KV_RANK),
                        ("d_qi", IDX_HEADS * IDX_DIM)):
        pieces[name] = w_in[:, :, o:o + width]
        o += width
    parts = [pieces["sb"], pieces["d_q"], pieces["ret"], pieces["d_qi"], pieces["d_ckv"], w_in[:, :, o:]]
    width = sum(a.shape[2] for a in parts)
    parts.append(jnp.zeros(w_in.shape[:2] + (D_PROJ - width,), w_in.dtype))
    return jnp.concatenate(parts, axis=2).astype(BF16)


def kernel(x, w_in, w_kv_up, kv_norm_g, ret_gn_g, w_o, ln1_g, ln1_b, w_router, w_gate, w_up, w_down,
           ln2_g, ln2_b):
    b, s, _ = x.shape
    assert b == 1 and s % KEY_CHUNK == 0
    h = x[0]
    tables = _retention_tables(s)
    wr_t = w_router.T
    w_proj = _reorder_w_in(w_in)
    for l in range(DEPTH):
        p = _proj(h, w_proj, l)
        y_sb = _stick_breaking(p)
        y_ret = _retention(p, ret_gn_g[l][None, :], tables)
        w_kv = w_kv_up[l].reshape(KV_RANK, DSA_HEADS, 2, HEAD_DIM)
        w_kv = jnp.concatenate([w_kv[:, :, 0, :].reshape(KV_RANK, D_DSA),
                                w_kv[:, :, 1, :].reshape(KV_RANK, D_DSA)], axis=1).astype(BF16)
        k_dsa, vt_dsa = _kv_up(p, kv_norm_g[l][None, :], w_kv)
        y_dsa = _dsa(p, k_dsa, vt_dsa)
        x1, gates, gates_t, cnt = _out_proj(y_sb, y_ret, y_dsa, w_o, l, h, ln1_g[l][None, :], ln1_b[l][None, :],
                                            wr_t)
        h = _moe(x1, gates, gates_t, cnt, w_gate, w_up, w_down, ln2_g[l][None, :], ln2_b[l][None, :], l)
    return h[None]
```
